```python
import math
import jax, jax.numpy as jnp
from jax import lax
import numpy as np

D_MODEL = 1024
BATCH = 4
SEQ = 8192
DEPTH = 2

N_EVEN = (DEPTH + 1) // 2
N_ODD = DEPTH // 2

NSA_HEADS = 8
NSA_KV_HEADS = 2
NSA_GROUP = NSA_HEADS // NSA_KV_HEADS
NSA_HEAD_DIM = 64
CMP_BLOCK = 32
CMP_STRIDE = 16
SLC_BLOCK = 64
SLC_TOP_N = 16
SLC_LOCAL = 2
WINDOW = 512
Q_BLOCK = 128
FORCED_SCORE = 1e6
GDN_HEADS = 4
GDN_HEAD_DIM = 128
GDN_CONV = 4
GDN_CHUNK = 64
POOL_SIZES = (2, 4, 8, 16)
POOL_GROUP = D_MODEL // 4
D_FF = 2816
N_EXPERTS = 8
TOP_K = 2
D_FF_EXPERT = 3584
ROPE_THETA = 10000.0
LN_EPS = 1e-5
NORM_EPS = 1e-6
DN_ALPHA = (2 * DEPTH) ** 0.25
DN_BETA = (8 * DEPTH) ** -0.25

NSA_Q = NSA_HEADS * NSA_HEAD_DIM
NSA_KV = NSA_KV_HEADS * NSA_HEAD_DIM
NSA_GATES = 3 * NSA_HEADS
GDN_W = GDN_HEADS * GDN_HEAD_DIM
IN_SPLITS = (NSA_Q,) + (NSA_KV,) * 6 + (NSA_GATES,) + (GDN_W,) * 4 + (GDN_HEADS, GDN_HEADS)
D_IN = sum(IN_SPLITS)
D_MIX = NSA_Q + GDN_W

kernel_name = "hybrid_nsa_gdn_pool_moe"

F32 = jnp.float32


def layer_norm(x, g, b):
    x32 = x.astype(F32)
    mu = jnp.mean(x32, -1, keepdims=True)
    var = jnp.mean(jnp.square(x32 - mu), -1, keepdims=True)
    return ((x32 - mu) * lax.rsqrt(var + LN_EPS) * g + b).astype(x.dtype)


def rope(x, pos):
    half = x.shape[-1] // 2
    inv = jnp.power(ROPE_THETA, -jnp.arange(half, dtype=F32) / half)
    ang = pos.astype(F32)[:, None] * inv[None, :]
    cos = jnp.cos(ang)[:, None, :]
    sin = jnp.sin(ang)[:, None, :]
    x1 = x[..., :half].astype(F32)
    x2 = x[..., half:].astype(F32)
    return jnp.concatenate([x1 * cos - x2 * sin, x2 * cos + x1 * sin], -1).astype(x.dtype)


def masked_softmax(s, mask):
    s = jnp.where(mask, s.astype(F32), -jnp.inf)
    m = jnp.max(s, -1, keepdims=True)
    m = jnp.where(jnp.isfinite(m), m, 0.0)
    e = jnp.exp(s - m)
    d = jnp.sum(e, -1, keepdims=True)
    return e / jnp.where(d > 0, d, 1.0)


def nsa_mixer(q, k_cmp, v_cmp, k_slc, v_slc, k_win, v_win, gate_logits, pe_k, w_k, pe_v, w_v):
    B, S = q.shape[:2]
    H, HKV, G, D = NSA_HEADS, NSA_KV_HEADS, NSA_GROUP, NSA_HEAD_DIM
    scale = D ** -0.5
    kv = lambda t: t.reshape(B, S, HKV, D)
    k_cmp, v_cmp, k_slc, v_slc, k_win, v_win = map(kv, (k_cmp, v_cmp, k_slc, v_slc, k_win, v_win))
    pos = jnp.arange(S, dtype=F32)
    q = rope(q.reshape(B, S, H, D), pos)
    k_slc = rope(k_slc, pos)
    k_win = rope(k_win, pos)

    n_cmp = (S - CMP_BLOCK) // CMP_STRIDE + 1
    starts = jnp.arange(n_cmp) * CMP_STRIDE
    blk_idx = starts[:, None] + jnp.arange(CMP_BLOCK)[None, :]
    cmp_end = starts + CMP_BLOCK - 1

    def compress(t, pe, w):
        tb = t[:, blk_idx] + pe[None, None, :, None, :]
        return jnp.einsum('bnlhd,lde->bnhe', tb, w)

    kc = rope(compress(k_cmp, pe_k, w_k), starts.astype(F32) + (CMP_BLOCK - 1) / 2)
    vc = compress(v_cmp, pe_v, w_v)
    kc = kc.transpose(0, 2, 1, 3)
    vc = vc.transpose(0, 2, 1, 3)

    n_slc = S // SLC_BLOCK
    n_sel = min(SLC_TOP_N, n_slc)
    slc_start = jnp.arange(n_slc) * SLC_BLOCK
    overlap = ((starts[:, None] <= slc_start[None, :] + SLC_BLOCK - 1)
               & (cmp_end[:, None] >= slc_start[None, :])).astype(F32)
    ks_blk = k_slc.reshape(B, n_slc, SLC_BLOCK, HKV, D).transpose(0, 3, 1, 2, 4)
    vs_blk = v_slc.reshape(B, n_slc, SLC_BLOCK, HKV, D).transpose(0, 3, 1, 2, 4)

    pad = ((0, 0), (0, 0), (WINDOW, 0), (0, 0))
    kw_pad = jnp.pad(k_win.transpose(0, 2, 1, 3), pad)
    vw_pad = jnp.pad(v_win.transpose(0, 2, 1, 3), pad)

    qh = q.reshape(B, S, HKV, G, D).transpose(0, 2, 3, 1, 4)
    bi = jnp.arange(B)[:, None, None, None]
    hi = jnp.arange(HKV)[None, :, None, None]
    j = jnp.arange(n_slc)

    def query_block(qb):
        s0 = qb * Q_BLOCK
        t = s0 + jnp.arange(Q_BLOCK)
        qblk = lax.dynamic_slice_in_dim(qh, s0, Q_BLOCK, axis=3)

        s_c = jnp.einsum('bhgqd,bhnd->bhgqn', qblk, kc).astype(F32) * scale
        p_c = masked_softmax(s_c, cmp_end[None, :] <= t[:, None])
        o_c = jnp.einsum('bhgqn,bhnd->bhgqd', p_c.astype(vc.dtype), vc)

        imp = jnp.einsum('bhgqn,nm->bhqm', p_c, overlap)
        cur = (t // SLC_BLOCK)[:, None]
        forced = (j[None, :] == 0) | ((j[None, :] <= cur) & (j[None, :] > cur - SLC_LOCAL))
        causal_blk = slc_start[None, :] <= t[:, None]
        score = jnp.where(causal_blk, jnp.where(forced, FORCED_SCORE, imp), -1.0)
        _, sel = lax.top_k(score, n_sel)

        kg = ks_blk[bi, hi, sel].reshape(B, HKV, Q_BLOCK, n_sel * SLC_BLOCK, D)
        vg = vs_blk[bi, hi, sel].reshape(B, HKV, Q_BLOCK, n_sel * SLC_BLOCK, D)
        kpos = (sel[..., None] * SLC_BLOCK + jnp.arange(SLC_BLOCK)).reshape(B, HKV, Q_BLOCK, n_sel * SLC_BLOCK)
        s_s = jnp.einsum('bhgqd,bhqkd->bhgqk', qblk, kg).astype(F32) * scale
        p_s = masked_softmax(s_s, (kpos <= t[:, None])[:, :, None])
        o_s = jnp.einsum('bhgqk,bhqkd->bhgqd', p_s.astype(vg.dtype), vg)

        kwb = lax.dynamic_slice_in_dim(kw_pad, s0, WINDOW + Q_BLOCK, axis=2)
        vwb = lax.dynamic_slice_in_dim(vw_pad, s0, WINDOW + Q_BLOCK, axis=2)
        wpos = s0 - WINDOW + jnp.arange(WINDOW + Q_BLOCK)
        rel = t[:, None] - wpos[None, :]
        m_w = (rel >= 0) & (rel < WINDOW) & (wpos[None, :] >= 0)
        s_w = jnp.einsum('bhgqd,bhkd->bhgqk', qblk, kwb).astype(F32) * scale
        p_w = masked_softmax(s_w, m_w)
        o_w = jnp.einsum('bhgqk,bhkd->bhgqd', p_w.astype(vwb.dtype), vwb)

        return jnp.stack([o_c.astype(F32), o_s.astype(F32), o_w.astype(F32)], axis=-2)

    o = lax.map(query_block, jnp.arange(S // Q_BLOCK))
    o = o.transpose(1, 0, 4, 2, 3, 5, 6).reshape(B, S, H, 3, D)
    gates = jax.nn.sigmoid(gate_logits.astype(F32)).reshape(B, S, H, 3)
    out = jnp.einsum('bshc,bshcd->bshd', gates, o)
    return out.reshape(B, S, H * D).astype(q.dtype)


def causal_dwconv(x, w):
    C = x.shape[-1]
    return lax.conv_general_dilated(x, w[:, None, :].astype(x.dtype), window_strides=(1,),
                                    padding=[(w.shape[0] - 1, 0)],
                                    dimension_numbers=('NWC', 'WIO', 'NWC'),
                                    feature_group_count=C)


def l2norm(x):
    return x * lax.rsqrt(jnp.sum(x * x, -1, keepdims=True) + NORM_EPS)


def gdn_mixer(q, k, v, z, b, a, conv_w, a_log, dt_bias, norm_w):
    B, S = q.shape[:2]
    H, D, C = GDN_HEADS, GDN_HEAD_DIM, GDN_CHUNK
    NC = S // C
    qkv = jax.nn.silu(causal_dwconv(jnp.concatenate([q, k, v], -1), conv_w)).astype(F32)
    q, k, v = jnp.split(qkv, 3, axis=-1)
    q = l2norm(q.reshape(B, S, H, D)) * (D ** -0.5)
    k = l2norm(k.reshape(B, S, H, D))
    v = v.reshape(B, S, H, D)
    beta = jax.nn.sigmoid(b.astype(F32))
    g = -jnp.exp(a_log.astype(F32)) * jax.nn.softplus(a.astype(F32) + dt_bias.astype(F32))

    chunk4 = lambda t: t.reshape(B, NC, C, H, D).transpose(0, 3, 1, 2, 4)
    chunk3 = lambda t: t.reshape(B, NC, C, H).transpose(0, 3, 1, 2)
    q, k, v = chunk4(q), chunk4(k), chunk4(v)
    beta = chunk3(beta)
    gc = jnp.cumsum(chunk3(g), axis=-1)

    idx = jnp.arange(C)
    tril = idx[:, None] >= idx[None, :]
    strict = idx[:, None] > idx[None, :]
    decay = jnp.exp(jnp.where(tril, gc[..., :, None] - gc[..., None, :], -jnp.inf))
    kb = k * beta[..., None]
    lmat = jnp.where(strict, jnp.einsum('bhnid,bhnjd->bhnij', kb, k) * decay, 0.0)
    amat = lmat + jnp.eye(C, dtype=F32)
    u = lax.linalg.triangular_solve(amat, v * beta[..., None], left_side=True, lower=True, unit_diagonal=True)
    w = lax.linalg.triangular_solve(amat, kb * jnp.exp(gc)[..., None], left_side=True, lower=True, unit_diagonal=True)
    intra = jnp.einsum('bhnid,bhnjd->bhnij', q, k) * decay

    def step(state, inp):
        qi, ki, ui, wi, gi, ai = inp
        v_new = ui - jnp.einsum('bhck,bhkv->bhcv', wi, state)
        o = (jnp.einsum('bhck,bhkv->bhcv', qi * jnp.exp(gi)[..., None], state)
             + jnp.einsum('bhij,bhjv->bhiv', ai, v_new))
        g_last = gi[..., -1]
        state = (state * jnp.exp(g_last)[..., None, None]
                 + jnp.einsum('bhck,bhcv->bhkv', ki * jnp.exp(g_last[..., None] - gi)[..., None], v_new))
        return state, o

    mv = lambda t: jnp.moveaxis(t, 2, 0)
    state0 = jnp.zeros((B, H, D, D), F32)
    _, o = lax.scan(step, state0, (mv(q), mv(k), mv(u), mv(w), mv(gc), mv(intra)))
    o = o.transpose(1, 0, 3, 2, 4).reshape(B, S, H, D)
    o = o * lax.rsqrt(jnp.mean(o * o, -1, keepdims=True) + NORM_EPS) * norm_w.astype(F32)
    o = o * jax.nn.silu(z.reshape(B, S, H, D).astype(F32))
    return o.reshape(B, S, H * D).astype(z.dtype)


def hybrid_attention_mixer(x, w_in, pe_k, w_k, pe_v, w_v, conv_w, a_log, dt_bias, gdn_norm, w_out):
    h = jnp.einsum('bsd,de->bse', x, w_in)
    offs = [sum(IN_SPLITS[:i + 1]) for i in range(len(IN_SPLITS) - 1)]
    (nq, kc, vc, ks, vs, kw, vw, gl, gq, gk, gv, gz, gb, ga) = jnp.split(h, offs, axis=-1)
    o_a = nsa_mixer(nq, kc, vc, ks, vs, kw, vw, gl, pe_k, w_k, pe_v, w_v)
    o_b = gdn_mixer(gq, gk, gv, gz, gb, ga, conv_w, a_log, dt_bias, gdn_norm)
    return jnp.einsum('bse,ed->bsd', jnp.concatenate([o_a, o_b], -1), w_out)


def pool_mixer(x, pool_w, pool_scale):
    B, S, _ = x.shape
    x32 = x.astype(F32)
    csum = jnp.concatenate([jnp.zeros((B, 1, D_MODEL), F32), jnp.cumsum(x32, axis=1)], axis=1)
    t = jnp.arange(S)
    outs = []
    for gi, win in enumerate(POOL_SIZES):
        sl = slice(gi * POOL_GROUP, (gi + 1) * POOL_GROUP)
        c = csum[:, :, sl]
        lo = jnp.maximum(t + 1 - win, 0)
        cnt = jnp.minimum(t + 1, win).astype(F32)[None, :, None]
        mean = (c[:, 1:] - c[:, lo]) / cnt
        outs.append(jnp.einsum('bsc,cd->bsd', mean - x32[:, :, sl], pool_w[gi].astype(F32)))
    return (jnp.concatenate(outs, -1) * pool_scale.astype(F32)).astype(x.dtype)


def swiglu(x, wg, wu, wd):
    return jnp.dot(jax.nn.silu(jnp.dot(x, wg)) * jnp.dot(x, wu), wd)


def moe_ffn(x, router_w, wg, wu, wd):
    logits = jnp.einsum('bsd,de->bse', x, router_w).astype(F32)
    probs = jax.nn.softmax(logits, -1)
    top_p, top_i = lax.top_k(probs, TOP_K)
    top_p = top_p / jnp.sum(top_p, -1, keepdims=True)
    gate = jnp.sum(jax.nn.one_hot(top_i, N_EXPERTS, dtype=F32) * top_p[..., None], axis=-2)
    y = jnp.zeros(x.shape, F32)
    for e in range(N_EXPERTS):
        y = y + gate[..., e:e + 1] * swiglu(x, wg[e], wu[e], wd[e]).astype(F32)
    return y.astype(x.dtype)


def setup_inputs(seed: int = 0) -> dict:
    key = jax.random.key(seed)
    ks = jax.random.split(key, 32)
    nrm = lambda i, shape, s: jax.random.normal(ks[i], shape, F32) * s
    gain = lambda i, shape: 1.0 + nrm(i, shape, 0.02)
    NE, NO = N_EVEN, N_ODD
    dt = jnp.exp(jax.random.uniform(ks[8], (NE, GDN_HEADS), F32, math.log(1e-3), math.log(1e-1)))
    return {
        'x': nrm(0, (BATCH, SEQ, D_MODEL), 1.0),
        'ev_w_in': nrm(1, (NE, D_MODEL, D_IN), D_MODEL ** -0.5),
        'ev_cmp_pe_k': nrm(2, (NE, CMP_BLOCK, NSA_HEAD_DIM), 0.02),
        'ev_cmp_w_k': nrm(3, (NE, CMP_BLOCK, NSA_HEAD_DIM, NSA_HEAD_DIM), (CMP_BLOCK * NSA_HEAD_DIM) ** -0.5),
        'ev_cmp_pe_v': nrm(4, (NE, CMP_BLOCK, NSA_HEAD_DIM), 0.02),
        'ev_cmp_w_v': nrm(5, (NE, CMP_BLOCK, NSA_HEAD_DIM, NSA_HEAD_DIM), (CMP_BLOCK * NSA_HEAD_DIM) ** -0.5),
        'ev_conv_w': nrm(6, (NE, GDN_CONV, 3 * GDN_W), GDN_CONV ** -0.5),
        'ev_a_log': jnp.log(jax.random.uniform(ks[7], (NE, GDN_HEADS), F32, 1.0, 16.0)),
        'ev_dt_bias': dt + jnp.log(-jnp.expm1(-dt)),
        'ev_gdn_norm': gain(9, (NE, GDN_HEAD_DIM)),
        'ev_w_out': nrm(10, (NE, D_MIX, D_MODEL), DN_BETA * D_MIX ** -0.5),
        'ev_ln1_g': gain(11, (NE, D_MODEL)),
        'ev_ln1_b': nrm(12, (NE, D_MODEL), 0.02),
        'ev_ffn_wg': nrm(13, (NE, D_MODEL, D_FF), D_MODEL ** -0.5),
        'ev_ffn_wu': nrm(14, (NE, D_MODEL, D_FF), D_MODEL ** -0.5),
        'ev_ffn_wd': nrm(15, (NE, D_FF, D_MODEL), DN_BETA * D_FF ** -0.5),
        'ev_ln2_g': gain(16, (NE, D_MODEL)),
        'ev_ln2_b': nrm(17, (NE, D_MODEL), 0.02),
        'od_pool_w': nrm(18, (NO, len(POOL_SIZES), POOL_GROUP, POOL_GROUP), DN_BETA * POOL_GROUP ** -0.5),
        'od_pool_scale': gain(19, (NO, D_MODEL)),
        'od_ln1_g': gain(20, (NO, D_MODEL)),
        'od_ln1_b': nrm(21, (NO, D_MODEL), 0.02),
        'od_router_w': nrm(22, (NO, D_MODEL, N_EXPERTS), D_MODEL ** -0.5),
        'od_exp_wg': nrm(23, (NO, N_EXPERTS, D_MODEL, D_FF_EXPERT), D_MODEL ** -0.5),
        'od_exp_wu': nrm(24, (NO, N_EXPERTS, D_MODEL, D_FF_EXPERT), D_MODEL ** -0.5),
        'od_exp_wd': nrm(25, (NO, N_EXPERTS, D_FF_EXPERT, D_MODEL), DN_BETA * D_FF_EXPERT ** -0.5),
        'od_ln2_g': gain(26, (NO, D_MODEL)),
        'od_ln2_b': nrm(27, (NO, D_MODEL), 0.02),
    }


def reference(x, ev_w_in, ev_cmp_pe_k, ev_cmp_w_k, ev_cmp_pe_v, ev_cmp_w_v, ev_conv_w, ev_a_log,
              ev_dt_bias, ev_gdn_norm, ev_w_out, ev_ln1_g, ev_ln1_b, ev_ffn_wg, ev_ffn_wu, ev_ffn_wd,
              ev_ln2_g, ev_ln2_b, od_pool_w, od_pool_scale, od_ln1_g, od_ln1_b, od_router_w,
              od_exp_wg, od_exp_wu, od_exp_wd, od_ln2_g, od_ln2_b):
    for layer in range(DEPTH):
        i = layer // 2
        if layer % 2 == 0:
            mix = hybrid_attention_mixer(x, ev_w_in[i], ev_cmp_pe_k[i], ev_cmp_w_k[i], ev_cmp_pe_v[i],
                                         ev_cmp_w_v[i], ev_conv_w[i], ev_a_log[i], ev_dt_bias[i],
                                         ev_gdn_norm[i], ev_w_out[i])
            x = layer_norm(DN_ALPHA * x + mix, ev_ln1_g[i], ev_ln1_b[i])
            ffn = swiglu(x, ev_ffn_wg[i], ev_ffn_wu[i], ev_ffn_wd[i])
            x = layer_norm(DN_ALPHA * x + ffn, ev_ln2_g[i], ev_ln2_b[i])
        else:
            mix = pool_mixer(x, od_pool_w[i], od_pool_scale[i])
            x = layer_norm(DN_ALPHA * x + mix, od_ln1_g[i], od_ln1_b[i])
            ffn = moe_ffn(x, od_router_w[i], od_exp_wg[i], od_exp_wu[i], od_exp_wd[i])
            x = layer_norm(DN_ALPHA * x + ffn, od_ln2_g[i], od_ln2_b[i])
    return x
```

```python
import functools
import math

import jax
import jax.numpy as jnp
from jax import lax
from jax.experimental import pallas as pl
from jax.experimental.pallas import tpu as pltpu

F32 = jnp.float32
BF16 = jnp.bfloat16

D_MODEL = 1024
DEPTH = 2
NSA_HEADS = 8
NSA_KV_HEADS = 2
NSA_GROUP = NSA_HEADS // NSA_KV_HEADS
NSA_HEAD_DIM = 64
CMP_BLOCK = 32
CMP_STRIDE = 16
SLC_BLOCK = 64
SLC_TOP_N = 16
SLC_LOCAL = 2
WINDOW = 512
Q_BLOCK = 128
FORCED_SCORE = 1e6
GDN_HEADS = 4
GDN_HEAD_DIM = 128
GDN_CONV = 4
GDN_CHUNK = 64
POOL_SIZES = (2, 4, 8, 16)
POOL_GROUP = D_MODEL // 4
D_FF = 2816
N_EXPERTS = 8
D_FF_EXPERT = 3584
ROPE_THETA = 10000.0
LN_EPS = 1e-5
NORM_EPS = 1e-6
DN_ALPHA = (2 * DEPTH) ** 0.25

NSA_Q = NSA_HEADS * NSA_HEAD_DIM
NSA_KV = NSA_KV_HEADS * NSA_HEAD_DIM
GDN_W = GDN_HEADS * GDN_HEAD_DIM
GDN_BD = GDN_HEADS * GDN_CHUNK

LANES = 128
NEG = -1e30
VMEM_LIMIT = 56 * 1024 * 1024


def _cparams(*sem):
    return pltpu.CompilerParams(dimension_semantics=sem, vmem_limit_bytes=VMEM_LIMIT)


def _dot(a, b):
    return jnp.dot(a, b, preferred_element_type=F32)


def _dot_nt(a, b):
    return lax.dot_general(a, b, (((1,), (1,)), ((), ())), preferred_element_type=F32)


def _dot_tn(a, b):
    return lax.dot_general(a, b, (((0,), (0,)), ((), ())), preferred_element_type=F32)


def _dot3(a, b):
    ah = a.astype(BF16)
    al = (a - ah.astype(F32)).astype(BF16)
    bh = b.astype(BF16)
    bl = (b - bh.astype(F32)).astype(BF16)
    return _dot(ah, bh) + (_dot(ah, bl) + _dot(al, bh))


def _layer_norm(y, g, b):
    mu = jnp.mean(y, -1, keepdims=True)
    d = y - mu
    var = jnp.mean(d * d, -1, keepdims=True)
    return d * lax.rsqrt(var + LN_EPS) * g + b


def _sigmoid(x):
    return 1.0 / (1.0 + jnp.exp(-x))


def _silu(x):
    return x * _sigmoid(x)


IN_TM = 512
_C_Q, _C_QR, _C_KS, _C_KSR, _C_KW, _C_KWR = 0, 512, 1024, 1152, 1280, 1408
_C_CMP, _C_VS, _C_VW, _C_GQKV, _C_Z, _C_SM, _C_END = 1536, 1792, 1920, 2048, 3584, 4096, 4224


def _inproj_kernel(x_ref, w_ref, cos_ref, sin_ref, q_ref, ks_ref, kw_ref, cmp_ref, vs_ref, vw_ref,
                   gqkv_ref, z_ref, sm_ref):
    xb = x_ref[...].astype(BF16)
    cos = cos_ref[...]
    sin = sin_ref[...]

    def mm(c0, n):
        return _dot(xb, w_ref[:, c0:c0 + n])

    scale = NSA_HEAD_DIM ** -0.5
    for j in range(NSA_Q // LANES):
        a = mm(_C_Q + j * LANES, LANES)
        r = mm(_C_QR + j * LANES, LANES)
        q_ref[:, j * LANES:(j + 1) * LANES] = ((a * cos + r * sin) * scale).astype(BF16)
    ks_ref[...] = (mm(_C_KS, LANES) * cos + mm(_C_KSR, LANES) * sin).astype(BF16)
    kw_ref[...] = (mm(_C_KW, LANES) * cos + mm(_C_KWR, LANES) * sin).astype(BF16)
    cmp_ref[...] = mm(_C_CMP, 2 * NSA_KV)
    vs_ref[...] = mm(_C_VS, NSA_KV).astype(BF16)
    vw_ref[...] = mm(_C_VW, NSA_KV).astype(BF16)
    gqkv_ref[...] = mm(_C_GQKV, 3 * GDN_W)
    z_ref[...] = mm(_C_Z, GDN_W)
    sm_ref[...] = mm(_C_SM, LANES)


def _rot_cols(w):
    k, n = w.shape
    half = NSA_HEAD_DIM // 2
    w4 = w.reshape(k, n // NSA_HEAD_DIM, 2, half)
    return jnp.stack([-w4[:, :, 1], w4[:, :, 0]], axis=2).reshape(k, n)


def _rope_tables(pos, reps):
    half = NSA_HEAD_DIM // 2
    inv = jnp.power(ROPE_THETA, -jnp.arange(half, dtype=F32) / half)
    ang = pos.astype(F32)[:, None] * inv[None, :]
    cos = jnp.tile(jnp.cos(ang), (1, 2 * reps))
    sin = jnp.tile(jnp.sin(ang), (1, 2 * reps))
    return cos, sin


def _in_projection(x2d, w_in, S):
    T = x2d.shape[0]
    o = 0
    parts = {}
    for name, n in (("q", NSA_Q), ("kc", NSA_KV), ("vc", NSA_KV), ("ks", NSA_KV), ("vs", NSA_KV),
                    ("kw", NSA_KV), ("vw", NSA_KV), ("gl", 3 * NSA_HEADS), ("gq", GDN_W), ("gk", GDN_W),
                    ("gv", GDN_W), ("gz", GDN_W), ("gb", GDN_HEADS), ("ga", GDN_HEADS)):
        parts[name] = w_in[:, o:o + n]
        o += n
    pad = jnp.zeros((D_MODEL, LANES - 3 * NSA_HEADS - 2 * GDN_HEADS), F32)
    w_all = jnp.concatenate([
        parts["q"], _rot_cols(parts["q"]), parts["ks"], _rot_cols(parts["ks"]),
        parts["kw"], _rot_cols(parts["kw"]), parts["kc"], parts["vc"], parts["vs"], parts["vw"],
        parts["gq"], parts["gk"], parts["gv"], parts["gz"], parts["gl"], parts["gb"], parts["ga"], pad,
    ], axis=1).astype(BF16)
    assert w_all.shape[1] == _C_END
    cos, sin = _rope_tables(jnp.arange(S), LANES // NSA_HEAD_DIM)
    nblk = S // IN_TM
    row = lambda i: (i, 0)
    outs = pl.pallas_call(
        _inproj_kernel,
        grid=(T // IN_TM,),
        in_specs=[
            pl.BlockSpec((IN_TM, D_MODEL), row),
            pl.BlockSpec((D_MODEL, _C_END), lambda i: (0, 0)),
            pl.BlockSpec((IN_TM, LANES), lambda i: (i % nblk, 0)),
            pl.BlockSpec((IN_TM, LANES), lambda i: (i % nblk, 0)),
        ],
        out_specs=[
            pl.BlockSpec((IN_TM, NSA_Q), row), pl.BlockSpec((IN_TM, NSA_KV), row),
            pl.BlockSpec((IN_TM, NSA_KV), row), pl.BlockSpec((IN_TM, 2 * NSA_KV), row),
            pl.BlockSpec((IN_TM, NSA_KV), row), pl.BlockSpec((IN_TM, NSA_KV), row),
            pl.BlockSpec((IN_TM, 3 * GDN_W), row), pl.BlockSpec((IN_TM, GDN_W), row),
            pl.BlockSpec((IN_TM, LANES), row),
        ],
        out_shape=[
            jax.ShapeDtypeStruct((T, NSA_Q), BF16), jax.ShapeDtypeStruct((T, NSA_KV), BF16),
            jax.ShapeDtypeStruct((T, NSA_KV), BF16), jax.ShapeDtypeStruct((T, 2 * NSA_KV), F32),
            jax.ShapeDtypeStruct((T, NSA_KV), BF16), jax.ShapeDtypeStruct((T, NSA_KV), BF16),
            jax.ShapeDtypeStruct((T, 3 * GDN_W), F32), jax.ShapeDtypeStruct((T, GDN_W), F32),
            jax.ShapeDtypeStruct((T, LANES), F32),
        ],
        compiler_params=_cparams("parallel"),
        name="in_projection",
    )(x2d, w_all, cos, sin)
    return outs


def _compress_kernel(k_ref, v_ref, wk_ref, wv_ref, pe_ref, cos_ref, sin_ref, kc_ref, vc_ref):
    nb = k_ref.shape[2]
    k = k_ref[0, 0]
    v = v_ref[0, 0]
    klo = (k + pe_ref[0:1, :]).astype(BF16)
    khi = (k + pe_ref[1:2, :]).astype(BF16)
    vlo = (v + pe_ref[2:3, :]).astype(BF16)
    vhi = (v + pe_ref[3:4, :]).astype(BF16)

    def up(m):
        return pltpu.roll(m, nb - 1, axis=0)

    a = _dot(klo, wk_ref[0]) + up(_dot(khi, wk_ref[1]))
    ar = _dot(klo, wk_ref[2]) + up(_dot(khi, wk_ref[3]))
    kc_ref[0] = a * cos_ref[...] + ar * sin_ref[...]
    vc_ref[0] = _dot(vlo, wv_ref[0]) + up(_dot(vhi, wv_ref[1]))


def _compress(cmp2d, pe_k, w_k, pe_v, w_v, B, S):
    nb = S // CMP_STRIDE
    bh = B * NSA_KV_HEADS
    row_w = CMP_STRIDE * NSA_HEAD_DIM
    c = cmp2d.reshape(B, S, 2, NSA_KV_HEADS, NSA_HEAD_DIM).transpose(2, 0, 3, 1, 4)
    c = c.reshape(2, bh, nb, row_w)
    wk = w_k.reshape(2, row_w, NSA_HEAD_DIM)
    wk_all = jnp.concatenate([wk, jnp.stack([_rot_cols(wk[0]), _rot_cols(wk[1])])], 0).astype(BF16)
    wv_all = w_v.reshape(2, row_w, NSA_HEAD_DIM).astype(BF16)
    pe = jnp.concatenate([pe_k.reshape(2, row_w), pe_v.reshape(2, row_w)], 0)
    cos, sin = _rope_tables(jnp.arange(nb) * CMP_STRIDE + (CMP_BLOCK - 1) / 2, 1)
    full = lambda shape: pl.BlockSpec(shape, lambda i: (0,) * len(shape))
    kc, vc = pl.pallas_call(
        _compress_kernel,
        grid=(bh,),
        in_specs=[
            pl.BlockSpec((1, 1, nb, row_w), lambda i: (0, i, 0, 0)),
            pl.BlockSpec((1, 1, nb, row_w), lambda i: (1, i, 0, 0)),
            full((4, row_w, NSA_HEAD_DIM)), full((2, row_w, NSA_HEAD_DIM)), full((4, row_w)),
            full((nb, NSA_HEAD_DIM)), full((nb, NSA_HEAD_DIM)),
        ],
        out_specs=[pl.BlockSpec((1, nb, NSA_HEAD_DIM), lambda i: (i, 0, 0))] * 2,
        out_shape=[jax.ShapeDtypeStruct((bh, nb, NSA_HEAD_DIM), F32)] * 2,
        compiler_params=_cparams("parallel"),
        name="nsa_compress",
    )(c, c, wk_all, wv_all, pe, cos, sin)
    return kc, vc


SEL_KT = 512


def _nsa_kernel(q_ref, kct_ref, vc_ref, ov_ref, kst_ref, vs_ref, kwt_ref, vw_ref, gate_ref, o_ref,
                *, n_sel, kt_w, win_w):
    G, QB, D = NSA_GROUP, Q_BLOCK, NSA_HEAD_DIM
    nb = kct_ref.shape[3]
    ns = ov_ref.shape[1]
    qb = pl.program_id(2)
    s0 = qb * QB
    q4 = q_ref[0, 0].reshape(G * QB, D)
    trow = s0 + lax.broadcasted_iota(jnp.int32, (QB, 1), 0)

    s_c = _dot(q4, kct_ref[0, 0]).reshape(G, QB, nb)
    ncol = lax.broadcasted_iota(jnp.int32, (QB, nb), 1)
    mask_c = ((ncol * CMP_STRIDE + (CMP_BLOCK - 1)) <= trow)[None]
    s_c = jnp.where(mask_c, s_c, NEG)
    m_c = jnp.max(s_c, -1, keepdims=True)
    e_c = jnp.where(mask_c, jnp.exp(s_c - m_c), 0.0)
    l_c = jnp.sum(e_c, -1, keepdims=True)
    p_c = e_c / jnp.where(l_c > 0, l_c, 1.0)
    o_c = _dot(p_c.reshape(G * QB, nb).astype(BF16), vc_ref[0, 0])

    psum = p_c[0] + p_c[1] + p_c[2] + p_c[3]
    p_hi = psum.astype(BF16)
    p_lo = (psum - p_hi.astype(F32)).astype(BF16)
    imp = _dot(p_hi, ov_ref[...]) + _dot(p_lo, ov_ref[...])
    jcol = lax.broadcasted_iota(jnp.int32, (QB, ns), 1)
    cur = trow // SLC_BLOCK
    forced = (jcol == 0) | ((jcol <= cur) & (jcol > cur - SLC_LOCAL))
    causal_blk = (jcol * SLC_BLOCK) <= trow
    score = jnp.where(causal_blk, jnp.where(forced, FORCED_SCORE, imp), -1.0)
    jf = jcol.astype(F32)
    sel = jnp.zeros((QB, ns), F32)
    for _ in range(n_sel):
        mx = jnp.max(score, -1, keepdims=True)
        first = jnp.min(jnp.where(score == mx, jf, float(ns)), -1, keepdims=True)
        pick = jf == first
        sel = jnp.where(pick, 1.0, sel)
        score = jnp.where(pick, -2.0, score)
    selb = sel.astype(BF16)

    brow = lax.broadcasted_iota(jnp.int32, (ns, kt_w), 0)
    bcol = lax.broadcasted_iota(jnp.int32, (ns, kt_w), 1)
    kcol = lax.broadcasted_iota(jnp.int32, (QB, kt_w), 1)

    def sel_step(kt, carry):
        m, l, acc = carry
        k0 = pl.multiple_of(kt * kt_w, kt_w)
        s = _dot(q4, kst_ref[0, 0, :, pl.ds(k0, kt_w)]).reshape(G, QB, kt_w)
        expand = jnp.where((k0 + bcol) // SLC_BLOCK == brow, 1.0, 0.0).astype(BF16)
        chosen = _dot(selb, expand)
        mask = ((chosen > 0.5) & ((k0 + kcol) <= trow))[None]
        s = jnp.where(mask, s, NEG)
        m_new = jnp.maximum(m, jnp.max(s, -1, keepdims=True))
        alpha = jnp.exp(m - m_new)
        p = jnp.exp(s - m_new)
        l = alpha * l + jnp.sum(p, -1, keepdims=True)
        pv = _dot(p.reshape(G * QB, kt_w).astype(BF16), vs_ref[0, 0, pl.ds(k0, kt_w), :])
        acc = alpha.reshape(G * QB, 1) * acc + pv
        return m_new, l, acc

    n_kt = (s0 + QB + kt_w - 1) // kt_w
    m_s, l_s, acc_s = lax.fori_loop(
        0, n_kt, sel_step,
        (jnp.full((G, QB, 1), NEG, F32), jnp.zeros((G, QB, 1), F32), jnp.zeros((G * QB, D), F32)))
    o_s = acc_s / l_s.reshape(G * QB, 1)

    w0 = pl.multiple_of(jnp.maximum(s0 + QB - win_w, 0), QB)
    s_w = _dot(q4, kwt_ref[0, 0, :, pl.ds(w0, win_w)]).reshape(G, QB, win_w)
    rel = trow - (w0 + lax.broadcasted_iota(jnp.int32, (QB, win_w), 1))
    mask_w = ((rel >= 0) & (rel < WINDOW))[None]
    s_w = jnp.where(mask_w, s_w, NEG)
    e_w = jnp.exp(s_w - jnp.max(s_w, -1, keepdims=True))
    p_w = e_w / jnp.sum(e_w, -1, keepdims=True)
    o_w = _dot(p_w.reshape(G * QB, win_w).astype(BF16), vw_ref[0, 0, pl.ds(w0, win_w), :])

    gate = _sigmoid(gate_ref[0, 0])
    for g in range(G):
        rows = slice(g * QB, (g + 1) * QB)
        o_ref[0, 0, g] = (gate[:, 3 * g:3 * g + 1] * o_c[rows] + gate[:, 3 * g + 1:3 * g + 2] * o_s[rows]
                          + gate[:, 3 * g + 2:3 * g + 3] * o_w[rows])


def _nsa_attention(q, ks, vs, kw, vw, kc, vc, small, B, S):
    H, HKV, G, D = NSA_HEADS, NSA_KV_HEADS, NSA_GROUP, NSA_HEAD_DIM
    nb = S // CMP_STRIDE
    ns = S // SLC_BLOCK
    n_sel = min(SLC_TOP_N, ns)
    kt_w = min(SEL_KT, S)
    win_w = min(WINDOW + Q_BLOCK, S)
    q5 = q.reshape(B, S, HKV, G, D).transpose(0, 2, 3, 1, 4)
    to_t = lambda t: t.reshape(B, S, HKV, D).transpose(0, 2, 3, 1)
    to_h = lambda t: t.reshape(B, S, HKV, D).transpose(0, 2, 1, 3)
    kct = kc.reshape(B, HKV, nb, D).transpose(0, 1, 3, 2).astype(BF16)
    vcb = vc.reshape(B, HKV, nb, D).astype(BF16)
    gates = small[:, :3 * H].reshape(B, S, HKV, 3 * G).transpose(0, 2, 1, 3)
    n_i = jnp.arange(nb)[:, None]
    m_i = jnp.arange(ns)[None, :]
    overlap = ((n_i * CMP_STRIDE <= m_i * SLC_BLOCK + SLC_BLOCK - 1)
               & (n_i * CMP_STRIDE + CMP_BLOCK - 1 >= m_i * SLC_BLOCK)).astype(BF16)
    per_bh = lambda shape: pl.BlockSpec((1, 1) + shape, lambda b, h, i: (b, h, 0, 0))
    o = pl.pallas_call(
        functools.partial(_nsa_kernel, n_sel=n_sel, kt_w=kt_w, win_w=win_w),
        grid=(B, HKV, S // Q_BLOCK),
        in_specs=[
            pl.BlockSpec((1, 1, G, Q_BLOCK, D), lambda b, h, i: (b, h, 0, i, 0)),
            per_bh((D, nb)), per_bh((nb, D)),
            pl.BlockSpec((nb, ns), lambda b, h, i: (0, 0)),
            per_bh((D, S)), per_bh((S, D)), per_bh((D, S)), per_bh((S, D)),
            pl.BlockSpec((1, 1, Q_BLOCK, 3 * G), lambda b, h, i: (b, h, i, 0)),
        ],
        out_specs=pl.BlockSpec((1, 1, G, Q_BLOCK, D), lambda b, h, i: (b, h, 0, i, 0)),
        out_shape=jax.ShapeDtypeStruct((B, HKV, G, S, D), F32),
        compiler_params=_cparams("parallel", "parallel", "parallel"),
        name="nsa_attention",
    )(q5, kct, vcb, overlap, to_t(ks), to_h(vs), to_t(kw), to_h(vw), gates)
    return o.transpose(0, 3, 1, 2, 4).reshape(B * S, H * D)


GDN_TB = 256
GDN_HALO = 8


def _gdn_prep_kernel(qkv_ref, halo_ref, sm_ref, cw_ref, alog_ref, dtb_ref,
                     u_ref, wq_ref, kd_ref, intra_ref, dec_ref, xs_scr, y_scr, bg_scr):
    C, HD, NH, BD = GDN_CHUNK, GDN_HEAD_DIM, GDN_HEADS, GDN_BD
    tb = qkv_ref.shape[1]
    first = pl.program_id(1) == 0
    xs_scr[GDN_HALO:GDN_HALO + tb, :] = qkv_ref[0]
    xs_scr[0:GDN_HALO, :] = jnp.where(first, 0.0, halo_ref[0])
    y = cw_ref[GDN_CONV - 1:GDN_CONV, :] * xs_scr[GDN_HALO:GDN_HALO + tb, :]
    for d in range(1, GDN_CONV):
        y = y + cw_ref[GDN_CONV - 1 - d:GDN_CONV - d, :] * xs_scr[GDN_HALO - d:GDN_HALO - d + tb, :]
    y_scr[...] = _silu(y)
    sm = sm_ref[0]
    b_logit = sm[:, 3 * NSA_HEADS:3 * NSA_HEADS + NH]
    a_logit = sm[:, 3 * NSA_HEADS + NH:3 * NSA_HEADS + 2 * NH] + dtb_ref[...]
    softplus = jnp.maximum(a_logit, 0.0) + jnp.log(1.0 + jnp.exp(-jnp.abs(a_logit)))
    bg_scr[:, 0:NH] = _sigmoid(b_logit)
    bg_scr[:, NH:2 * NH] = -jnp.exp(alog_ref[...]) * softplus

    r = lax.broadcasted_iota(jnp.int32, (BD, BD), 0)
    c = lax.broadcasted_iota(jnp.int32, (BD, BD), 1)
    same = (r // C) == (c // C)
    eye = r == c
    tril = same & (r >= c)
    triu = same & (r <= c)
    strict = same & (r > c)
    eye_f = jnp.where(eye, 1.0, 0.0)

    def stack_heads(m, c0):
        return jnp.concatenate([m[:, c0 + h * HD:c0 + (h + 1) * HD] for h in range(NH)], axis=0)

    def chunk(ci, carry):
        r0 = pl.multiple_of(ci * C, C)
        yc = y_scr[pl.ds(r0, C), :]
        bg = bg_scr[pl.ds(r0, C), :]
        q4 = stack_heads(yc, 0)
        k4 = stack_heads(yc, GDN_W)
        v4 = stack_heads(yc, 2 * GDN_W)
        q4 = q4 * lax.rsqrt(jnp.sum(q4 * q4, -1, keepdims=True) + NORM_EPS) * (HD ** -0.5)
        k4 = k4 * lax.rsqrt(jnp.sum(k4 * k4, -1, keepdims=True) + NORM_EPS)
        b4 = jnp.concatenate([bg[:, h:h + 1] for h in range(NH)], axis=0)
        g4 = jnp.concatenate([bg[:, NH + h:NH + h + 1] for h in range(NH)], axis=0)
        g_row = jnp.sum(jnp.where(eye, g4, 0.0), axis=0, keepdims=True)
        gc_col = jnp.sum(jnp.where(tril, g_row, 0.0), axis=1, keepdims=True)
        gc_row = jnp.sum(jnp.where(triu, g4, 0.0), axis=0, keepdims=True)
        gl_col = jnp.sum(jnp.where(same, g_row, 0.0), axis=1, keepdims=True)
        decay = jnp.where(tril, jnp.exp(jnp.minimum(gc_col - gc_row, 0.0)), 0.0)
        kb4 = k4 * b4
        k4b = k4.astype(BF16)
        lmat = jnp.where(strict, _dot_nt(kb4.astype(BF16), k4b) * decay, 0.0)
        intra = jnp.where(tril, _dot_nt(q4.astype(BF16), k4b) * decay, 0.0)
        inv = eye_f - lmat
        pw = lmat
        for _ in range(int(math.log2(C)) - 1):
            pw = _dot3(pw, pw)
            inv = _dot3(inv, eye_f + pw)
        egc = jnp.exp(gc_col)
        uw = _dot3(inv, jnp.concatenate([v4 * b4, kb4 * egc], axis=1))
        u_ref[0, ci] = uw[:, :HD]
        w4 = uw[:, HD:].astype(BF16).reshape(NH, C, HD)
        qg4 = (q4 * egc).astype(BF16).reshape(NH, C, HD)
        wq_ref[0, ci] = jnp.concatenate([w4, qg4], axis=1)
        kd_ref[0, ci] = (k4 * jnp.exp(gl_col - gc_col)).astype(BF16)
        intra_ref[0, ci] = intra.astype(BF16)
        dec = jnp.broadcast_to(jnp.exp(gl_col), (BD, LANES)).reshape(NH, C, LANES)
        dec_ref[0, ci] = dec[:, 0:8, :]
        return carry

    lax.fori_loop(0, tb // C, chunk, 0)


def _gdn_scan_kernel(u_ref, wq_ref, kd_ref, intra_ref, dec_ref, z_ref, nw_ref, o_ref, s_scr):
    C, HD, NH = GDN_CHUNK, GDN_HEAD_DIM, GDN_HEADS
    nbatch = u_ref.shape[0]

    @pl.when(pl.program_id(0) == 0)
    def _():
        s_scr[...] = jnp.zeros_like(s_scr)

    nw = nw_ref[...]
    for b in range(nbatch):
        vnew, qs = [], []
        for h in range(NH):
            st = s_scr[b * NH + h]
            res = _dot(wq_ref[b, 0, h], st.astype(BF16))
            vnew.append(u_ref[b, 0, h * C:(h + 1) * C, :] - res[:C])
            qs.append(res[C:])
        vnew4 = jnp.concatenate(vnew, axis=0).astype(BF16)
        o4 = jnp.concatenate(qs, axis=0) + _dot(intra_ref[b, 0], vnew4)
        for h in range(NH):
            rows = slice(h * C, (h + 1) * C)
            st = s_scr[b * NH + h]
            s_scr[b * NH + h] = st * dec_ref[b, 0, h, 0:1, :] + _dot_tn(kd_ref[b, 0, rows, :], vnew4[rows])
            oh = o4[rows]
            oh = oh * lax.rsqrt(jnp.mean(oh * oh, -1, keepdims=True) + NORM_EPS) * nw
            o_ref[b, :, h * HD:(h + 1) * HD] = oh * _silu(z_ref[b, :, h * HD:(h + 1) * HD])


def _gdn_mixer(gqkv, z, small, conv_w, a_log, dt_bias, norm_w, B, S):
    C, HD, NH, BD = GDN_CHUNK, GDN_HEAD_DIM, GDN_HEADS, GDN_BD
    nc = S // C
    tb = min(GDN_TB, S)
    cb = tb // C
    qkv3 = gqkv.reshape(B, S, 3 * GDN_W)
    sm3 = small.reshape(B, S, LANES)
    hb = tb // GDN_HALO
    u, wq, kd, intra, dec = pl.pallas_call(
        _gdn_prep_kernel,
        grid=(B, S // tb),
        in_specs=[
            pl.BlockSpec((1, tb, 3 * GDN_W), lambda b, i: (b, i, 0)),
            pl.BlockSpec((1, GDN_HALO, 3 * GDN_W), lambda b, i: (b, jnp.maximum(i * hb - 1, 0), 0)),
            pl.BlockSpec((1, tb, LANES), lambda b, i: (b, i, 0)),
            pl.BlockSpec((GDN_CONV, 3 * GDN_W), lambda b, i: (0, 0)),
            pl.BlockSpec((1, NH), lambda b, i: (0, 0)),
            pl.BlockSpec((1, NH), lambda b, i: (0, 0)),
        ],
        out_specs=[
            pl.BlockSpec((1, cb, BD, HD), lambda b, i: (b, i, 0, 0)),
            pl.BlockSpec((1, cb, NH, 2 * C, HD), lambda b, i: (b, i, 0, 0, 0)),
            pl.BlockSpec((1, cb, BD, HD), lambda b, i: (b, i, 0, 0)),
            pl.BlockSpec((1, cb, BD, BD), lambda b, i: (b, i, 0, 0)),
            pl.BlockSpec((1, cb, NH, 8, LANES), lambda b, i: (b, i, 0, 0, 0)),
        ],
        out_shape=[
            jax.ShapeDtypeStruct((B, nc, BD, HD), F32),
            jax.ShapeDtypeStruct((B, nc, NH, 2 * C, HD), BF16),
            jax.ShapeDtypeStruct((B, nc, BD, HD), BF16),
            jax.ShapeDtypeStruct((B, nc, BD, BD), BF16),
            jax.ShapeDtypeStruct((B, nc, NH, 8, LANES), F32),
        ],
        scratch_shapes=[
            pltpu.VMEM((GDN_HALO + tb, 3 * GDN_W), F32),
            pltpu.VMEM((tb, 3 * GDN_W), F32),
            pltpu.VMEM((tb, 2 * NH), F32),
        ],
        compiler_params=_cparams("parallel", "parallel"),
        name="gdn_prepare",
    )(qkv3, qkv3, sm3, conv_w, a_log.reshape(1, NH), dt_bias.reshape(1, NH))

    o = pl.pallas_call(
        _gdn_scan_kernel,
        grid=(nc,),
        in_specs=[
            pl.BlockSpec((B, 1, BD, HD), lambda c: (0, c, 0, 0)),
            pl.BlockSpec((B, 1, NH, 2 * C, HD), lambda c: (0, c, 0, 0, 0)),
            pl.BlockSpec((B, 1, BD, HD), lambda c: (0, c, 0, 0)),
            pl.BlockSpec((B, 1, BD, BD), lambda c: (0, c, 0, 0)),
            pl.BlockSpec((B, 1, NH, 8, LANES), lambda c: (0, c, 0, 0, 0)),
            pl.BlockSpec((B, C, GDN_W), lambda c: (0, c, 0)),
            pl.BlockSpec((1, HD), lambda c: (0, 0)),
        ],
        out_specs=pl.BlockSpec((B, C, GDN_W), lambda c: (0, c, 0)),
        out_shape=jax.ShapeDtypeStruct((B, S, GDN_W), F32),
        scratch_shapes=[pltpu.VMEM((B * NH, HD, HD), F32)],
        compiler_params=_cparams("arbitrary"),
        name="gdn_scan",
    )(u, wq, kd, intra, dec, z.reshape(B, S, GDN_W), norm_w.reshape(1, HD))
    return o.reshape(B * S, GDN_W)


MIX_TM = 512


def _mix_ln_kernel(x_ref, a_ref, b_ref, w_ref, g_ref, be_ref, o_ref):
    mix = (_dot(a_ref[...].astype(BF16), w_ref[0:NSA_Q, :])
           + _dot(b_ref[...].astype(BF16), w_ref[NSA_Q:NSA_Q + GDN_W, :]))
    o_ref[...] = _layer_norm(DN_ALPHA * x_ref[...] + mix, g_ref[...], be_ref[...])


def _mix_out_ln(x2d, o_a, o_b, w_out, g, b):
    T = x2d.shape[0]
    row = lambda i: (i, 0)
    const = lambda i: (0, 0)
    return pl.pallas_call(
        _mix_ln_kernel,
        grid=(T // MIX_TM,),
        in_specs=[
            pl.BlockSpec((MIX_TM, D_MODEL), row), pl.BlockSpec((MIX_TM, NSA_Q), row),
            pl.BlockSpec((MIX_TM, GDN_W), row), pl.BlockSpec((NSA_Q + GDN_W, D_MODEL), const),
            pl.BlockSpec((1, D_MODEL), const), pl.BlockSpec((1, D_MODEL), const),
        ],
        out_specs=pl.BlockSpec((MIX_TM, D_MODEL), row),
        out_shape=jax.ShapeDtypeStruct((T, D_MODEL), F32),
        compiler_params=_cparams("parallel"),
        name="mix_out_ln",
    )(x2d, o_a, o_b, w_out.astype(BF16), g.reshape(1, D_MODEL), b.reshape(1, D_MODEL))


FFN_TM = 1024
FFN_TF = 256


def _ffn_ln_kernel(x_ref, wg_ref, wu_ref, wd_ref, g_ref, be_ref, o_ref, xb_scr, acc_scr):
    j = pl.program_id(1)

    @pl.when(j == 0)
    def _():
        xb_scr[...] = x_ref[...].astype(BF16)
        acc_scr[...] = jnp.zeros_like(acc_scr)

    xb = xb_scr[...]
    hid = _silu(_dot(xb, wg_ref[...])) * _dot(xb, wu_ref[...])
    acc_scr[...] += _dot(hid.astype(BF16), wd_ref[...])

    @pl.when(j == pl.num_programs(1) - 1)
    def _():
        o_ref[...] = _layer_norm(DN_ALPHA * x_ref[...] + acc_scr[...], g_ref[...], be_ref[...])


def _ffn_ln(x2d, wg, wu, wd, g, b):
    T = x2d.shape[0]
    dff = wg.shape[1]
    tm = min(FFN_TM, T)
    return pl.pallas_call(
        _ffn_ln_kernel,
        grid=(T // tm, dff // FFN_TF),
        in_specs=[
            pl.BlockSpec((tm, D_MODEL), lambda i, j: (i, 0)),
            pl.BlockSpec((D_MODEL, FFN_TF), lambda i, j: (0, j)),
            pl.BlockSpec((D_MODEL, FFN_TF), lambda i, j: (0, j)),
            pl.BlockSpec((FFN_TF, D_MODEL), lambda i, j: (j, 0)),
            pl.BlockSpec((1, D_MODEL), lambda i, j: (0, 0)),
            pl.BlockSpec((1, D_MODEL), lambda i, j: (0, 0)),
        ],
        out_specs=pl.BlockSpec((tm, D_MODEL), lambda i, j: (i, 0)),
        out_shape=jax.ShapeDtypeStruct((T, D_MODEL), F32),
        scratch_shapes=[pltpu.VMEM((tm, D_MODEL), BF16), pltpu.VMEM((tm, D_MODEL), F32)],
        compiler_params=_cparams("parallel", "arbitrary"),
        name="ffn_ln",
    )(x2d, wg.astype(BF16), wu.astype(BF16), wd.astype(BF16), g.reshape(1, D_MODEL), b.reshape(1, D_MODEL))


POOL_TM = 512
POOL_HALO = 16


def _pool_ln_kernel(x_ref, halo_ref, pw_ref, ps_ref, g_ref, be_ref, o_ref, xs_scr):
    tm = x_ref.shape[1]
    i = pl.program_id(1)
    x = x_ref[0]
    xs_scr[POOL_HALO:POOL_HALO + tm, :] = x
    xs_scr[0:POOL_HALO, :] = jnp.where(i == 0, 0.0, halo_ref[0])
    t1 = (i * tm + 1 + lax.broadcasted_iota(jnp.int32, (tm, 1), 0)).astype(F32)
    for gi, win in enumerate(POOL_SIZES):
        cols = slice(gi * POOL_GROUP, (gi + 1) * POOL_GROUP)
        acc = x[:, cols]
        for d in range(1, win):
            acc = acc + xs_scr[POOL_HALO - d:POOL_HALO - d + tm, cols]
        mean = acc / jnp.minimum(t1, float(win))
        mix = _dot((mean - x[:, cols]).astype(BF16), pw_ref[gi]) * ps_ref[:, cols]
        xs_scr[POOL_HALO:POOL_HALO + tm, cols] = DN_ALPHA * x[:, cols] + mix
    o_ref[0] = _layer_norm(xs_scr[POOL_HALO:POOL_HALO + tm, :], g_ref[...], be_ref[...])


def _pool_ln(x3d, pool_w, pool_scale, g, b):
    B, S, _ = x3d.shape
    tm = min(POOL_TM, S)
    hb = tm // POOL_HALO
    const = lambda bb, i: (0, 0)
    return pl.pallas_call(
        _pool_ln_kernel,
        grid=(B, S // tm),
        in_specs=[
            pl.BlockSpec((1, tm, D_MODEL), lambda bb, i: (bb, i, 0)),
            pl.BlockSpec((1, POOL_HALO, D_MODEL), lambda bb, i: (bb, jnp.maximum(i * hb - 1, 0), 0)),
            pl.BlockSpec((len(POOL_SIZES), POOL_GROUP, POOL_GROUP), lambda bb, i: (0, 0, 0)),
            pl.BlockSpec((1, D_MODEL), const), pl.BlockSpec((1, D_MODEL), const),
            pl.BlockSpec((1, D_MODEL), const),
        ],
        out_specs=pl.BlockSpec((1, tm, D_MODEL), lambda bb, i: (bb, i, 0)),
        out_shape=jax.ShapeDtypeStruct((B, S, D_MODEL), F32),
        scratch_shapes=[pltpu.VMEM((POOL_HALO + tm, D_MODEL), F32)],
        compiler_params=_cparams("parallel", "parallel"),
        name="pool_ln",
    )(x3d, x3d, pool_w.astype(BF16), pool_scale.reshape(1, D_MODEL), g.reshape(1, D_MODEL),
      b.reshape(1, D_MODEL))


MOE_TM = 1024
MOE_TF = 512


def _moe_ln_kernel(x_ref, rw_ref, wg_ref, wu_ref, wd_ref, g_ref, be_ref, o_ref, xb_scr, acc_scr, gate_scr):
    e = pl.program_id(1)
    j = pl.program_id(2)
    lane = lax.broadcasted_iota(jnp.int32, (x_ref.shape[0], LANES), 1)

    @pl.when((e == 0) & (j == 0))
    def _():
        x = x_ref[...]
        xb_scr[...] = x.astype(BF16)
        acc_scr[...] = jnp.zeros_like(acc_scr)
        lanef = lane.astype(F32)
        logits = jnp.where(lane < N_EXPERTS, _dot3(x, rw_ref[...]), NEG)
        ex = jnp.exp(logits - jnp.max(logits, -1, keepdims=True))
        probs = jnp.where(lane < N_EXPERTS, ex / jnp.sum(ex, -1, keepdims=True), -1.0)
        p1 = jnp.max(probs, -1, keepdims=True)
        i1 = jnp.min(jnp.where(probs == p1, lanef, float(LANES)), -1, keepdims=True)
        rest = jnp.where(lanef == i1, -1.0, probs)
        p2 = jnp.max(rest, -1, keepdims=True)
        i2 = jnp.min(jnp.where(rest == p2, lanef, float(LANES)), -1, keepdims=True)
        den = p1 + p2
        gate_scr[...] = jnp.where(lanef == i1, p1 / den, 0.0) + jnp.where(lanef == i2, p2 / den, 0.0)

    gate_e = jnp.sum(jnp.where(lane == e, gate_scr[...], 0.0), -1, keepdims=True)
    xb = xb_scr[...]
    hid = _silu(_dot(xb, wg_ref[0])) * _dot(xb, wu_ref[0])
    acc_scr[...] += gate_e * _dot(hid.astype(BF16), wd_ref[0])

    @pl.when((e == pl.num_programs(1) - 1) & (j == pl.num_programs(2) - 1))
    def _():
        o_ref[...] = _layer_norm(DN_ALPHA * x_ref[...] + acc_scr[...], g_ref[...], be_ref[...])


def _moe_ln(x2d, router_w, wg, wu, wd, g, b):
    T = x2d.shape[0]
    ne, _, dff = wg.shape
    tm = min(MOE_TM, T)
    rw = jnp.concatenate([router_w, jnp.zeros((D_MODEL, LANES - ne), F32)], axis=1)
    const = lambda i, e, j: (0, 0)
    return pl.pallas_call(
        _moe_ln_kernel,
        grid=(T // tm, ne, dff // MOE_TF),
        in_specs=[
            pl.BlockSpec((tm, D_MODEL), lambda i, e, j: (i, 0)),
            pl.BlockSpec((D_MODEL, LANES), const),
            pl.BlockSpec((1, D_MODEL, MOE_TF), lambda i, e, j: (e, 0, j)),
            pl.BlockSpec((1, D_MODEL, MOE_TF), lambda i, e, j: (e, 0, j)),
            pl.BlockSpec((1, MOE_TF, D_MODEL), lambda i, e, j: (e, j, 0)),
            pl.BlockSpec((1, D_MODEL), const), pl.BlockSpec((1, D_MODEL), const),
        ],
        out_specs=pl.BlockSpec((tm, D_MODEL), lambda i, e, j: (i, 0)),
        out_shape=jax.ShapeDtypeStruct((T, D_MODEL), F32),
        scratch_shapes=[pltpu.VMEM((tm, D_MODEL), BF16), pltpu.VMEM((tm, D_MODEL), F32),
                        pltpu.VMEM((tm, LANES), F32)],
        compiler_params=_cparams("parallel", "arbitrary", "arbitrary"),
        name="moe_ln",
    )(x2d, rw, wg.astype(BF16), wu.astype(BF16), wd.astype(BF16), g.reshape(1, D_MODEL),
      b.reshape(1, D_MODEL))


def _even_layer(x2d, B, S, w_in, pe_k, w_k, pe_v, w_v, conv_w, a_log, dt_bias, gdn_norm, w_out,
                ln1_g, ln1_b, wg, wu, wd, ln2_g, ln2_b):
    q, ks, kw, cmp2d, vs, vw, gqkv, z, small = _in_projection(x2d, w_in, S)
    kc, vc = _compress(cmp2d, pe_k, w_k, pe_v, w_v, B, S)
    o_a = _nsa_attention(q, ks, vs, kw, vw, kc, vc, small, B, S)
    o_b = _gdn_mixer(gqkv, z, small, conv_w, a_log, dt_bias, gdn_norm, B, S)
    x2d = _mix_out_ln(x2d, o_a, o_b, w_out, ln1_g, ln1_b)
    return _ffn_ln(x2d, wg, wu, wd, ln2_g, ln2_b)


def _odd_layer(x2d, B, S, pool_w, pool_scale, ln1_g, ln1_b, router_w, wg, wu, wd, ln2_g, ln2_b):
    x3d = _pool_ln(x2d.reshape(B, S, D_MODEL), pool_w, pool_scale, ln1_g, ln1_b)
    return _moe_ln(x3d.reshape(B * S, D_MODEL), router_w, wg, wu, wd, ln2_g, ln2_b)


def kernel(x, ev_w_in, ev_cmp_pe_k, ev_cmp_w_k, ev_cmp_pe_v, ev_cmp_w_v, ev_conv_w, ev_a_log, ev_dt_bias,
           ev_gdn_norm, ev_w_out, ev_ln1_g, ev_ln1_b, ev_ffn_wg, ev_ffn_wu, ev_ffn_wd, ev_ln2_g, ev_ln2_b,
           od_pool_w, od_pool_scale, od_ln1_g, od_ln1_b, od_router_w, od_exp_wg, od_exp_wu, od_exp_wd,
           od_ln2_g, od_ln2_b):
    B, S, _ = x.shape
    h = x.reshape(B * S, D_MODEL)
    for layer in range(DEPTH):
        i = layer // 2
        if layer % 2 == 0:
            h = _even_layer(h, B, S, ev_w_in[i], ev_cmp_pe_k[i], ev_cmp_w_k[i], ev_cmp_pe_v[i],
                            ev_cmp_w_v[i], ev_conv_w[i], ev_a_log[i], ev_dt_bias[i], ev_gdn_norm[i],
                            ev_w_out[i], ev_ln1_g[i], ev_ln1_b[i], ev_ffn_wg[i], ev_ffn_wu[i],
                            ev_ffn_wd[i], ev_ln2_g[i], ev_ln2_b[i])
        else:
            h = _odd_layer(h, B, S, od_pool_w[i], od_pool_scale[i], od_ln1_g[i], od_ln1_b[i],
                           od_router_w[i], od_exp_wg[i], od_exp_wu[i], od_exp_wd[i], od_ln2_g[i],
                           od_ln2_b[i])
    return h.reshape(B, S, D_MODEL)
```

```python
import functools
import math

import jax
import jax.numpy as jnp
from jax import lax
from jax.experimental import pallas as pl
from jax.experimental.pallas import tpu as pltpu

F32 = jnp.float32
BF16 = jnp.bfloat16

D_MODEL = 1024
DEPTH = 2
NSA_HEADS = 8
NSA_KV_HEADS = 2
NSA_GROUP = NSA_HEADS // NSA_KV_HEADS
NSA_HEAD_DIM = 64
CMP_BLOCK = 32
CMP_STRIDE = 16
SLC_BLOCK = 64
SLC_TOP_N = 16
SLC_LOCAL = 2
WINDOW = 512
Q_BLOCK = 128
FORCED_SCORE = 1e6
GDN_HEADS = 4
GDN_HEAD_DIM = 128
GDN_CONV = 4
GDN_CHUNK = 64
POOL_SIZES = (2, 4, 8, 16)
POOL_GROUP = D_MODEL // 4
D_FF = 2816
N_EXPERTS = 8
D_FF_EXPERT = 3584
ROPE_THETA = 10000.0
LN_EPS = 1e-5
NORM_EPS = 1e-6
DN_ALPHA = (2 * DEPTH) ** 0.25

NSA_Q = NSA_HEADS * NSA_HEAD_DIM
NSA_KV = NSA_KV_HEADS * NSA_HEAD_DIM
GDN_W = GDN_HEADS * GDN_HEAD_DIM
GDN_BD = GDN_HEADS * GDN_CHUNK

LANES = 128
NEG = -1e30
LOG2E = 1.4426950408889634
VMEM_LIMIT = 56 * 1024 * 1024


def _cparams(*sem):
    return pltpu.CompilerParams(dimension_semantics=sem, vmem_limit_bytes=VMEM_LIMIT)


def _dot(a, b):
    return jnp.dot(a, b, preferred_element_type=F32)


def _dot_nt(a, b):
    return lax.dot_general(a, b, (((1,), (1,)), ((), ())), preferred_element_type=F32)


def _dot_tn(a, b):
    return lax.dot_general(a, b, (((0,), (0,)), ((), ())), preferred_element_type=F32)


def _dot3(a, b):
    ah = a.astype(BF16)
    al = (a - ah.astype(F32)).astype(BF16)
    bh = b.astype(BF16)
    bl = (b - bh.astype(F32)).astype(BF16)
    return _dot(ah, bh) + (_dot(ah, bl) + _dot(al, bh))


def _layer_norm(y, g, b):
    mu = jnp.mean(y, -1, keepdims=True)
    d = y - mu
    var = jnp.mean(d * d, -1, keepdims=True)
    return d * lax.rsqrt(var + LN_EPS) * g + b


def _sigmoid(x):
    return 1.0 / (1.0 + jnp.exp(-x))


def _silu(x):
    return x * _sigmoid(x)


IN_TM = 512
_C_Q, _C_QR, _C_KS, _C_KSR, _C_KW, _C_KWR = 0, 512, 1024, 1152, 1280, 1408
_C_CMP, _C_VS, _C_VW, _C_GQKV, _C_Z, _C_SM, _C_END = 1536, 1792, 1920, 2048, 3584, 4096, 4224


def _inproj_kernel(x_ref, w_ref, cos_ref, sin_ref, q_ref, ks_ref, kw_ref, cmp_ref, vs_ref, vw_ref,
                   gqkv_ref, z_ref, sm_ref):
    xb = x_ref[...].astype(BF16)
    cos = cos_ref[...]
    sin = sin_ref[...]

    def mm(c0, n):
        return _dot(xb, w_ref[:, c0:c0 + n])

    scale = NSA_HEAD_DIM ** -0.5 * LOG2E
    for j in range(NSA_Q // LANES):
        a = mm(_C_Q + j * LANES, LANES)
        r = mm(_C_QR + j * LANES, LANES)
        q_ref[:, j * LANES:(j + 1) * LANES] = ((a * cos + r * sin) * scale).astype(BF16)
    ks_ref[...] = (mm(_C_KS, LANES) * cos + mm(_C_KSR, LANES) * sin).astype(BF16)
    kw_ref[...] = (mm(_C_KW, LANES) * cos + mm(_C_KWR, LANES) * sin).astype(BF16)
    cmp_ref[...] = mm(_C_CMP, 2 * NSA_KV)
    vs_ref[...] = mm(_C_VS, NSA_KV).astype(BF16)
    vw_ref[...] = mm(_C_VW, NSA_KV).astype(BF16)
    gqkv_ref[...] = mm(_C_GQKV, 3 * GDN_W)
    z_ref[...] = mm(_C_Z, GDN_W)
    sm_ref[...] = mm(_C_SM, LANES)


def _rot_cols(w):
    k, n = w.shape
    half = NSA_HEAD_DIM // 2
    w4 = w.reshape(k, n // NSA_HEAD_DIM, 2, half)
    return jnp.stack([-w4[:, :, 1], w4[:, :, 0]], axis=2).reshape(k, n)


def _rope_tables(pos, reps):
    half = NSA_HEAD_DIM // 2
    inv = jnp.power(ROPE_THETA, -jnp.arange(half, dtype=F32) / half)
    ang = pos.astype(F32)[:, None] * inv[None, :]
    cos = jnp.tile(jnp.cos(ang), (1, 2 * reps))
    sin = jnp.tile(jnp.sin(ang), (1, 2 * reps))
    return cos, sin


def _in_projection(x2d, w_in, S):
    T = x2d.shape[0]
    o = 0
    parts = {}
    for name, n in (("q", NSA_Q), ("kc", NSA_KV), ("vc", NSA_KV), ("ks", NSA_KV), ("vs", NSA_KV),
                    ("kw", NSA_KV), ("vw", NSA_KV), ("gl", 3 * NSA_HEADS), ("gq", GDN_W), ("gk", GDN_W),
                    ("gv", GDN_W), ("gz", GDN_W), ("gb", GDN_HEADS), ("ga", GDN_HEADS)):
        parts[name] = w_in[:, o:o + n]
        o += n
    pad = jnp.zeros((D_MODEL, LANES - 3 * NSA_HEADS - 2 * GDN_HEADS), F32)
    w_all = jnp.concatenate([
        parts["q"], _rot_cols(parts["q"]), parts["ks"], _rot_cols(parts["ks"]),
        parts["kw"], _rot_cols(parts["kw"]), parts["kc"], parts["vc"], parts["vs"], parts["vw"],
        parts["gq"], parts["gk"], parts["gv"], parts["gz"], parts["gl"], parts["gb"], parts["ga"], pad,
    ], axis=1).astype(BF16)
    assert w_all.shape[1] == _C_END
    cos, sin = _rope_tables(jnp.arange(S), LANES // NSA_HEAD_DIM)
    nblk = S // IN_TM
    row = lambda i: (i, 0)
    outs = pl.pallas_call(
        _inproj_kernel,
        grid=(T // IN_TM,),
        in_specs=[
            pl.BlockSpec((IN_TM, D_MODEL), row),
            pl.BlockSpec((D_MODEL, _C_END), lambda i: (0, 0)),
            pl.BlockSpec((IN_TM, LANES), lambda i: (i % nblk, 0)),
            pl.BlockSpec((IN_TM, LANES), lambda i: (i % nblk, 0)),
        ],
        out_specs=[
            pl.BlockSpec((IN_TM, NSA_Q), row), pl.BlockSpec((IN_TM, NSA_KV), row),
            pl.BlockSpec((IN_TM, NSA_KV), row), pl.BlockSpec((IN_TM, 2 * NSA_KV), row),
            pl.BlockSpec((IN_TM, NSA_KV), row), pl.BlockSpec((IN_TM, NSA_KV), row),
            pl.BlockSpec((IN_TM, 3 * GDN_W), row), pl.BlockSpec((IN_TM, GDN_W), row),
            pl.BlockSpec((IN_TM, LANES), row),
        ],
        out_shape=[
            jax.ShapeDtypeStruct((T, NSA_Q), BF16), jax.ShapeDtypeStruct((T, NSA_KV), BF16),
            jax.ShapeDtypeStruct((T, NSA_KV), BF16), jax.ShapeDtypeStruct((T, 2 * NSA_KV), F32),
            jax.ShapeDtypeStruct((T, NSA_KV), BF16), jax.ShapeDtypeStruct((T, NSA_KV), BF16),
            jax.ShapeDtypeStruct((T, 3 * GDN_W), F32), jax.ShapeDtypeStruct((T, GDN_W), F32),
            jax.ShapeDtypeStruct((T, LANES), F32),
        ],
        compiler_params=_cparams("parallel"),
        name="in_projection",
    )(x2d, w_all, cos, sin)
    return outs


def _compress_kernel(k_ref, v_ref, wk_ref, wv_ref, pe_ref, cos_ref, sin_ref, kc_ref, vc_ref):
    nb = k_ref.shape[2]
    k = k_ref[0, 0]
    v = v_ref[0, 0]
    klo = (k + pe_ref[0:1, :]).astype(BF16)
    khi = (k + pe_ref[1:2, :]).astype(BF16)
    vlo = (v + pe_ref[2:3, :]).astype(BF16)
    vhi = (v + pe_ref[3:4, :]).astype(BF16)

    def up(m):
        return pltpu.roll(m, nb - 1, axis=0)

    a = _dot(klo, wk_ref[0]) + up(_dot(khi, wk_ref[1]))
    ar = _dot(klo, wk_ref[2]) + up(_dot(khi, wk_ref[3]))
    kc_ref[0] = a * cos_ref[...] + ar * sin_ref[...]
    vc_ref[0] = _dot(vlo, wv_ref[0]) + up(_dot(vhi, wv_ref[1]))


def _compress(cmp2d, pe_k, w_k, pe_v, w_v, B, S):
    nb = S // CMP_STRIDE
    bh = B * NSA_KV_HEADS
    row_w = CMP_STRIDE * NSA_HEAD_DIM
    c = cmp2d.reshape(B, S, 2, NSA_KV_HEADS, NSA_HEAD_DIM).transpose(2, 0, 3, 1, 4)
    c = c.reshape(2, bh, nb, row_w)
    wk = w_k.reshape(2, row_w, NSA_HEAD_DIM)
    wk_all = jnp.concatenate([wk, jnp.stack([_rot_cols(wk[0]), _rot_cols(wk[1])])], 0).astype(BF16)
    wv_all = w_v.reshape(2, row_w, NSA_HEAD_DIM).astype(BF16)
    pe = jnp.concatenate([pe_k.reshape(2, row_w), pe_v.reshape(2, row_w)], 0)
    cos, sin = _rope_tables(jnp.arange(nb) * CMP_STRIDE + (CMP_BLOCK - 1) / 2, 1)
    full = lambda shape: pl.BlockSpec(shape, lambda i: (0,) * len(shape))
    kc, vc = pl.pallas_call(
        _compress_kernel,
        grid=(bh,),
        in_specs=[
            pl.BlockSpec((1, 1, nb, row_w), lambda i: (0, i, 0, 0)),
            pl.BlockSpec((1, 1, nb, row_w), lambda i: (1, i, 0, 0)),
            full((4, row_w, NSA_HEAD_DIM)), full((2, row_w, NSA_HEAD_DIM)), full((4, row_w)),
            full((nb, NSA_HEAD_DIM)), full((nb, NSA_HEAD_DIM)),
        ],
        out_specs=[pl.BlockSpec((1, nb, NSA_HEAD_DIM), lambda i: (i, 0, 0))] * 2,
        out_shape=[jax.ShapeDtypeStruct((bh, nb, NSA_HEAD_DIM), F32)] * 2,
        compiler_params=_cparams("parallel"),
        name="nsa_compress",
    )(c, c, wk_all, wv_all, pe, cos, sin)
    return kc, vc


SEL_KT = 512


def _nsa_kernel(q_ref, kct_ref, vc_ref, ov_ref, kst_ref, vs_ref, kwt_ref, vw_ref, gate_ref, o_ref,
                *, n_sel, kt_w, win_w):
    G, QB, D = NSA_GROUP, Q_BLOCK, NSA_HEAD_DIM
    nb = kct_ref.shape[3]
    ns = ov_ref.shape[1]
    qb = pl.program_id(2)
    s0 = qb * QB
    q4 = q_ref[0, 0].reshape(G * QB, D)
    trow = s0 + lax.broadcasted_iota(jnp.int32, (QB, 1), 0)

    s_c = _dot(q4, kct_ref[0, 0]).reshape(G, QB, nb)
    ncol = lax.broadcasted_iota(jnp.int32, (QB, nb), 1)
    mask_c = ((ncol * CMP_STRIDE + (CMP_BLOCK - 1)) <= trow)[None]
    s_c = jnp.where(mask_c, s_c, NEG)
    m_c = jnp.max(s_c, -1, keepdims=True)
    e_c = jnp.where(mask_c, jnp.exp2(s_c - m_c), 0.0)
    l_c = jnp.sum(e_c, -1, keepdims=True)
    p_c = e_c / jnp.where(l_c > 0, l_c, 1.0)
    o_c = _dot(p_c.reshape(G * QB, nb).astype(BF16), vc_ref[0, 0])

    psum = p_c[0] + p_c[1] + p_c[2] + p_c[3]
    p_hi = psum.astype(BF16)
    p_lo = (psum - p_hi.astype(F32)).astype(BF16)
    imp_t = (_dot(p_hi, ov_ref[...]) + _dot(p_lo, ov_ref[...])).T
    jrow = lax.broadcasted_iota(jnp.int32, (ns, QB), 0)
    tcol = s0 + lax.broadcasted_iota(jnp.int32, (ns, QB), 1)
    cur = tcol // SLC_BLOCK
    forced = (jrow == 0) | ((jrow <= cur) & (jrow > cur - SLC_LOCAL))
    score = jnp.where((jrow * SLC_BLOCK) <= tcol, jnp.where(forced, FORCED_SCORE, imp_t), -1.0)
    jf = jrow.astype(F32)
    sel_t = jnp.zeros((ns, QB), F32)
    for _ in range(n_sel):
        mx = jnp.max(score, 0, keepdims=True)
        first = jnp.min(jnp.where(score == mx, jf, float(ns)), 0, keepdims=True)
        pick = jf == first
        sel_t = jnp.where(pick, 1.0, sel_t)
        score = jnp.where(pick, -2.0, score)
    selb = sel_t.T.astype(BF16)

    brow = lax.broadcasted_iota(jnp.int32, (ns, kt_w), 0)
    bcol = lax.broadcasted_iota(jnp.int32, (ns, kt_w), 1)
    kcol = lax.broadcasted_iota(jnp.int32, (QB, kt_w), 1)

    def sel_step(kt, carry):
        m, acc = carry
        k0 = pl.multiple_of(kt * kt_w, kt_w)
        s = _dot(q4, kst_ref[0, 0, :, pl.ds(k0, kt_w)]).reshape(G, QB, kt_w)
        expand = jnp.where((k0 + bcol) // SLC_BLOCK == brow, 1.0, 0.0).astype(BF16)
        chosen = _dot(selb, expand)
        mask = ((chosen > 0.5) & ((k0 + kcol) <= trow))[None]
        s = jnp.where(mask, s, NEG)
        m_new = jnp.maximum(m, jnp.max(s, -1, keepdims=True))
        alpha = jnp.exp2(m - m_new)
        p = jnp.exp2((s - m_new).astype(BF16))
        pv = _dot(p.reshape(G * QB, kt_w), vs_ref[0, 0, pl.ds(k0, kt_w), :])
        return m_new, alpha.reshape(G * QB, 1) * acc + pv

    n_kt = (s0 + QB + kt_w - 1) // kt_w
    _, acc_s = lax.fori_loop(0, n_kt, sel_step,
                             (jnp.full((G, QB, 1), NEG, F32), jnp.zeros((G * QB, 2 * D), F32)))
    o_s = acc_s[:, :D] / acc_s[:, D:D + 1]

    w0 = pl.multiple_of(jnp.maximum(s0 + QB - win_w, 0), QB)
    s_w = _dot(q4, kwt_ref[0, 0, :, pl.ds(w0, win_w)]).reshape(G, QB, win_w)
    rel = trow - (w0 + lax.broadcasted_iota(jnp.int32, (QB, win_w), 1))
    mask_w = ((rel >= 0) & (rel < WINDOW))[None]
    s_w = jnp.where(mask_w, s_w, NEG)
    p_w = jnp.exp2((s_w - jnp.max(s_w, -1, keepdims=True)).astype(BF16))
    acc_w = _dot(p_w.reshape(G * QB, win_w), vw_ref[0, 0, pl.ds(w0, win_w), :])
    o_w = acc_w[:, :D] / acc_w[:, D:D + 1]

    gate = _sigmoid(gate_ref[0, 0])
    for g in range(G):
        rows = slice(g * QB, (g + 1) * QB)
        o_ref[0, 0, g] = (gate[:, 3 * g:3 * g + 1] * o_c[rows] + gate[:, 3 * g + 1:3 * g + 2] * o_s[rows]
                          + gate[:, 3 * g + 2:3 * g + 3] * o_w[rows])


def _nsa_attention(q, ks, vs, kw, vw, kc, vc, small, B, S):
    H, HKV, G, D = NSA_HEADS, NSA_KV_HEADS, NSA_GROUP, NSA_HEAD_DIM
    nb = S // CMP_STRIDE
    ns = S // SLC_BLOCK
    n_sel = min(SLC_TOP_N, ns)
    kt_w = min(SEL_KT, S)
    win_w = min(WINDOW + Q_BLOCK, S)
    q5 = q.reshape(B, S, HKV, G, D).transpose(0, 2, 3, 1, 4)
    to_t = lambda t: t.reshape(B, S, HKV, D).transpose(0, 2, 3, 1)
    ones_col = jnp.concatenate([jnp.ones((B, HKV, S, 1), BF16), jnp.zeros((B, HKV, S, D - 1), BF16)], -1)
    to_h = lambda t: jnp.concatenate([t.reshape(B, S, HKV, D).transpose(0, 2, 1, 3), ones_col], -1)
    kct = kc.reshape(B, HKV, nb, D).transpose(0, 1, 3, 2).astype(BF16)
    vcb = vc.reshape(B, HKV, nb, D).astype(BF16)
    gates = small[:, :3 * H].reshape(B, S, HKV, 3 * G).transpose(0, 2, 1, 3)
    n_i = jnp.arange(nb)[:, None]
    m_i = jnp.arange(ns)[None, :]
    overlap = ((n_i * CMP_STRIDE <= m_i * SLC_BLOCK + SLC_BLOCK - 1)
               & (n_i * CMP_STRIDE + CMP_BLOCK - 1 >= m_i * SLC_BLOCK)).astype(BF16)
    per_bh = lambda shape: pl.BlockSpec((1, 1) + shape, lambda b, h, i: (b, h, 0, 0))
    o = pl.pallas_call(
        functools.partial(_nsa_kernel, n_sel=n_sel, kt_w=kt_w, win_w=win_w),
        grid=(B, HKV, S // Q_BLOCK),
        in_specs=[
            pl.BlockSpec((1, 1, G, Q_BLOCK, D), lambda b, h, i: (b, h, 0, i, 0)),
            per_bh((D, nb)), per_bh((nb, D)),
            pl.BlockSpec((nb, ns), lambda b, h, i: (0, 0)),
            per_bh((D, S)), per_bh((S, 2 * D)), per_bh((D, S)), per_bh((S, 2 * D)),
            pl.BlockSpec((1, 1, Q_BLOCK, 3 * G), lambda b, h, i: (b, h, i, 0)),
        ],
        out_specs=pl.BlockSpec((1, 1, G, Q_BLOCK, D), lambda b, h, i: (b, h, 0, i, 0)),
        out_shape=jax.ShapeDtypeStruct((B, HKV, G, S, D), F32),
        compiler_params=_cparams("parallel", "parallel", "parallel"),
        name="nsa_attention",
    )(q5, kct, vcb, overlap, to_t(ks), to_h(vs), to_t(kw), to_h(vw), gates)
    return o.transpose(0, 3, 1, 2, 4).reshape(B * S, H * D)


GDN_TB = 256
GDN_HALO = 8


def _gdn_prep_kernel(qkv_ref, halo_ref, sm_ref, cw_ref, alog_ref, dtb_ref,
                     u_ref, wq_ref, kd_ref, intra_ref, dec_ref, xs_scr, y_scr, bg_scr):
    C, HD, NH, BD = GDN_CHUNK, GDN_HEAD_DIM, GDN_HEADS, GDN_BD
    tb = qkv_ref.shape[1]
    first = pl.program_id(1) == 0
    xs_scr[GDN_HALO:GDN_HALO + tb, :] = qkv_ref[0]
    xs_scr[0:GDN_HALO, :] = jnp.where(first, 0.0, halo_ref[0])
    y = cw_ref[GDN_CONV - 1:GDN_CONV, :] * xs_scr[GDN_HALO:GDN_HALO + tb, :]
    for d in range(1, GDN_CONV):
        y = y + cw_ref[GDN_CONV - 1 - d:GDN_CONV - d, :] * xs_scr[GDN_HALO - d:GDN_HALO - d + tb, :]
    y_scr[...] = _silu(y)
    sm = sm_ref[0]
    b_logit = sm[:, 3 * NSA_HEADS:3 * NSA_HEADS + NH]
    a_logit = sm[:, 3 * NSA_HEADS + NH:3 * NSA_HEADS + 2 * NH] + dtb_ref[...]
    softplus = jnp.maximum(a_logit, 0.0) + jnp.log(1.0 + jnp.exp(-jnp.abs(a_logit)))
    bg_scr[:, 0:NH] = _sigmoid(b_logit)
    bg_scr[:, NH:2 * NH] = -jnp.exp(alog_ref[...]) * softplus

    r = lax.broadcasted_iota(jnp.int32, (BD, BD), 0)
    c = lax.broadcasted_iota(jnp.int32, (BD, BD), 1)
    same = (r // C) == (c // C)
    eye = r == c
    tril = same & (r >= c)
    triu = same & (r <= c)
    strict = same & (r > c)
    eye_f = jnp.where(eye, 1.0, 0.0)

    def stack_heads(m, c0):
        return jnp.concatenate([m[:, c0 + h * HD:c0 + (h + 1) * HD] for h in range(NH)], axis=0)

    def chunk(ci, carry):
        r0 = pl.multiple_of(ci * C, C)
        yc = y_scr[pl.ds(r0, C), :]
        bg = bg_scr[pl.ds(r0, C), :]
        q4 = stack_heads(yc, 0)
        k4 = stack_heads(yc, GDN_W)
        v4 = stack_heads(yc, 2 * GDN_W)
        q4 = q4 * lax.rsqrt(jnp.sum(q4 * q4, -1, keepdims=True) + NORM_EPS) * (HD ** -0.5)
        k4 = k4 * lax.rsqrt(jnp.sum(k4 * k4, -1, keepdims=True) + NORM_EPS)
        b4 = jnp.concatenate([bg[:, h:h + 1] for h in range(NH)], axis=0)
        g4 = jnp.concatenate([bg[:, NH + h:NH + h + 1] for h in range(NH)], axis=0)
        g_row = jnp.sum(jnp.where(eye, g4, 0.0), axis=0, keepdims=True)
        gc_col = jnp.sum(jnp.where(tril, g_row, 0.0), axis=1, keepdims=True)
        gc_row = jnp.sum(jnp.where(triu, g4, 0.0), axis=0, keepdims=True)
        gl_col = jnp.sum(jnp.where(same, g_row, 0.0), axis=1, keepdims=True)
        decay = jnp.where(tril, jnp.exp(jnp.minimum(gc_col - gc_row, 0.0)), 0.0)
        kb4 = k4 * b4
        k4b = k4.astype(BF16)
        lmat = jnp.where(strict, _dot_nt(kb4.astype(BF16), k4b) * decay, 0.0)
        intra = jnp.where(tril, _dot_nt(q4.astype(BF16), k4b) * decay, 0.0)
        inv = eye_f - lmat
        pw = lmat
        for _ in range(int(math.log2(C)) - 1):
            pw = _dot3(pw, pw)
            inv = _dot3(inv, eye_f + pw)
        egc = jnp.exp(gc_col)
        uw = _dot3(inv, jnp.concatenate([v4 * b4, kb4 * egc], axis=1))
        u_ref[0, ci] = uw[:, :HD]
        w4 = uw[:, HD:].astype(BF16).reshape(NH, C, HD)
        qg4 = (q4 * egc).astype(BF16).reshape(NH, C, HD)
        wq_ref[0, ci] = jnp.concatenate([w4, qg4], axis=1)
        kd_ref[0, ci] = (k4 * jnp.exp(gl_col - gc_col)).astype(BF16)
        intra_ref[0, ci] = intra.astype(BF16)
        dec = jnp.broadcast_to(jnp.exp(gl_col), (BD, LANES)).reshape(NH, C, LANES)
        dec_ref[0, ci] = dec[:, 0:8, :]
        return carry

    lax.fori_loop(0, tb // C, chunk, 0)


def _gdn_scan_kernel(u_ref, wq_ref, kd_ref, intra_ref, dec_ref, z_ref, nw_ref, o_ref, s_scr):
    C, HD, NH = GDN_CHUNK, GDN_HEAD_DIM, GDN_HEADS
    nbatch = u_ref.shape[0]

    @pl.when(pl.program_id(0) == 0)
    def _():
        s_scr[...] = jnp.zeros_like(s_scr)

    nw = nw_ref[...]
    for b in range(nbatch):
        vnew, qs = [], []
        for h in range(NH):
            st = s_scr[b * NH + h]
            res = _dot(wq_ref[b, 0, h], st.astype(BF16))
            vnew.append(u_ref[b, 0, h * C:(h + 1) * C, :] - res[:C])
            qs.append(res[C:])
        vnew4 = jnp.concatenate(vnew, axis=0).astype(BF16)
        o4 = jnp.concatenate(qs, axis=0) + _dot(intra_ref[b, 0], vnew4)
        for h in range(NH):
            rows = slice(h * C, (h + 1) * C)
            st = s_scr[b * NH + h]
            s_scr[b * NH + h] = st * dec_ref[b, 0, h, 0:1, :] + _dot_tn(kd_ref[b, 0, rows, :], vnew4[rows])
            oh = o4[rows]
            oh = oh * lax.rsqrt(jnp.mean(oh * oh, -1, keepdims=True) + NORM_EPS) * nw
            o_ref[b, :, h * HD:(h + 1) * HD] = oh * _silu(z_ref[b, :, h * HD:(h + 1) * HD])


def _gdn_mixer(gqkv, z, small, conv_w, a_log, dt_bias, norm_w, B, S):
    C, HD, NH, BD = GDN_CHUNK, GDN_HEAD_DIM, GDN_HEADS, GDN_BD
    nc = S // C
    tb = min(GDN_TB, S)
    cb = tb // C
    qkv3 = gqkv.reshape(B, S, 3 * GDN_W)
    sm3 = small.reshape(B, S, LANES)
    hb = tb // GDN_HALO
    u, wq, kd, intra, dec = pl.pallas_call(
        _gdn_prep_kernel,
        grid=(B, S // tb),
        in_specs=[
            pl.BlockSpec((1, tb, 3 * GDN_W), lambda b, i: (b, i, 0)),
            pl.BlockSpec((1, GDN_HALO, 3 * GDN_W), lambda b, i: (b, jnp.maximum(i * hb - 1, 0), 0)),
            pl.BlockSpec((1, tb, LANES), lambda b, i: (b, i, 0)),
            pl.BlockSpec((GDN_CONV, 3 * GDN_W), lambda b, i: (0, 0)),
            pl.BlockSpec((1, NH), lambda b, i: (0, 0)),
            pl.BlockSpec((1, NH), lambda b, i: (0, 0)),
        ],
        out_specs=[
            pl.BlockSpec((1, cb, BD, HD), lambda b, i: (b, i, 0, 0)),
            pl.BlockSpec((1, cb, NH, 2 * C, HD), lambda b, i: (b, i, 0, 0, 0)),
            pl.BlockSpec((1, cb, BD, HD), lambda b, i: (b, i, 0, 0)),
            pl.BlockSpec((1, cb, BD, BD), lambda b, i: (b, i, 0, 0)),
            pl.BlockSpec((1, cb, NH, 8, LANES), lambda b, i: (b, i, 0, 0, 0)),
        ],
        out_shape=[
            jax.ShapeDtypeStruct((B, nc, BD, HD), F32),
            jax.ShapeDtypeStruct((B, nc, NH, 2 * C, HD), BF16),
            jax.ShapeDtypeStruct((B, nc, BD, HD), BF16),
            jax.ShapeDtypeStruct((B, nc, BD, BD), BF16),
            jax.ShapeDtypeStruct((B, nc, NH, 8, LANES), F32),
        ],
        scratch_shapes=[
            pltpu.VMEM((GDN_HALO + tb, 3 * GDN_W), F32),
            pltpu.VMEM((tb, 3 * GDN_W), F32),
            pltpu.VMEM((tb, 2 * NH), F32),
        ],
        compiler_params=_cparams("parallel", "parallel"),
        name="gdn_prepare",
    )(qkv3, qkv3, sm3, conv_w, a_log.reshape(1, NH), dt_bias.reshape(1, NH))

    o = pl.pallas_call(
        _gdn_scan_kernel,
        grid=(nc,),
        in_specs=[
            pl.BlockSpec((B, 1, BD, HD), lambda c: (0, c, 0, 0)),
            pl.BlockSpec((B, 1, NH, 2 * C, HD), lambda c: (0, c, 0, 0, 0)),
            pl.BlockSpec((B, 1, BD, HD), lambda c: (0, c, 0, 0)),
            pl.BlockSpec((B, 1, BD, BD), lambda c: (0, c, 0, 0)),
            pl.BlockSpec((B, 1, NH, 8, LANES), lambda c: (0, c, 0, 0, 0)),
            pl.BlockSpec((B, C, GDN_W), lambda c: (0, c, 0)),
            pl.BlockSpec((1, HD), lambda c: (0, 0)),
        ],
        out_specs=pl.BlockSpec((B, C, GDN_W), lambda c: (0, c, 0)),
        out_shape=jax.ShapeDtypeStruct((B, S, GDN_W), F32),
        scratch_shapes=[pltpu.VMEM((B * NH, HD, HD), F32)],
        compiler_params=_cparams("arbitrary"),
        name="gdn_scan",
    )(u, wq, kd, intra, dec, z.reshape(B, S, GDN_W), norm_w.reshape(1, HD))
    return o.reshape(B * S, GDN_W)


MIX_TM = 512


def _mix_ln_kernel(x_ref, a_ref, b_ref, w_ref, g_ref, be_ref, o_ref):
    mix = (_dot(a_ref[...].astype(BF16), w_ref[0:NSA_Q, :])
           + _dot(b_ref[...].astype(BF16), w_ref[NSA_Q:NSA_Q + GDN_W, :]))
    o_ref[...] = _layer_norm(DN_ALPHA * x_ref[...] + mix, g_ref[...], be_ref[...])


def _mix_out_ln(x2d, o_a, o_b, w_out, g, b):
    T = x2d.shape[0]
    row = lambda i: (i, 0)
    const = lambda i: (0, 0)
    return pl.pallas_call(
        _mix_ln_kernel,
        grid=(T // MIX_TM,),
        in_specs=[
            pl.BlockSpec((MIX_TM, D_MODEL), row), pl.BlockSpec((MIX_TM, NSA_Q), row),
            pl.BlockSpec((MIX_TM, GDN_W), row), pl.BlockSpec((NSA_Q + GDN_W, D_MODEL), const),
            pl.BlockSpec((1, D_MODEL), const), pl.BlockSpec((1, D_MODEL), const),
        ],
        out_specs=pl.BlockSpec((MIX_TM, D_MODEL), row),
        out_shape=jax.ShapeDtypeStruct((T, D_MODEL), F32),
        compiler_params=_cparams("parallel"),
        name="mix_out_ln",
    )(x2d, o_a, o_b, w_out.astype(BF16), g.reshape(1, D_MODEL), b.reshape(1, D_MODEL))


FFN_TM = 1024
FFN_TF = 256


def _ffn_ln_kernel(x_ref, wg_ref, wu_ref, wd_ref, g_ref, be_ref, o_ref, xb_scr, acc_scr):
    j = pl.program_id(1)

    @pl.when(j == 0)
    def _():
        xb_scr[...] = x_ref[...].astype(BF16)
        acc_scr[...] = jnp.zeros_like(acc_scr)

    xb = xb_scr[...]
    hid = _silu(_dot(xb, wg_ref[...])) * _dot(xb, wu_ref[...])
    acc_scr[...] += _dot(hid.astype(BF16), wd_ref[...])

    @pl.when(j == pl.num_programs(1) - 1)
    def _():
        o_ref[...] = _layer_norm(DN_ALPHA * x_ref[...] + acc_scr[...], g_ref[...], be_ref[...])


def _ffn_ln(x2d, wg, wu, wd, g, b):
    T = x2d.shape[0]
    dff = wg.shape[1]
    tm = min(FFN_TM, T)
    return pl.pallas_call(
        _ffn_ln_kernel,
        grid=(T // tm, dff // FFN_TF),
        in_specs=[
            pl.BlockSpec((tm, D_MODEL), lambda i, j: (i, 0)),
            pl.BlockSpec((D_MODEL, FFN_TF), lambda i, j: (0, j)),
            pl.BlockSpec((D_MODEL, FFN_TF), lambda i, j: (0, j)),
            pl.BlockSpec((FFN_TF, D_MODEL), lambda i, j: (j, 0)),
            pl.BlockSpec((1, D_MODEL), lambda i, j: (0, 0)),
            pl.BlockSpec((1, D_MODEL), lambda i, j: (0, 0)),
        ],
        out_specs=pl.BlockSpec((tm, D_MODEL), lambda i, j: (i, 0)),
        out_shape=jax.ShapeDtypeStruct((T, D_MODEL), F32),
        scratch_shapes=[pltpu.VMEM((tm, D_MODEL), BF16), pltpu.VMEM((tm, D_MODEL), F32)],
        compiler_params=_cparams("parallel", "arbitrary"),
        name="ffn_ln",
    )(x2d, wg.astype(BF16), wu.astype(BF16), wd.astype(BF16), g.reshape(1, D_MODEL), b.reshape(1, D_MODEL))


POOL_TM = 512
POOL_HALO = 16


def _route_top2(x1, rw):
    lane = lax.broadcasted_iota(jnp.int32, (x1.shape[0], LANES), 1)
    lanef = lane.astype(F32)
    logits = jnp.where(lane < N_EXPERTS, _dot3(x1, rw), NEG)
    ex = jnp.exp(logits - jnp.max(logits, -1, keepdims=True))
    probs = jnp.where(lane < N_EXPERTS, ex / jnp.sum(ex, -1, keepdims=True), -1.0)
    p1 = jnp.max(probs, -1, keepdims=True)
    i1 = jnp.min(jnp.where(probs == p1, lanef, float(LANES)), -1, keepdims=True)
    rest = jnp.where(lanef == i1, -1.0, probs)
    p2 = jnp.max(rest, -1, keepdims=True)
    i2 = jnp.min(jnp.where(rest == p2, lanef, float(LANES)), -1, keepdims=True)
    den = p1 + p2
    return jnp.where(lane == 0, i1, jnp.where(lane == 1, i2, jnp.where(
        lane == 2, p1 / den, jnp.where(lane == 3, p2 / den, 0.0))))


def _pool_ln_kernel(x_ref, halo_ref, pw_ref, ps_ref, g_ref, be_ref, rw_ref, o_ref, route_ref, xs_scr):
    tm = x_ref.shape[1]
    i = pl.program_id(1)
    x = x_ref[0]
    xs_scr[POOL_HALO:POOL_HALO + tm, :] = x
    xs_scr[0:POOL_HALO, :] = jnp.where(i == 0, 0.0, halo_ref[0])
    t1 = (i * tm + 1 + lax.broadcasted_iota(jnp.int32, (tm, 1), 0)).astype(F32)
    for gi, win in enumerate(POOL_SIZES):
        cols = slice(gi * POOL_GROUP, (gi + 1) * POOL_GROUP)
        acc = x[:, cols]
        for d in range(1, win):
            acc = acc + xs_scr[POOL_HALO - d:POOL_HALO - d + tm, cols]
        mean = acc / jnp.minimum(t1, float(win))
        mix = _dot((mean - x[:, cols]).astype(BF16), pw_ref[gi]) * ps_ref[:, cols]
        xs_scr[POOL_HALO:POOL_HALO + tm, cols] = DN_ALPHA * x[:, cols] + mix
    x1 = _layer_norm(xs_scr[POOL_HALO:POOL_HALO + tm, :], g_ref[...], be_ref[...])
    o_ref[0] = x1
    route_ref[0] = _route_top2(x1, rw_ref[...])


def _pool_ln(x3d, pool_w, pool_scale, g, b, router_w):
    B, S, _ = x3d.shape
    tm = min(POOL_TM, S)
    hb = tm // POOL_HALO
    const = lambda bb, i: (0, 0)
    rw = jnp.concatenate([router_w, jnp.zeros((D_MODEL, LANES - N_EXPERTS), F32)], axis=1)
    return pl.pallas_call(
        _pool_ln_kernel,
        grid=(B, S // tm),
        in_specs=[
            pl.BlockSpec((1, tm, D_MODEL), lambda bb, i: (bb, i, 0)),
            pl.BlockSpec((1, POOL_HALO, D_MODEL), lambda bb, i: (bb, jnp.maximum(i * hb - 1, 0), 0)),
            pl.BlockSpec((len(POOL_SIZES), POOL_GROUP, POOL_GROUP), lambda bb, i: (0, 0, 0)),
            pl.BlockSpec((1, D_MODEL), const), pl.BlockSpec((1, D_MODEL), const),
            pl.BlockSpec((1, D_MODEL), const), pl.BlockSpec((D_MODEL, LANES), const),
        ],
        out_specs=[pl.BlockSpec((1, tm, D_MODEL), lambda bb, i: (bb, i, 0)),
                   pl.BlockSpec((1, tm, LANES), lambda bb, i: (bb, i, 0))],
        out_shape=[jax.ShapeDtypeStruct((B, S, D_MODEL), F32), jax.ShapeDtypeStruct((B, S, LANES), F32)],
        scratch_shapes=[pltpu.VMEM((POOL_HALO + tm, D_MODEL), F32)],
        compiler_params=_cparams("parallel", "parallel"),
        name="pool_ln",
    )(x3d, x3d, pool_w.astype(BF16), pool_scale.reshape(1, D_MODEL), g.reshape(1, D_MODEL),
      b.reshape(1, D_MODEL), rw)


MOE_TM = 1024
MOE_TF = 512
MOE_TC = 512


def _moe_plan(route, T, tm):
    ne = N_EXPERTS
    e_flat = jnp.concatenate([route[:, 0], route[:, 1]]).astype(jnp.int32)
    onehot = (e_flat[:, None] == jnp.arange(ne)[None, :]).astype(jnp.int32)
    csum = jnp.cumsum(onehot, axis=0)
    rank = jnp.sum(onehot * csum, axis=1) - 1
    counts = csum[-1]
    padded = (counts + tm - 1) // tm * tm
    gend = jnp.cumsum(padded)
    gstart = gend - padded
    pos = gstart[e_flat] + rank
    n_tiles = (2 * T) // tm + ne
    tile_e = jnp.minimum(jnp.searchsorted(gend, jnp.arange(n_tiles) * tm, side="right"), ne - 1)
    tile_e = tile_e.astype(jnp.int32)
    n_used = (gend[-1] // tm).astype(jnp.int32).reshape(1)
    order = jnp.argsort(e_flat, stable=True)
    cstart = jnp.cumsum(counts) - counts
    e_p = jnp.repeat(tile_e, tm)
    r = jnp.clip(jnp.arange(n_tiles * tm) - gstart[e_p], 0, jnp.maximum(counts[e_p] - 1, 0))
    src = (order[jnp.clip(cstart[e_p] + r, 0, 2 * T - 1)] % T).astype(jnp.int32)
    return pos.astype(jnp.int32), src, tile_e, n_used, n_tiles


def _row_copy(src_hbm, row, dst, slot, sem):
    return pltpu.make_async_copy(src_hbm.at[pl.ds(row, 1)], dst.at[pl.ds(slot, 1)], sem)


def _moe_expert_kernel(te_ref, nu_ref, src_ref, x_hbm, wg_ref, wu_ref, wd_ref, o_ref,
                       xbuf, xb_scr, acc_scr, sem):
    i = pl.program_id(0)
    j = pl.program_id(1)
    tm = xbuf.shape[0]
    used = i < nu_ref[0]

    @pl.when(used & (j == 0))
    def _():
        def issue(r, c):
            _row_copy(x_hbm, src_ref[0, 0, r], xbuf, r, sem).start()
            return c

        def drain(r, c):
            _row_copy(x_hbm, 0, xbuf, r, sem).wait()
            return c

        lax.fori_loop(0, tm, issue, 0)
        lax.fori_loop(0, tm, drain, 0)
        xb_scr[...] = xbuf[...].astype(BF16)
        acc_scr[...] = jnp.zeros_like(acc_scr)

    @pl.when(used)
    def _():
        xb = xb_scr[...]
        hid = _silu(_dot(xb, wg_ref[0])) * _dot(xb, wu_ref[0])
        acc_scr[...] += _dot(hid.astype(BF16), wd_ref[0])

    last = j == pl.num_programs(1) - 1

    @pl.when(used & last)
    def _():
        o_ref[...] = acc_scr[...]

    @pl.when(jnp.logical_not(used) & last)
    def _():
        o_ref[...] = jnp.zeros_like(o_ref)


def _moe_combine_kernel(pos_ref, x_ref, route_ref, ys_hbm, g_ref, be_ref, o_ref, ybuf, sem):
    tc = x_ref.shape[0]

    def issue(r, c):
        _row_copy(ys_hbm, pos_ref[0, 0, r], ybuf.at[0], r, sem).start()
        _row_copy(ys_hbm, pos_ref[0, 1, r], ybuf.at[1], r, sem).start()
        return c

    def drain(r, c):
        _row_copy(ys_hbm, 0, ybuf.at[0], r, sem).wait()
        _row_copy(ys_hbm, 0, ybuf.at[1], r, sem).wait()
        return c

    lax.fori_loop(0, tc, issue, 0)
    lax.fori_loop(0, tc, drain, 0)
    route = route_ref[...]
    y = DN_ALPHA * x_ref[...] + route[:, 2:3] * ybuf[0] + route[:, 3:4] * ybuf[1]
    o_ref[...] = _layer_norm(y, g_ref[...], be_ref[...])


def _moe_ln(x2d, route, wg, wu, wd, g, b):
    T = x2d.shape[0]
    ne, _, dff = wg.shape
    tm = min(MOE_TM, T)
    tc = min(MOE_TC, T)
    pos, src, tile_e, n_used, n_tiles = _moe_plan(route, T, tm)

    def w_col(i, j, te, nu):
        return (te[i], 0, jnp.where(i < nu[0], j, 0))

    def w_row(i, j, te, nu):
        return (te[i], jnp.where(i < nu[0], j, 0), 0)

    ys = pl.pallas_call(
        _moe_expert_kernel,
        grid_spec=pltpu.PrefetchScalarGridSpec(
            num_scalar_prefetch=2,
            grid=(n_tiles, dff // MOE_TF),
            in_specs=[
                pl.BlockSpec((1, 1, tm), lambda i, j, te, nu: (i, 0, 0), memory_space=pltpu.SMEM),
                pl.BlockSpec(memory_space=pl.ANY),
                pl.BlockSpec((1, D_MODEL, MOE_TF), w_col),
                pl.BlockSpec((1, D_MODEL, MOE_TF), w_col),
                pl.BlockSpec((1, MOE_TF, D_MODEL), w_row),
            ],
            out_specs=pl.BlockSpec((tm, D_MODEL), lambda i, j, te, nu: (i, 0)),
            scratch_shapes=[pltpu.VMEM((tm, D_MODEL), F32), pltpu.VMEM((tm, D_MODEL), BF16),
                            pltpu.VMEM((tm, D_MODEL), F32), pltpu.SemaphoreType.DMA(())],
        ),
        out_shape=jax.ShapeDtypeStruct((n_tiles * tm, D_MODEL), F32),
        compiler_params=_cparams("arbitrary", "arbitrary"),
        name="moe_experts",
    )(tile_e, n_used, src.reshape(n_tiles, 1, tm), x2d, wg.astype(BF16), wu.astype(BF16), wd.astype(BF16))

    pos3 = pos.reshape(2, T // tc, tc).transpose(1, 0, 2)
    row = lambda i: (i, 0)
    const = lambda i: (0, 0)
    return pl.pallas_call(
        _moe_combine_kernel,
        grid=(T // tc,),
        in_specs=[
            pl.BlockSpec((1, 2, tc), lambda i: (i, 0, 0), memory_space=pltpu.SMEM),
            pl.BlockSpec((tc, D_MODEL), row), pl.BlockSpec((tc, LANES), row),
            pl.BlockSpec(memory_space=pl.ANY),
            pl.BlockSpec((1, D_MODEL), const), pl.BlockSpec((1, D_MODEL), const),
        ],
        out_specs=pl.BlockSpec((tc, D_MODEL), row),
        out_shape=jax.ShapeDtypeStruct((T, D_MODEL), F32),
        scratch_shapes=[pltpu.VMEM((2, tc, D_MODEL), F32), pltpu.SemaphoreType.DMA(())],
        compiler_params=_cparams("arbitrary"),
        name="moe_combine_ln",
    )(pos3, x2d, route, ys, g.reshape(1, D_MODEL), b.reshape(1, D_MODEL))


def _even_layer(x2d, B, S, w_in, pe_k, w_k, pe_v, w_v, conv_w, a_log, dt_bias, gdn_norm, w_out,
                ln1_g, ln1_b, wg, wu, wd, ln2_g, ln2_b):
    q, ks, kw, cmp2d, vs, vw, gqkv, z, small = _in_projection(x2d, w_in, S)
    kc, vc = _compress(cmp2d, pe_k, w_k, pe_v, w_v, B, S)
    o_a = _nsa_attention(q, ks, vs, kw, vw, kc, vc, small, B, S)
    o_b = _gdn_mixer(gqkv, z, small, conv_w, a_log, dt_bias, gdn_norm, B, S)
    x2d = _mix_out_ln(x2d, o_a, o_b, w_out, ln1_g, ln1_b)
    return _ffn_ln(x2d, wg, wu, wd, ln2_g, ln2_b)


def _odd_layer(x2d, B, S, pool_w, pool_scale, ln1_g, ln1_b, router_w, wg, wu, wd, ln2_g, ln2_b):
    x3d, route = _pool_ln(x2d.reshape(B, S, D_MODEL), pool_w, pool_scale, ln1_g, ln1_b, router_w)
    return _moe_ln(x3d.reshape(B * S, D_MODEL), route.reshape(B * S, LANES), wg, wu, wd, ln2_g, ln2_b)


def kernel(x, ev_w_in, ev_cmp_pe_k, ev_cmp_w_k, ev_cmp_pe_v, ev_cmp_w_v, ev_conv_w, ev_a_log, ev_dt_bias,
           ev_gdn_norm, ev_w_out, ev_ln1_g, ev_ln1_b, ev_ffn_wg, ev_ffn_wu, ev_ffn_wd, ev_ln2_g, ev_ln2_b,
           od_pool_w, od_pool_scale, od_ln1_g, od_ln1_b, od_router_w, od_exp_wg, od_exp_wu, od_exp_wd,
           od_ln2_g, od_ln2_b):
    B, S, _ = x.shape
    h = x.reshape(B * S, D_MODEL)
    for layer in range(DEPTH):
        i = layer // 2
        if layer % 2 == 0:
            h = _even_layer(h, B, S, ev_w_in[i], ev_cmp_pe_k[i], ev_cmp_w_k[i], ev_cmp_pe_v[i],
                            ev_cmp_w_v[i], ev_conv_w[i], ev_a_log[i], ev_dt_bias[i], ev_gdn_norm[i],
                            ev_w_out[i], ev_ln1_g[i], ev_ln1_b[i], ev_ffn_wg[i], ev_ffn_wu[i],
                            ev_ffn_wd[i], ev_ln2_g[i], ev_ln2_b[i])
        else:
            h = _odd_layer(h, B, S, od_pool_w[i], od_pool_scale[i], od_ln1_g[i], od_ln1_b[i],
                           od_router_w[i], od_exp_wg[i], od_exp_wu[i], od_exp_wd[i], od_ln2_g[i],
                           od_ln2_b[i])
    return h.reshape(B, S, D_MODEL)
```

```python
import functools
import math

import jax
import jax.numpy as jnp
from jax import lax
from jax.experimental import pallas as pl
from jax.experimental.pallas import tpu as pltpu

F32 = jnp.float32
BF16 = jnp.bfloat16

D_MODEL = 1024
DEPTH = 2
NSA_HEADS = 8
NSA_KV_HEADS = 2
NSA_GROUP = NSA_HEADS // NSA_KV_HEADS
NSA_HEAD_DIM = 64
CMP_BLOCK = 32
CMP_STRIDE = 16
SLC_BLOCK = 64
SLC_TOP_N = 16
SLC_LOCAL = 2
WINDOW = 512
Q_BLOCK = 128
FORCED_SCORE = 1e6
GDN_HEADS = 4
GDN_HEAD_DIM = 128
GDN_CONV = 4
GDN_CHUNK = 64
POOL_SIZES = (2, 4, 8, 16)
POOL_GROUP = D_MODEL // 4
D_FF = 2816
N_EXPERTS = 8
D_FF_EXPERT = 3584
ROPE_THETA = 10000.0
LN_EPS = 1e-5
NORM_EPS = 1e-6
DN_ALPHA = (2 * DEPTH) ** 0.25

NSA_Q = NSA_HEADS * NSA_HEAD_DIM
NSA_KV = NSA_KV_HEADS * NSA_HEAD_DIM
GDN_W = GDN_HEADS * GDN_HEAD_DIM
GDN_BD = GDN_HEADS * GDN_CHUNK

LANES = 128
NEG = -1e30
LOG2E = 1.4426950408889634
VMEM_LIMIT = 56 * 1024 * 1024


def _cparams(*sem):
    return pltpu.CompilerParams(dimension_semantics=sem, vmem_limit_bytes=VMEM_LIMIT)


def _dot(a, b):
    return jnp.dot(a, b, preferred_element_type=F32)


def _dot_nt(a, b):
    return lax.dot_general(a, b, (((1,), (1,)), ((), ())), preferred_element_type=F32)


def _dot_tn(a, b):
    return lax.dot_general(a, b, (((0,), (0,)), ((), ())), preferred_element_type=F32)


def _dot3(a, b):
    ah = a.astype(BF16)
    al = (a - ah.astype(F32)).astype(BF16)
    bh = b.astype(BF16)
    bl = (b - bh.astype(F32)).astype(BF16)
    return _dot(ah, bh) + (_dot(ah, bl) + _dot(al, bh))


def _dot2(a, b):
    ah = a.astype(BF16)
    al = (a - ah.astype(F32)).astype(BF16)
    bh = b.astype(BF16)
    return _dot(ah, bh) + _dot(al, bh)


def _layer_norm(y, g, b):
    mu = jnp.mean(y, -1, keepdims=True)
    d = y - mu
    var = jnp.mean(d * d, -1, keepdims=True)
    return d * lax.rsqrt(var + LN_EPS) * g + b


def _sigmoid(x):
    return 1.0 / (1.0 + jnp.exp(-x))


def _silu(x):
    return x * _sigmoid(x)


IN_TM = 512
_R_Q, _R_QR, _R_VS, _R_VW, _R_END = 0, 512, 1024, 1152, 1280
_C_KS, _C_KSR, _C_KW, _C_KWR, _C_CMP, _C_GQKV, _C_Z, _C_SM, _C_END = 0, 128, 256, 384, 512, 768, 2304, 2816, 2944


def _inproj_kernel(x_ref, wt_ref, wn_ref, cos_ref, sin_ref, cost_ref, sint_ref,
                   qt_ref, vst_ref, vwt_ref, ks_ref, kw_ref, cmp_ref, gqkv_ref, z_ref, sm_ref):
    D, HKV = NSA_HEAD_DIM, NSA_KV_HEADS
    tm = x_ref.shape[0]
    xb = x_ref[...].astype(BF16)

    def mt(r0, n):
        return _dot_nt(wt_ref[r0:r0 + n, :], xb)

    def mm(c0, n):
        return _dot(xb, wn_ref[:, c0:c0 + n])

    scale = D ** -0.5 * LOG2E
    cost = cost_ref[...]
    sint = sint_ref[...]
    for hd in range(NSA_HEADS):
        a = mt(_R_Q + hd * D, D)
        r = mt(_R_QR + hd * D, D)
        qt_ref[hd] = ((a * cost + r * sint) * scale).astype(BF16)
    ones_rows = jnp.where(lax.broadcasted_iota(jnp.int32, (D, tm), 0) == 0, 1.0, 0.0).astype(BF16)
    for r0, out in ((_R_VS, vst_ref), (_R_VW, vwt_ref)):
        vt = mt(r0, NSA_KV).astype(BF16)
        for h in range(HKV):
            out[h, 0:D, :] = vt[h * D:(h + 1) * D]
            out[h, D:2 * D, :] = ones_rows
    cos = cos_ref[...]
    sin = sin_ref[...]
    ks_ref[...] = (mm(_C_KS, LANES) * cos + mm(_C_KSR, LANES) * sin).astype(BF16)
    kw_ref[...] = (mm(_C_KW, LANES) * cos + mm(_C_KWR, LANES) * sin).astype(BF16)
    cmp_ref[...] = mm(_C_CMP, 2 * NSA_KV)
    gqkv_ref[...] = mm(_C_GQKV, 3 * GDN_W)
    z_ref[...] = mm(_C_Z, GDN_W)
    sm_ref[...] = mm(_C_SM, LANES)


def _rot_cols(w):
    k, n = w.shape
    half = NSA_HEAD_DIM // 2
    w4 = w.reshape(k, n // NSA_HEAD_DIM, 2, half)
    return jnp.stack([-w4[:, :, 1], w4[:, :, 0]], axis=2).reshape(k, n)


def _rope_tables(pos, reps):
    half = NSA_HEAD_DIM // 2
    inv = jnp.power(ROPE_THETA, -jnp.arange(half, dtype=F32) / half)
    ang = pos.astype(F32)[:, None] * inv[None, :]
    cos = jnp.tile(jnp.cos(ang), (1, 2 * reps))
    sin = jnp.tile(jnp.sin(ang), (1, 2 * reps))
    return cos, sin


def _in_projection(x2d, w_in, S):
    T = x2d.shape[0]
    o = 0
    parts = {}
    for name, n in (("q", NSA_Q), ("kc", NSA_KV), ("vc", NSA_KV), ("ks", NSA_KV), ("vs", NSA_KV),
                    ("kw", NSA_KV), ("vw", NSA_KV), ("gl", 3 * NSA_HEADS), ("gq", GDN_W), ("gk", GDN_W),
                    ("gv", GDN_W), ("gz", GDN_W), ("gb", GDN_HEADS), ("ga", GDN_HEADS)):
        parts[name] = w_in[:, o:o + n]
        o += n
    pad = jnp.zeros((D_MODEL, LANES - 3 * NSA_HEADS - 2 * GDN_HEADS), F32)
    w_t = jnp.concatenate([parts["q"], _rot_cols(parts["q"]), parts["vs"], parts["vw"]], axis=1).T.astype(BF16)
    w_n = jnp.concatenate([
        parts["ks"], _rot_cols(parts["ks"]), parts["kw"], _rot_cols(parts["kw"]), parts["kc"], parts["vc"],
        parts["gq"], parts["gk"], parts["gv"], parts["gz"], parts["gl"], parts["gb"], parts["ga"], pad,
    ], axis=1).astype(BF16)
    assert w_t.shape[0] == _R_END and w_n.shape[1] == _C_END
    cos, sin = _rope_tables(jnp.arange(S), LANES // NSA_HEAD_DIM)
    cos_t, sin_t = cos[:, :NSA_HEAD_DIM].T, sin[:, :NSA_HEAD_DIM].T
    tm = min(IN_TM, S)
    nblk = S // tm
    row = lambda i: (i, 0)
    col3 = lambda i: (0, 0, i)
    const = lambda i: (0, 0)
    return pl.pallas_call(
        _inproj_kernel,
        grid=(T // tm,),
        in_specs=[
            pl.BlockSpec((tm, D_MODEL), row),
            pl.BlockSpec((_R_END, D_MODEL), const), pl.BlockSpec((D_MODEL, _C_END), const),
            pl.BlockSpec((tm, LANES), lambda i: (i % nblk, 0)),
            pl.BlockSpec((tm, LANES), lambda i: (i % nblk, 0)),
            pl.BlockSpec((NSA_HEAD_DIM, tm), lambda i: (0, i % nblk)),
            pl.BlockSpec((NSA_HEAD_DIM, tm), lambda i: (0, i % nblk)),
        ],
        out_specs=[
            pl.BlockSpec((NSA_HEADS, NSA_HEAD_DIM, tm), col3),
            pl.BlockSpec((NSA_KV_HEADS, 2 * NSA_HEAD_DIM, tm), col3),
            pl.BlockSpec((NSA_KV_HEADS, 2 * NSA_HEAD_DIM, tm), col3),
            pl.BlockSpec((tm, NSA_KV), row), pl.BlockSpec((tm, NSA_KV), row),
            pl.BlockSpec((tm, 2 * NSA_KV), row), pl.BlockSpec((tm, 3 * GDN_W), row),
            pl.BlockSpec((tm, GDN_W), row), pl.BlockSpec((tm, LANES), row),
        ],
        out_shape=[
            jax.ShapeDtypeStruct((NSA_HEADS, NSA_HEAD_DIM, T), BF16),
            jax.ShapeDtypeStruct((NSA_KV_HEADS, 2 * NSA_HEAD_DIM, T), BF16),
            jax.ShapeDtypeStruct((NSA_KV_HEADS, 2 * NSA_HEAD_DIM, T), BF16),
            jax.ShapeDtypeStruct((T, NSA_KV), BF16), jax.ShapeDtypeStruct((T, NSA_KV), BF16),
            jax.ShapeDtypeStruct((T, 2 * NSA_KV), F32), jax.ShapeDtypeStruct((T, 3 * GDN_W), F32),
            jax.ShapeDtypeStruct((T, GDN_W), F32), jax.ShapeDtypeStruct((T, LANES), F32),
        ],
        compiler_params=_cparams("parallel"),
        name="in_projection",
    )(x2d, w_t, w_n, cos, sin, cos_t, sin_t)


def _compress_kernel(k_ref, v_ref, wk_ref, wv_ref, pe_ref, cos_ref, sin_ref, kc_ref, vc_ref):
    nb = k_ref.shape[2]
    k = k_ref[0, 0]
    v = v_ref[0, 0]
    klo = (k + pe_ref[0:1, :]).astype(BF16)
    khi = (k + pe_ref[1:2, :]).astype(BF16)
    vlo = (v + pe_ref[2:3, :]).astype(BF16)
    vhi = (v + pe_ref[3:4, :]).astype(BF16)

    def up(m):
        return pltpu.roll(m, nb - 1, axis=0)

    a = _dot(klo, wk_ref[0]) + up(_dot(khi, wk_ref[1]))
    ar = _dot(klo, wk_ref[2]) + up(_dot(khi, wk_ref[3]))
    kc_ref[0] = a * cos_ref[...] + ar * sin_ref[...]
    vc_ref[0] = _dot(vlo, wv_ref[0]) + up(_dot(vhi, wv_ref[1]))


def _compress(cmp2d, pe_k, w_k, pe_v, w_v, B, S):
    nb = S // CMP_STRIDE
    bh = B * NSA_KV_HEADS
    row_w = CMP_STRIDE * NSA_HEAD_DIM
    c = cmp2d.reshape(B, S, 2, NSA_KV_HEADS, NSA_HEAD_DIM).transpose(2, 0, 3, 1, 4)
    c = c.reshape(2, bh, nb, row_w)
    wk = w_k.reshape(2, row_w, NSA_HEAD_DIM)
    wk_all = jnp.concatenate([wk, jnp.stack([_rot_cols(wk[0]), _rot_cols(wk[1])])], 0).astype(BF16)
    wv_all = w_v.reshape(2, row_w, NSA_HEAD_DIM).astype(BF16)
    pe = jnp.concatenate([pe_k.reshape(2, row_w), pe_v.reshape(2, row_w)], 0)
    cos, sin = _rope_tables(jnp.arange(nb) * CMP_STRIDE + (CMP_BLOCK - 1) / 2, 1)
    full = lambda shape: pl.BlockSpec(shape, lambda i: (0,) * len(shape))
    kc, vc = pl.pallas_call(
        _compress_kernel,
        grid=(bh,),
        in_specs=[
            pl.BlockSpec((1, 1, nb, row_w), lambda i: (0, i, 0, 0)),
            pl.BlockSpec((1, 1, nb, row_w), lambda i: (1, i, 0, 0)),
            full((4, row_w, NSA_HEAD_DIM)), full((2, row_w, NSA_HEAD_DIM)), full((4, row_w)),
            full((nb, NSA_HEAD_DIM)), full((nb, NSA_HEAD_DIM)),
        ],
        out_specs=[pl.BlockSpec((1, nb, NSA_HEAD_DIM), lambda i: (i, 0, 0))] * 2,
        out_shape=[jax.ShapeDtypeStruct((bh, nb, NSA_HEAD_DIM), F32)] * 2,
        compiler_params=_cparams("parallel"),
        name="nsa_compress",
    )(c, c, wk_all, wv_all, pe, cos, sin)
    return kc, vc


SEL_KT = 1024
SEL_SUB = 512


def _nsa_kernel(qt_ref, kc_ref, vct_ref, ovt_ref, ks_ref, vst_ref, kw_ref, vwt_ref, gate_ref, o_ref, sel_scr,
                *, n_sel, kt_w, win_w):
    G, QB, D = NSA_GROUP, Q_BLOCK, NSA_HEAD_DIM
    GQ = G * QB
    nb = kc_ref.shape[2]
    ns = ovt_ref.shape[0]
    h = pl.program_id(1)
    qb = pl.program_id(2)
    s0 = qb * QB
    q4 = jnp.concatenate([qt_ref[g] for g in range(G)], axis=1)
    zeros = jnp.zeros_like(q4)
    qpad = jnp.concatenate([jnp.where(h == hh, q4, zeros) for hh in range(NSA_KV_HEADS)], axis=0)
    tq = s0 + lax.broadcasted_iota(jnp.int32, (1, QB), 1)
    lanes4 = lambda a: jnp.concatenate([a] * G, axis=1)

    s_c = _dot(kc_ref[0, 0], q4)
    nrow = lax.broadcasted_iota(jnp.int32, (nb, QB), 0)
    mask_c = lanes4((nrow * CMP_STRIDE + (CMP_BLOCK - 1)) <= tq)
    s_c = jnp.where(mask_c, s_c, NEG)
    m_c = jnp.max(s_c, 0, keepdims=True)
    e_c = jnp.where(mask_c, jnp.exp2(s_c - m_c), 0.0)
    l_c = jnp.sum(e_c, 0, keepdims=True)
    p_c = e_c / jnp.where(l_c > 0, l_c, 1.0)
    o_c = _dot(vct_ref[0, 0], p_c.astype(BF16))

    psum = p_c[:, 0:QB]
    for g in range(1, G):
        psum = psum + p_c[:, g * QB:(g + 1) * QB]
    p_hi = psum.astype(BF16)
    p_lo = (psum - p_hi.astype(F32)).astype(BF16)
    imp_t = _dot(ovt_ref[...], p_hi) + _dot(ovt_ref[...], p_lo)

    w0 = pl.multiple_of(jnp.maximum(s0 + QB - win_w, 0), QB)
    s_w = _dot(kw_ref[0, pl.ds(w0, win_w), :], qpad)
    rel = tq - (w0 + lax.broadcasted_iota(jnp.int32, (win_w, QB), 0))
    s_w = s_w + lanes4(jnp.where((rel >= 0) & (rel < WINDOW), 0.0, NEG))
    p_w = jnp.exp2((s_w - jnp.max(s_w, 0, keepdims=True)).astype(BF16))
    acc_w = _dot(vwt_ref[0, :, pl.ds(w0, win_w)], p_w)
    o_w = acc_w[:D] / acc_w[D:D + 1]

    jrow = lax.broadcasted_iota(jnp.int32, (ns, QB), 0)
    tcol = s0 + lax.broadcasted_iota(jnp.int32, (ns, QB), 1)
    cur = tcol // SLC_BLOCK
    forced = (jrow == 0) | ((jrow <= cur) & (jrow > cur - SLC_LOCAL))
    score = jnp.where((jrow * SLC_BLOCK) <= tcol, jnp.where(forced, FORCED_SCORE, imp_t), -1.0)
    jf = jrow.astype(F32)
    sel_t = jnp.zeros((ns, QB), F32)
    for _ in range(n_sel):
        mx = jnp.max(score, 0, keepdims=True)
        first = jnp.min(jnp.where(score == mx, jf, float(ns)), 0, keepdims=True)
        pick = jf == first
        sel_t = jnp.where(pick, 1.0, sel_t)
        score = jnp.where(pick, -2.0, score)
    sel_scr[...] = sel_t

    sub_w = min(SEL_SUB, kt_w)
    subs = kt_w // sub_w
    krow = lax.broadcasted_iota(jnp.int32, (sub_w, QB), 0)
    blocks_per_step = kt_w // SLC_BLOCK
    blocks_per_sub = sub_w // SLC_BLOCK

    n_kt = (s0 + QB + kt_w - 1) // kt_w

    def sel_step(kt, carry):
        k0s = [pl.multiple_of(kt * kt_w + c * sub_w, sub_w) for c in range(subs)]
        scores = [_dot(ks_ref[0, pl.ds(k0, sub_w), :], qpad) for k0 in k0s]
        probs = []
        for c, (m, _) in enumerate(carry):
            chosen = jnp.concatenate(
                [jnp.broadcast_to(sel_scr[pl.ds(kt * blocks_per_step + c * blocks_per_sub + j, 1), :],
                                  (SLC_BLOCK, QB)) for j in range(blocks_per_sub)], axis=0)
            bias = jnp.where((chosen > 0.5) & ((k0s[c] + krow) <= tq), 0.0, NEG)
            s = scores[c] + lanes4(bias)
            m_new = jnp.maximum(m, jnp.max(s, 0, keepdims=True))
            probs.append((m_new, jnp.exp2(m - m_new), jnp.exp2((s - m_new).astype(BF16))))
        return tuple((m_new, alpha * acc + _dot(vst_ref[0, :, pl.ds(k0, sub_w)], p))
                     for k0, (m_new, alpha, p), (_, acc) in zip(k0s, probs, carry))

    chains = lax.fori_loop(
        0, n_kt, sel_step,
        tuple((jnp.full((1, GQ), NEG, F32), jnp.zeros((2 * D, GQ), F32)) for _ in range(subs)))
    m_s = chains[0][0]
    for m, _ in chains[1:]:
        m_s = jnp.maximum(m_s, m)
    acc_s = sum(jnp.exp2(m - m_s) * acc for m, acc in chains)
    o_s = acc_s[:D] / acc_s[D:D + 1]

    gate = _sigmoid(gate_ref[0, 0])
    for g in range(G):
        cols = slice(g * QB, (g + 1) * QB)
        o_ref[g] = (gate[3 * g:3 * g + 1] * o_c[:, cols] + gate[3 * g + 1:3 * g + 2] * o_s[:, cols]
                    + gate[3 * g + 2:3 * g + 3] * o_w[:, cols]).astype(BF16)


def _nsa_attention(qt, ks, vst, kw, vwt, kc, vc, small, B, S):
    H, HKV, G, D = NSA_HEADS, NSA_KV_HEADS, NSA_GROUP, NSA_HEAD_DIM
    nb = S // CMP_STRIDE
    ns = S // SLC_BLOCK
    nq = S // Q_BLOCK
    n_sel = min(SLC_TOP_N, ns)
    kt_w = min(SEL_KT, S)
    win_w = min(WINDOW + Q_BLOCK, S)
    kcb = kc.reshape(B, HKV, nb, D).astype(BF16)
    vct = vc.reshape(B, HKV, nb, D).transpose(0, 1, 3, 2).astype(BF16)
    gates = small[:, :3 * H].reshape(B, S, HKV, 3 * G).transpose(0, 2, 3, 1)
    m_i = jnp.arange(ns)[:, None]
    n_i = jnp.arange(nb)[None, :]
    overlap_t = ((n_i * CMP_STRIDE <= m_i * SLC_BLOCK + SLC_BLOCK - 1)
                 & (n_i * CMP_STRIDE + CMP_BLOCK - 1 >= m_i * SLC_BLOCK)).astype(BF16)
    per_bh = lambda shape: pl.BlockSpec((1, 1) + shape, lambda b, h, i: (b, h, 0, 0))
    keys = pl.BlockSpec((1, S, NSA_KV), lambda b, h, i: (b, 0, 0))
    vals = pl.BlockSpec((1, 2 * D, S), lambda b, h, i: (h, 0, b))
    heads = pl.BlockSpec((G, D, Q_BLOCK), lambda b, h, i: (h, 0, b * nq + i))
    return pl.pallas_call(
        functools.partial(_nsa_kernel, n_sel=n_sel, kt_w=kt_w, win_w=win_w),
        grid=(B, HKV, nq),
        in_specs=[
            heads, per_bh((nb, D)), per_bh((D, nb)),
            pl.BlockSpec((ns, nb), lambda b, h, i: (0, 0)),
            keys, vals, keys, vals,
            pl.BlockSpec((1, 1, 3 * G, Q_BLOCK), lambda b, h, i: (b, h, 0, i)),
        ],
        out_specs=heads,
        out_shape=jax.ShapeDtypeStruct((H, D, B * S), BF16),
        scratch_shapes=[pltpu.VMEM((ns, Q_BLOCK), F32)],
        compiler_params=_cparams("parallel", "parallel", "arbitrary"),
        name="nsa_attention",
    )(qt, kcb, vct, overlap_t, ks.reshape(B, S, NSA_KV), vst, kw.reshape(B, S, NSA_KV), vwt, gates)


GDN_TB = 256
GDN_HALO = 8


def _gdn_prep_kernel(qkv_ref, halo_ref, sm_ref, cw_ref, alog_ref, dtb_ref,
                     u_ref, wq_ref, kd_ref, intra_ref, dec_ref, xs_scr, y_scr, bg_scr):
    C, HD, NH, BD = GDN_CHUNK, GDN_HEAD_DIM, GDN_HEADS, GDN_BD
    tb = qkv_ref.shape[1]
    first = pl.program_id(1) == 0
    xs_scr[GDN_HALO:GDN_HALO + tb, :] = qkv_ref[0]
    xs_scr[0:GDN_HALO, :] = jnp.where(first, 0.0, halo_ref[0])
    for cb in range(3 * GDN_W // LANES):
        cols = slice(cb * LANES, (cb + 1) * LANES)
        y = cw_ref[GDN_CONV - 1:GDN_CONV, cols] * xs_scr[GDN_HALO:GDN_HALO + tb, cols]
        for d in range(1, GDN_CONV):
            y = y + cw_ref[GDN_CONV - 1 - d:GDN_CONV - d, cols] * xs_scr[GDN_HALO - d:GDN_HALO - d + tb, cols]
        y_scr[:, cols] = _silu(y)
    sm = sm_ref[0]
    b_logit = sm[:, 3 * NSA_HEADS:3 * NSA_HEADS + NH]
    a_logit = sm[:, 3 * NSA_HEADS + NH:3 * NSA_HEADS + 2 * NH] + dtb_ref[...]
    softplus = jnp.maximum(a_logit, 0.0) + jnp.log(1.0 + jnp.exp(-jnp.abs(a_logit)))
    bg_scr[:, 0:NH] = _sigmoid(b_logit)
    bg_scr[:, NH:2 * NH] = -jnp.exp(alog_ref[...]) * softplus

    r = lax.broadcasted_iota(jnp.int32, (BD, BD), 0)
    c = lax.broadcasted_iota(jnp.int32, (BD, BD), 1)
    same = (r // C) == (c // C)
    eye = r == c
    tril = same & (r >= c)
    triu = same & (r <= c)
    strict = same & (r > c)
    eye_f = jnp.where(eye, 1.0, 0.0)

    def stack_heads(m, c0):
        return jnp.concatenate([m[:, c0 + h * HD:c0 + (h + 1) * HD] for h in range(NH)], axis=0)

    def chunk(ci, carry):
        r0 = pl.multiple_of(ci * C, C)
        yc = y_scr[pl.ds(r0, C), :]
        bg = bg_scr[pl.ds(r0, C), :]
        q4 = stack_heads(yc, 0)
        k4 = stack_heads(yc, GDN_W)
        v4 = stack_heads(yc, 2 * GDN_W)
        q4 = q4 * lax.rsqrt(jnp.sum(q4 * q4, -1, keepdims=True) + NORM_EPS) * (HD ** -0.5)
        k4 = k4 * lax.rsqrt(jnp.sum(k4 * k4, -1, keepdims=True) + NORM_EPS)
        b4 = jnp.concatenate([bg[:, h:h + 1] for h in range(NH)], axis=0)
        g4 = jnp.concatenate([bg[:, NH + h:NH + h + 1] for h in range(NH)], axis=0)
        g_row = jnp.sum(jnp.where(eye, g4, 0.0), axis=0, keepdims=True)
        gc_col = jnp.sum(jnp.where(tril, g_row, 0.0), axis=1, keepdims=True)
        gc_row = jnp.sum(jnp.where(triu, g4, 0.0), axis=0, keepdims=True)
        gl_col = jnp.sum(jnp.where(same, g_row, 0.0), axis=1, keepdims=True)
        decay = jnp.where(tril, jnp.exp(jnp.minimum(gc_col - gc_row, 0.0)), 0.0)
        kb4 = k4 * b4
        k4b = k4.astype(BF16)
        lmat = jnp.where(strict, _dot_nt(kb4.astype(BF16), k4b) * decay, 0.0)
        intra = jnp.where(tril, _dot_nt(q4.astype(BF16), k4b) * decay, 0.0)
        inv = eye_f - lmat
        pw = lmat
        for _ in range(int(math.log2(C)) - 1):
            pw = _dot2(pw, pw)
            inv = _dot2(inv, eye_f + pw)
        egc = jnp.exp(gc_col)
        uw = _dot2(inv, jnp.concatenate([v4 * b4, kb4 * egc], axis=1))
        u_ref[0, ci] = uw[:, :HD]
        w4 = uw[:, HD:].astype(BF16).reshape(NH, C, HD)
        qg4 = (q4 * egc).astype(BF16).reshape(NH, C, HD)
        wq_ref[0, ci] = jnp.concatenate([w4, qg4], axis=1)
        kd_ref[0, ci] = (k4 * jnp.exp(gl_col - gc_col)).astype(BF16)
        intra_ref[0, ci] = intra.astype(BF16)
        dec = jnp.broadcast_to(jnp.exp(gl_col), (BD, LANES)).reshape(NH, C, LANES)
        dec_ref[0, ci] = dec[:, 0:8, :]
        return carry

    def chunk_pair(pi, carry):
        chunk(2 * pi, carry)
        chunk(2 * pi + 1, carry)
        return carry

    lax.fori_loop(0, tb // (2 * C), chunk_pair, 0)


def _gdn_scan_kernel(u_ref, wq_ref, kd_ref, intra_ref, dec_ref, z_ref, nw_ref, o_ref, s_scr):
    C, HD, NH = GDN_CHUNK, GDN_HEAD_DIM, GDN_HEADS
    nbatch = u_ref.shape[0]

    @pl.when(pl.program_id(0) == 0)
    def _():
        s_scr[...] = jnp.zeros_like(s_scr)

    nw = nw_ref[...]
    for b in range(nbatch):
        vnew, qs = [], []
        for h in range(NH):
            st = s_scr[b * NH + h]
            res = _dot(wq_ref[b, 0, h], st.astype(BF16))
            vnew.append(u_ref[b, 0, h * C:(h + 1) * C, :] - res[:C])
            qs.append(res[C:])
        vnew4 = jnp.concatenate(vnew, axis=0).astype(BF16)
        o4 = jnp.concatenate(qs, axis=0) + _dot(intra_ref[b, 0], vnew4)
        for h in range(NH):
            rows = slice(h * C, (h + 1) * C)
            st = s_scr[b * NH + h]
            s_scr[b * NH + h] = st * dec_ref[b, 0, h, 0:1, :] + _dot_tn(kd_ref[b, 0, rows, :], vnew4[rows])
            oh = o4[rows]
            oh = oh * lax.rsqrt(jnp.mean(oh * oh, -1, keepdims=True) + NORM_EPS) * nw
            o_ref[b, :, h * HD:(h + 1) * HD] = oh * _silu(z_ref[b, :, h * HD:(h + 1) * HD])


def _gdn_mixer(gqkv, z, small, conv_w, a_log, dt_bias, norm_w, B, S):
    C, HD, NH, BD = GDN_CHUNK, GDN_HEAD_DIM, GDN_HEADS, GDN_BD
    nc = S // C
    tb = min(GDN_TB, S)
    cb = tb // C
    qkv3 = gqkv.reshape(B, S, 3 * GDN_W)
    sm3 = small.reshape(B, S, LANES)
    hb = tb // GDN_HALO
    u, wq, kd, intra, dec = pl.pallas_call(
        _gdn_prep_kernel,
        grid=(B, S // tb),
        in_specs=[
            pl.BlockSpec((1, tb, 3 * GDN_W), lambda b, i: (b, i, 0)),
            pl.BlockSpec((1, GDN_HALO, 3 * GDN_W), lambda b, i: (b, jnp.maximum(i * hb - 1, 0), 0)),
            pl.BlockSpec((1, tb, LANES), lambda b, i: (b, i, 0)),
            pl.BlockSpec((GDN_CONV, 3 * GDN_W), lambda b, i: (0, 0)),
            pl.BlockSpec((1, NH), lambda b, i: (0, 0)),
            pl.BlockSpec((1, NH), lambda b, i: (0, 0)),
        ],
        out_specs=[
            pl.BlockSpec((1, cb, BD, HD), lambda b, i: (b, i, 0, 0)),
            pl.BlockSpec((1, cb, NH, 2 * C, HD), lambda b, i: (b, i, 0, 0, 0)),
            pl.BlockSpec((1, cb, BD, HD), lambda b, i: (b, i, 0, 0)),
            pl.BlockSpec((1, cb, BD, BD), lambda b, i: (b, i, 0, 0)),
            pl.BlockSpec((1, cb, NH, 8, LANES), lambda b, i: (b, i, 0, 0, 0)),
        ],
        out_shape=[
            jax.ShapeDtypeStruct((B, nc, BD, HD), F32),
            jax.ShapeDtypeStruct((B, nc, NH, 2 * C, HD), BF16),
            jax.ShapeDtypeStruct((B, nc, BD, HD), BF16),
            jax.ShapeDtypeStruct((B, nc, BD, BD), BF16),
            jax.ShapeDtypeStruct((B, nc, NH, 8, LANES), F32),
        ],
        scratch_shapes=[
            pltpu.VMEM((GDN_HALO + tb, 3 * GDN_W), F32),
            pltpu.VMEM((tb, 3 * GDN_W), F32),
            pltpu.VMEM((tb, 2 * NH), F32),
        ],
        compiler_params=_cparams("parallel", "parallel"),
        name="gdn_prepare",
    )(qkv3, qkv3, sm3, conv_w, a_log.reshape(1, NH), dt_bias.reshape(1, NH))

    o = pl.pallas_call(
        _gdn_scan_kernel,
        grid=(nc,),
        in_specs=[
            pl.BlockSpec((B, 1, BD, HD), lambda c: (0, c, 0, 0)),
            pl.BlockSpec((B, 1, NH, 2 * C, HD), lambda c: (0, c, 0, 0, 0)),
            pl.BlockSpec((B, 1, BD, HD), lambda c: (0, c, 0, 0)),
            pl.BlockSpec((B, 1, BD, BD), lambda c: (0, c, 0, 0)),
            pl.BlockSpec((B, 1, NH, 8, LANES), lambda c: (0, c, 0, 0, 0)),
            pl.BlockSpec((B, C, GDN_W), lambda c: (0, c, 0)),
            pl.BlockSpec((1, HD), lambda c: (0, 0)),
        ],
        out_specs=pl.BlockSpec((B, C, GDN_W), lambda c: (0, c, 0)),
        out_shape=jax.ShapeDtypeStruct((B, S, GDN_W), F32),
        scratch_shapes=[pltpu.VMEM((B * NH, HD, HD), F32)],
        compiler_params=_cparams("arbitrary"),
        name="gdn_scan",
    )(u, wq, kd, intra, dec, z.reshape(B, S, GDN_W), norm_w.reshape(1, HD))
    return o.reshape(B * S, GDN_W)


MIX_TM = 512


def _mix_ln_kernel(x_ref, at_ref, b_ref, w_ref, g_ref, be_ref, o_ref):
    a_t = at_ref[...].reshape(NSA_Q, x_ref.shape[0])
    mix = (_dot_tn(a_t, w_ref[0:NSA_Q, :])
           + _dot(b_ref[...].astype(BF16), w_ref[NSA_Q:NSA_Q + GDN_W, :]))
    o_ref[...] = _layer_norm(DN_ALPHA * x_ref[...] + mix, g_ref[...], be_ref[...])


def _mix_out_ln(x2d, o_at, o_b, w_out, g, b):
    T = x2d.shape[0]
    row = lambda i: (i, 0)
    const = lambda i: (0, 0)
    return pl.pallas_call(
        _mix_ln_kernel,
        grid=(T // MIX_TM,),
        in_specs=[
            pl.BlockSpec((MIX_TM, D_MODEL), row),
            pl.BlockSpec((NSA_HEADS, NSA_HEAD_DIM, MIX_TM), lambda i: (0, 0, i)),
            pl.BlockSpec((MIX_TM, GDN_W), row), pl.BlockSpec((NSA_Q + GDN_W, D_MODEL), const),
            pl.BlockSpec((1, D_MODEL), const), pl.BlockSpec((1, D_MODEL), const),
        ],
        out_specs=pl.BlockSpec((MIX_TM, D_MODEL), row),
        out_shape=jax.ShapeDtypeStruct((T, D_MODEL), F32),
        compiler_params=_cparams("parallel"),
        name="mix_out_ln",
    )(x2d, o_at, o_b, w_out.astype(BF16), g.reshape(1, D_MODEL), b.reshape(1, D_MODEL))


FFN_TM = 1024
FFN_TF = 256


def _ffn_ln_kernel(x_ref, wg_ref, wu_ref, wd_ref, g_ref, be_ref, o_ref, xb_scr, acc_scr):
    j = pl.program_id(1)

    @pl.when(j == 0)
    def _():
        xb_scr[...] = x_ref[...].astype(BF16)
        acc_scr[...] = jnp.zeros_like(acc_scr)

    xb = xb_scr[...]
    hid = _silu(_dot(xb, wg_ref[...])) * _dot(xb, wu_ref[...])
    acc_scr[...] += _dot(hid.astype(BF16), wd_ref[...])

    @pl.when(j == pl.num_programs(1) - 1)
    def _():
        o_ref[...] = _layer_norm(DN_ALPHA * x_ref[...] + acc_scr[...], g_ref[...], be_ref[...])


def _ffn_ln(x2d, wg, wu, wd, g, b):
    T = x2d.shape[0]
    dff = wg.shape[1]
    tm = min(FFN_TM, T)
    return pl.pallas_call(
        _ffn_ln_kernel,
        grid=(T // tm, dff // FFN_TF),
        in_specs=[
            pl.BlockSpec((tm, D_MODEL), lambda i, j: (i, 0)),
            pl.BlockSpec((D_MODEL, FFN_TF), lambda i, j: (0, j)),
            pl.BlockSpec((D_MODEL, FFN_TF), lambda i, j: (0, j)),
            pl.BlockSpec((FFN_TF, D_MODEL), lambda i, j: (j, 0)),
            pl.BlockSpec((1, D_MODEL), lambda i, j: (0, 0)),
            pl.BlockSpec((1, D_MODEL), lambda i, j: (0, 0)),
        ],
        out_specs=pl.BlockSpec((tm, D_MODEL), lambda i, j: (i, 0)),
        out_shape=jax.ShapeDtypeStruct((T, D_MODEL), F32),
        scratch_shapes=[pltpu.VMEM((tm, D_MODEL), BF16), pltpu.VMEM((tm, D_MODEL), F32)],
        compiler_params=_cparams("parallel", "arbitrary"),
        name="ffn_ln",
    )(x2d, wg.astype(BF16), wu.astype(BF16), wd.astype(BF16), g.reshape(1, D_MODEL), b.reshape(1, D_MODEL))


POOL_TM = 512
POOL_HALO = 16


def _route_top2(x1, rw):
    lane = lax.broadcasted_iota(jnp.int32, (x1.shape[0], LANES), 1)
    lanef = lane.astype(F32)
    logits = jnp.where(lane < N_EXPERTS, _dot3(x1, rw), NEG)
    ex = jnp.exp(logits - jnp.max(logits, -1, keepdims=True))
    probs = jnp.where(lane < N_EXPERTS, ex / jnp.sum(ex, -1, keepdims=True), -1.0)
    p1 = jnp.max(probs, -1, keepdims=True)
    i1 = jnp.min(jnp.where(probs == p1, lanef, float(LANES)), -1, keepdims=True)
    rest = jnp.where(lanef == i1, -1.0, probs)
    p2 = jnp.max(rest, -1, keepdims=True)
    i2 = jnp.min(jnp.where(rest == p2, lanef, float(LANES)), -1, keepdims=True)
    den = p1 + p2
    return jnp.where(lane == 0, i1, jnp.where(lane == 1, i2, jnp.where(
        lane == 2, p1 / den, jnp.where(lane == 3, p2 / den, 0.0))))


def _pool_ln_kernel(x_ref, halo_ref, pw_ref, ps_ref, g_ref, be_ref, rw_ref, o_ref, route_ref, xs_scr):
    tm = x_ref.shape[1]
    i = pl.program_id(1)
    x = x_ref[0]
    xs_scr[POOL_HALO:POOL_HALO + tm, :] = x
    xs_scr[0:POOL_HALO, :] = jnp.where(i == 0, 0.0, halo_ref[0])
    t1 = (i * tm + 1 + lax.broadcasted_iota(jnp.int32, (tm, 1), 0)).astype(F32)
    for gi, win in enumerate(POOL_SIZES):
        cols = slice(gi * POOL_GROUP, (gi + 1) * POOL_GROUP)
        acc = x[:, cols]
        for d in range(1, win):
            acc = acc + xs_scr[POOL_HALO - d:POOL_HALO - d + tm, cols]
        mean = acc / jnp.minimum(t1, float(win))
        mix = _dot((mean - x[:, cols]).astype(BF16), pw_ref[gi]) * ps_ref[:, cols]
        xs_scr[POOL_HALO:POOL_HALO + tm, cols] = DN_ALPHA * x[:, cols] + mix
    x1 = _layer_norm(xs_scr[POOL_HALO:POOL_HALO + tm, :], g_ref[...], be_ref[...])
    o_ref[0] = x1
    route_ref[0] = _route_top2(x1, rw_ref[...])


def _pool_ln(x3d, pool_w, pool_scale, g, b, router_w):
    B, S, _ = x3d.shape
    tm = min(POOL_TM, S)
    hb = tm // POOL_HALO
    const = lambda bb, i: (0, 0)
    rw = jnp.concatenate([router_w, jnp.zeros((D_MODEL, LANES - N_EXPERTS), F32)], axis=1)
    return pl.pallas_call(
        _pool_ln_kernel,
        grid=(B, S // tm),
        in_specs=[
            pl.BlockSpec((1, tm, D_MODEL), lambda bb, i: (bb, i, 0)),
            pl.BlockSpec((1, POOL_HALO, D_MODEL), lambda bb, i: (bb, jnp.maximum(i * hb - 1, 0), 0)),
            pl.BlockSpec((len(POOL_SIZES), POOL_GROUP, POOL_GROUP), lambda bb, i: (0, 0, 0)),
            pl.BlockSpec((1, D_MODEL), const), pl.BlockSpec((1, D_MODEL), const),
            pl.BlockSpec((1, D_MODEL), const), pl.BlockSpec((D_MODEL, LANES), const),
        ],
        out_specs=[pl.BlockSpec((1, tm, D_MODEL), lambda bb, i: (bb, i, 0)),
                   pl.BlockSpec((1, tm, LANES), lambda bb, i: (bb, i, 0))],
        out_shape=[jax.ShapeDtypeStruct((B, S, D_MODEL), F32), jax.ShapeDtypeStruct((B, S, LANES), F32)],
        scratch_shapes=[pltpu.VMEM((POOL_HALO + tm, D_MODEL), F32)],
        compiler_params=_cparams("parallel", "parallel"),
        name="pool_ln",
    )(x3d, x3d, pool_w.astype(BF16), pool_scale.reshape(1, D_MODEL), g.reshape(1, D_MODEL),
      b.reshape(1, D_MODEL), rw)


MOE_TM = 1024
MOE_TF = 512
MOE_TC = 512


def _moe_plan(route, T, tm):
    ne = N_EXPERTS
    e_flat = jnp.concatenate([route[:, 0], route[:, 1]]).astype(jnp.int32)
    onehot = (e_flat[:, None] == jnp.arange(ne)[None, :]).astype(jnp.int32)
    csum = jnp.cumsum(onehot, axis=0)
    rank = jnp.sum(onehot * csum, axis=1) - 1
    counts = csum[-1]
    padded = (counts + tm - 1) // tm * tm
    gend = jnp.cumsum(padded)
    gstart = gend - padded
    pos = gstart[e_flat] + rank
    n_tiles = (2 * T) // tm + ne
    tile_e = jnp.minimum(jnp.searchsorted(gend, jnp.arange(n_tiles) * tm, side="right"), ne - 1)
    tile_e = tile_e.astype(jnp.int32)
    n_used = (gend[-1] // tm).astype(jnp.int32).reshape(1)
    order = jnp.argsort(e_flat, stable=True)
    cstart = jnp.cumsum(counts) - counts
    e_p = jnp.repeat(tile_e, tm)
    r = jnp.clip(jnp.arange(n_tiles * tm) - gstart[e_p], 0, jnp.maximum(counts[e_p] - 1, 0))
    src = (order[jnp.clip(cstart[e_p] + r, 0, 2 * T - 1)] % T).astype(jnp.int32)
    return pos.astype(jnp.int32), src, tile_e, n_used, n_tiles


GATHER_UNROLL = 8


def _gather_rows(src_hbm, idx, dst, sem):
    n = dst.shape[0]

    def issue(blk, c):
        for u in range(GATHER_UNROLL):
            r = blk * GATHER_UNROLL + u
            pltpu.make_async_copy(src_hbm.at[pl.ds(idx(r), 1)], dst.at[pl.ds(r, 1)], sem).start()
        return c

    lax.fori_loop(0, n // GATHER_UNROLL, issue, 0)


def _gather_wait(src_hbm, dst, sem):
    pltpu.make_async_copy(src_hbm.at[pl.ds(0, dst.shape[0])], dst, sem).wait()


def _moe_expert_kernel(te_ref, nu_ref, src_ref, nxt_ref, x_hbm, wg_ref, wu_ref, wd_ref, o_ref,
                       xbuf, xb_scr, acc_scr, sem):
    i = pl.program_id(0)
    j = pl.program_id(1)
    n_used = nu_ref[0]
    used = i < n_used
    slot = i % 2

    @pl.when((i == 0) & (j == 0))
    def _():
        _gather_rows(x_hbm, lambda r: src_ref[0, 0, r], xbuf.at[0], sem.at[0])

    @pl.when((i + 1 < n_used) & (j == 0))
    def _():
        _gather_rows(x_hbm, lambda r: nxt_ref[0, 0, r], xbuf.at[1 - slot], sem.at[1 - slot])

    @pl.when(used & (j == 0))
    def _():
        _gather_wait(x_hbm, xbuf.at[slot], sem.at[slot])
        xb_scr[...] = xbuf[slot].astype(BF16)
        acc_scr[...] = jnp.zeros_like(acc_scr)

    @pl.when(used)
    def _():
        xb = xb_scr[...]
        hid = _silu(_dot(xb, wg_ref[0, 0])) * _dot(xb, wu_ref[0, 0])
        acc_scr[...] += _dot(hid.astype(BF16), wd_ref[0])

    last = j == pl.num_programs(1) - 1

    @pl.when(used & last)
    def _():
        o_ref[...] = acc_scr[...]

    @pl.when(jnp.logical_not(used) & last)
    def _():
        o_ref[...] = jnp.zeros_like(o_ref)


def _moe_combine_kernel(pos_ref, nxt_ref, x_ref, route_ref, ys_hbm, g_ref, be_ref, o_ref, ybuf, sem):
    i = pl.program_id(0)
    slot = i % 2

    def fetch(p_ref, s):
        for k in range(2):
            _gather_rows(ys_hbm, lambda r: p_ref[0, k, r], ybuf.at[s, k], sem.at[s])

    @pl.when(i == 0)
    def _():
        fetch(pos_ref, 0)

    @pl.when(i + 1 < pl.num_programs(0))
    def _():
        fetch(nxt_ref, 1 - slot)

    for k in range(2):
        _gather_wait(ys_hbm, ybuf.at[slot, k], sem.at[slot])
    route = route_ref[...]
    y = DN_ALPHA * x_ref[...] + route[:, 2:3] * ybuf[slot, 0] + route[:, 3:4] * ybuf[slot, 1]
    o_ref[...] = _layer_norm(y, g_ref[...], be_ref[...])


def _moe_ln(x2d, route, wg, wu, wd, g, b):
    T = x2d.shape[0]
    ne, _, dff = wg.shape
    tm = min(MOE_TM, T)
    tc = min(MOE_TC, T)
    pos, src, tile_e, n_used, n_tiles = _moe_plan(route, T, tm)

    nj = dff // MOE_TF

    def w_col(i, j, te, nu):
        return (te[i], jnp.where(i < nu[0], j, 0), 0, 0)

    def w_row(i, j, te, nu):
        return (te[i], jnp.where(i < nu[0], j, 0), 0)

    col_blocks = lambda w: w.reshape(ne, D_MODEL, nj, MOE_TF).transpose(0, 2, 1, 3).astype(BF16)
    src3 = src.reshape(n_tiles, 1, tm)
    ys = pl.pallas_call(
        _moe_expert_kernel,
        grid_spec=pltpu.PrefetchScalarGridSpec(
            num_scalar_prefetch=2,
            grid=(n_tiles, nj),
            in_specs=[
                pl.BlockSpec((1, 1, tm), lambda i, j, te, nu: (i, 0, 0), memory_space=pltpu.SMEM),
                pl.BlockSpec((1, 1, tm), lambda i, j, te, nu: (jnp.minimum(i + 1, n_tiles - 1), 0, 0),
                             memory_space=pltpu.SMEM),
                pl.BlockSpec(memory_space=pl.ANY),
                pl.BlockSpec((1, 1, D_MODEL, MOE_TF), w_col),
                pl.BlockSpec((1, 1, D_MODEL, MOE_TF), w_col),
                pl.BlockSpec((1, MOE_TF, D_MODEL), w_row),
            ],
            out_specs=pl.BlockSpec((tm, D_MODEL), lambda i, j, te, nu: (i, 0)),
            scratch_shapes=[pltpu.VMEM((2, tm, D_MODEL), F32), pltpu.VMEM((tm, D_MODEL), BF16),
                            pltpu.VMEM((tm, D_MODEL), F32), pltpu.SemaphoreType.DMA((2,))],
        ),
        out_shape=jax.ShapeDtypeStruct((n_tiles * tm, D_MODEL), F32),
        compiler_params=_cparams("arbitrary", "arbitrary"),
        name="moe_experts",
    )(tile_e, n_used, src3, src3, x2d, col_blocks(wg), col_blocks(wu), wd.astype(BF16))

    n_steps = T // tc
    pos3 = pos.reshape(2, n_steps, tc).transpose(1, 0, 2)
    row = lambda i: (i, 0)
    const = lambda i: (0, 0)
    return pl.pallas_call(
        _moe_combine_kernel,
        grid=(n_steps,),
        in_specs=[
            pl.BlockSpec((1, 2, tc), lambda i: (i, 0, 0), memory_space=pltpu.SMEM),
            pl.BlockSpec((1, 2, tc), lambda i: (jnp.minimum(i + 1, n_steps - 1), 0, 0),
                         memory_space=pltpu.SMEM),
            pl.BlockSpec((tc, D_MODEL), row), pl.BlockSpec((tc, LANES), row),
            pl.BlockSpec(memory_space=pl.ANY),
            pl.BlockSpec((1, D_MODEL), const), pl.BlockSpec((1, D_MODEL), const),
        ],
        out_specs=pl.BlockSpec((tc, D_MODEL), row),
        out_shape=jax.ShapeDtypeStruct((T, D_MODEL), F32),
        scratch_shapes=[pltpu.VMEM((2, 2, tc, D_MODEL), F32), pltpu.SemaphoreType.DMA((2,))],
        compiler_params=_cparams("arbitrary"),
        name="moe_combine_ln",
    )(pos3, pos3, x2d, route, ys, g.reshape(1, D_MODEL), b.reshape(1, D_MODEL))


def _even_layer(x2d, B, S, w_in, pe_k, w_k, pe_v, w_v, conv_w, a_log, dt_bias, gdn_norm, w_out,
                ln1_g, ln1_b, wg, wu, wd, ln2_g, ln2_b):
    qt, vst, vwt, ks, kw, cmp2d, gqkv, z, small = _in_projection(x2d, w_in, S)
    kc, vc = _compress(cmp2d, pe_k, w_k, pe_v, w_v, B, S)
    o_a = _nsa_attention(qt, ks, vst, kw, vwt, kc, vc, small, B, S)
    o_b = _gdn_mixer(gqkv, z, small, conv_w, a_log, dt_bias, gdn_norm, B, S)
    x2d = _mix_out_ln(x2d, o_a, o_b, w_out, ln1_g, ln1_b)
    return _ffn_ln(x2d, wg, wu, wd, ln2_g, ln2_b)


def _odd_layer(x2d, B, S, pool_w, pool_scale, ln1_g, ln1_b, router_w, wg, wu, wd, ln2_g, ln2_b):
    x3d, route = _pool_ln(x2d.reshape(B, S, D_MODEL), pool_w, pool_scale, ln1_g, ln1_b, router_w)
    return _moe_ln(x3d.reshape(B * S, D_MODEL), route.reshape(B * S, LANES), wg, wu, wd, ln2_g, ln2_b)


def kernel(x, ev_w_in, ev_cmp_pe_k, ev_cmp_w_k, ev_cmp_pe_v, ev_cmp_w_v, ev_conv_w, ev_a_log, ev_dt_bias,
           ev_gdn_norm, ev_w_out, ev_ln1_g, ev_ln1_b, ev_ffn_wg, ev_ffn_wu, ev_ffn_wd, ev_ln2_g, ev_ln2_b,
           od_pool_w, od_pool_scale, od_ln1_g, od_ln1_b, od_router_w, od_exp_wg, od_exp_wu, od_exp_wd,
           od_ln2_g, od_ln2_b):
    B, S, _ = x.shape
    h = x.reshape(B * S, D_MODEL)
    for layer in range(DEPTH):
        i = layer // 2
        if layer % 2 == 0:
            h = _even_layer(h, B, S, ev_w_in[i], ev_cmp_pe_k[i], ev_cmp_w_k[i], ev_cmp_pe_v[i],
                            ev_cmp_w_v[i], ev_conv_w[i], ev_a_log[i], ev_dt_bias[i], ev_gdn_norm[i],
                            ev_w_out[i], ev_ln1_g[i], ev_ln1_b[i], ev_ffn_wg[i], ev_ffn_wu[i],
                            ev_ffn_wd[i], ev_ln2_g[i], ev_ln2_b[i])
        else:
            h = _odd_layer(h, B, S, od_pool_w[i], od_pool_scale[i], od_ln1_g[i], od_ln1_b[i],
                           od_router_w[i], od_exp_wg[i], od_exp_wu[i], od_exp_wd[i], od_ln2_g[i],
                           od_ln2_b[i])
    return h.reshape(B, S, D_MODEL)
```

```python
import functools
import math

import jax
import jax.numpy as jnp
from jax import lax
from jax.experimental import pallas as pl
from jax.experimental.pallas import tpu as pltpu

F32 = jnp.float32
BF16 = jnp.bfloat16

D_MODEL = 1024
DEPTH = 2
NSA_HEADS = 8
NSA_KV_HEADS = 2
NSA_GROUP = NSA_HEADS // NSA_KV_HEADS
NSA_HEAD_DIM = 64
CMP_BLOCK = 32
CMP_STRIDE = 16
SLC_BLOCK = 64
SLC_TOP_N = 16
SLC_LOCAL = 2
WINDOW = 512
Q_BLOCK = 128
FORCED_SCORE = 1e6
GDN_HEADS = 4
GDN_HEAD_DIM = 128
GDN_CONV = 4
GDN_CHUNK = 64
POOL_SIZES = (2, 4, 8, 16)
POOL_GROUP = D_MODEL // 4
D_FF = 2816
N_EXPERTS = 8
D_FF_EXPERT = 3584
ROPE_THETA = 10000.0
LN_EPS = 1e-5
NORM_EPS = 1e-6
DN_ALPHA = (2 * DEPTH) ** 0.25

NSA_Q = NSA_HEADS * NSA_HEAD_DIM
NSA_KV = NSA_KV_HEADS * NSA_HEAD_DIM
GDN_W = GDN_HEADS * GDN_HEAD_DIM
GDN_BD = GDN_HEADS * GDN_CHUNK

LANES = 128
NEG = -1e30
LOG2E = 1.4426950408889634
VMEM_LIMIT = 56 * 1024 * 1024


def _cparams(*sem):
    return pltpu.CompilerParams(dimension_semantics=sem, vmem_limit_bytes=VMEM_LIMIT)


def _dot(a, b):
    return jnp.dot(a, b, preferred_element_type=F32)


def _dot_nt(a, b):
    return lax.dot_general(a, b, (((1,), (1,)), ((), ())), preferred_element_type=F32)


def _dot_tn(a, b):
    return lax.dot_general(a, b, (((0,), (0,)), ((), ())), preferred_element_type=F32)


def _dot3(a, b):
    ah = a.astype(BF16)
    al = (a - ah.astype(F32)).astype(BF16)
    bh = b.astype(BF16)
    bl = (b - bh.astype(F32)).astype(BF16)
    return _dot(ah, bh) + (_dot(ah, bl) + _dot(al, bh))


def _dot2(a, b):
    ah = a.astype(BF16)
    al = (a - ah.astype(F32)).astype(BF16)
    bh = b.astype(BF16)
    return _dot(ah, bh) + _dot(al, bh)


def _layer_norm(y, g, b):
    mu = jnp.mean(y, -1, keepdims=True)
    d = y - mu
    var = jnp.mean(d * d, -1, keepdims=True)
    return d * lax.rsqrt(var + LN_EPS) * g + b


def _sigmoid(x):
    return 1.0 / (1.0 + jnp.exp(-x))


def _silu(x):
    return x * _sigmoid(x)


IN_TM = 512
_R_Q, _R_QR, _R_VS, _R_VW, _R_END = 0, 512, 1024, 1152, 1280
_C_KS, _C_KSR, _C_KW, _C_KWR, _C_CMP, _C_GQKV, _C_Z, _C_SM, _C_END = 0, 128, 256, 384, 512, 768, 2304, 2816, 2944


def _inproj_kernel(x_ref, wt_ref, wn_ref, cos_ref, sin_ref, cost_ref, sint_ref,
                   qt_ref, vst_ref, vwt_ref, ks_ref, kw_ref, cmp_ref, gqkv_ref, z_ref, sm_ref):
    D, HKV = NSA_HEAD_DIM, NSA_KV_HEADS
    tm = x_ref.shape[0]
    xb = x_ref[...].astype(BF16)

    def mt(r0, n):
        return _dot_nt(wt_ref[r0:r0 + n, :], xb)

    def mm(c0, n):
        return _dot(xb, wn_ref[:, c0:c0 + n])

    scale = D ** -0.5 * LOG2E
    cost = cost_ref[...]
    sint = sint_ref[...]
    for h0 in range(0, NSA_HEADS, NSA_GROUP):
        a = mt(_R_Q + h0 * D, NSA_GROUP * D)
        r = mt(_R_QR + h0 * D, NSA_GROUP * D)
        for g in range(NSA_GROUP):
            rows = slice(g * D, (g + 1) * D)
            qt_ref[h0 + g] = ((a[rows] * cost + r[rows] * sint) * scale).astype(BF16)
    ones_rows = jnp.where(lax.broadcasted_iota(jnp.int32, (D, tm), 0) == 0, 1.0, 0.0).astype(BF16)
    for r0, out in ((_R_VS, vst_ref), (_R_VW, vwt_ref)):
        vt = mt(r0, NSA_KV).astype(BF16)
        for h in range(HKV):
            out[h, 0:D, :] = vt[h * D:(h + 1) * D]
            out[h, D:2 * D, :] = ones_rows
    cos = cos_ref[...]
    sin = sin_ref[...]
    ks_ref[...] = (mm(_C_KS, LANES) * cos + mm(_C_KSR, LANES) * sin).astype(BF16)
    kw_ref[...] = (mm(_C_KW, LANES) * cos + mm(_C_KWR, LANES) * sin).astype(BF16)
    cmp_ref[...] = mm(_C_CMP, 2 * NSA_KV)
    gqkv_ref[...] = mm(_C_GQKV, 3 * GDN_W)
    z_ref[...] = mm(_C_Z, GDN_W)
    sm_ref[...] = mm(_C_SM, LANES)


def _rot_cols(w):
    k, n = w.shape
    half = NSA_HEAD_DIM // 2
    w4 = w.reshape(k, n // NSA_HEAD_DIM, 2, half)
    return jnp.stack([-w4[:, :, 1], w4[:, :, 0]], axis=2).reshape(k, n)


def _rope_tables(pos, reps):
    half = NSA_HEAD_DIM // 2
    inv = jnp.power(ROPE_THETA, -jnp.arange(half, dtype=F32) / half)
    ang = pos.astype(F32)[:, None] * inv[None, :]
    cos = jnp.tile(jnp.cos(ang), (1, 2 * reps))
    sin = jnp.tile(jnp.sin(ang), (1, 2 * reps))
    return cos, sin


def _in_projection(x2d, w_in, S):
    T = x2d.shape[0]
    o = 0
    parts = {}
    for name, n in (("q", NSA_Q), ("kc", NSA_KV), ("vc", NSA_KV), ("ks", NSA_KV), ("vs", NSA_KV),
                    ("kw", NSA_KV), ("vw", NSA_KV), ("gl", 3 * NSA_HEADS), ("gq", GDN_W), ("gk", GDN_W),
                    ("gv", GDN_W), ("gz", GDN_W), ("gb", GDN_HEADS), ("ga", GDN_HEADS)):
        parts[name] = w_in[:, o:o + n]
        o += n
    pad = jnp.zeros((D_MODEL, LANES - 3 * NSA_HEADS - 2 * GDN_HEADS), F32)
    w_t = jnp.concatenate([parts["q"], _rot_cols(parts["q"]), parts["vs"], parts["vw"]], axis=1).T.astype(BF16)
    w_n = jnp.concatenate([
        parts["ks"], _rot_cols(parts["ks"]), parts["kw"], _rot_cols(parts["kw"]), parts["kc"], parts["vc"],
        parts["gq"], parts["gk"], parts["gv"], parts["gz"], parts["gl"], parts["gb"], parts["ga"], pad,
    ], axis=1).astype(BF16)
    assert w_t.shape[0] == _R_END and w_n.shape[1] == _C_END
    cos, sin = _rope_tables(jnp.arange(S), LANES // NSA_HEAD_DIM)
    cos_t, sin_t = cos[:, :NSA_HEAD_DIM].T, sin[:, :NSA_HEAD_DIM].T
    tm = min(IN_TM, S)
    nblk = S // tm
    row = lambda i: (i, 0)
    col3 = lambda i: (0, 0, i)
    const = lambda i: (0, 0)
    return pl.pallas_call(
        _inproj_kernel,
        grid=(T // tm,),
        in_specs=[
            pl.BlockSpec((tm, D_MODEL), row),
            pl.BlockSpec((_R_END, D_MODEL), const), pl.BlockSpec((D_MODEL, _C_END), const),
            pl.BlockSpec((tm, LANES), lambda i: (i % nblk, 0)),
            pl.BlockSpec((tm, LANES), lambda i: (i % nblk, 0)),
            pl.BlockSpec((NSA_HEAD_DIM, tm), lambda i: (0, i % nblk)),
            pl.BlockSpec((NSA_HEAD_DIM, tm), lambda i: (0, i % nblk)),
        ],
        out_specs=[
            pl.BlockSpec((NSA_HEADS, NSA_HEAD_DIM, tm), col3),
            pl.BlockSpec((NSA_KV_HEADS, 2 * NSA_HEAD_DIM, tm), col3),
            pl.BlockSpec((NSA_KV_HEADS, 2 * NSA_HEAD_DIM, tm), col3),
            pl.BlockSpec((tm, NSA_KV), row), pl.BlockSpec((tm, NSA_KV), row),
            pl.BlockSpec((tm, 2 * NSA_KV), row), pl.BlockSpec((tm, 3 * GDN_W), row),
            pl.BlockSpec((tm, GDN_W), row), pl.BlockSpec((tm, LANES), row),
        ],
        out_shape=[
            jax.ShapeDtypeStruct((NSA_HEADS, NSA_HEAD_DIM, T), BF16),
            jax.ShapeDtypeStruct((NSA_KV_HEADS, 2 * NSA_HEAD_DIM, T), BF16),
            jax.ShapeDtypeStruct((NSA_KV_HEADS, 2 * NSA_HEAD_DIM, T), BF16),
            jax.ShapeDtypeStruct((T, NSA_KV), BF16), jax.ShapeDtypeStruct((T, NSA_KV), BF16),
            jax.ShapeDtypeStruct((T, 2 * NSA_KV), F32), jax.ShapeDtypeStruct((T, 3 * GDN_W), F32),
            jax.ShapeDtypeStruct((T, GDN_W), F32), jax.ShapeDtypeStruct((T, LANES), F32),
        ],
        compiler_params=_cparams("parallel"),
        name="in_projection",
    )(x2d, w_t, w_n, cos, sin, cos_t, sin_t)


def _compress_kernel(k_ref, v_ref, wk_ref, wv_ref, pe_ref, cos_ref, sin_ref, kc_ref, vc_ref):
    nb = k_ref.shape[2]
    k = k_ref[0, 0]
    v = v_ref[0, 0]
    klo = (k + pe_ref[0:1, :]).astype(BF16)
    khi = (k + pe_ref[1:2, :]).astype(BF16)
    vlo = (v + pe_ref[2:3, :]).astype(BF16)
    vhi = (v + pe_ref[3:4, :]).astype(BF16)

    def up(m):
        return pltpu.roll(m, nb - 1, axis=0)

    a = _dot(klo, wk_ref[0]) + up(_dot(khi, wk_ref[1]))
    ar = _dot(klo, wk_ref[2]) + up(_dot(khi, wk_ref[3]))
    kc_ref[0] = a * cos_ref[...] + ar * sin_ref[...]
    vc_ref[0] = _dot(vlo, wv_ref[0]) + up(_dot(vhi, wv_ref[1]))


def _compress(cmp2d, pe_k, w_k, pe_v, w_v, B, S):
    nb = S // CMP_STRIDE
    bh = B * NSA_KV_HEADS
    row_w = CMP_STRIDE * NSA_HEAD_DIM
    c = cmp2d.reshape(B, S, 2, NSA_KV_HEADS, NSA_HEAD_DIM).transpose(2, 0, 3, 1, 4)
    c = c.reshape(2, bh, nb, row_w)
    wk = w_k.reshape(2, row_w, NSA_HEAD_DIM)
    wk_all = jnp.concatenate([wk, jnp.stack([_rot_cols(wk[0]), _rot_cols(wk[1])])], 0).astype(BF16)
    wv_all = w_v.reshape(2, row_w, NSA_HEAD_DIM).astype(BF16)
    pe = jnp.concatenate([pe_k.reshape(2, row_w), pe_v.reshape(2, row_w)], 0)
    cos, sin = _rope_tables(jnp.arange(nb) * CMP_STRIDE + (CMP_BLOCK - 1) / 2, 1)
    full = lambda shape: pl.BlockSpec(shape, lambda i: (0,) * len(shape))
    kc, vc = pl.pallas_call(
        _compress_kernel,
        grid=(bh,),
        in_specs=[
            pl.BlockSpec((1, 1, nb, row_w), lambda i: (0, i, 0, 0)),
            pl.BlockSpec((1, 1, nb, row_w), lambda i: (1, i, 0, 0)),
            full((4, row_w, NSA_HEAD_DIM)), full((2, row_w, NSA_HEAD_DIM)), full((4, row_w)),
            full((nb, NSA_HEAD_DIM)), full((nb, NSA_HEAD_DIM)),
        ],
        out_specs=[pl.BlockSpec((1, nb, NSA_HEAD_DIM), lambda i: (i, 0, 0))] * 2,
        out_shape=[jax.ShapeDtypeStruct((bh, nb, NSA_HEAD_DIM), F32)] * 2,
        compiler_params=_cparams("parallel"),
        name="nsa_compress",
    )(c, c, wk_all, wv_all, pe, cos, sin)
    return kc, vc


NSA_QB = 256
SEL_KT = 1024
SEL_SUB = 512


def _nsa_kernel(qt_ref, kc_ref, vct_ref, ovt_ref, ks_ref, vst_ref, kw_ref, vwt_ref, gate_ref, o_ref, sel_scr,
                *, n_sel, kt_w, win_w):
    G, QB, D = NSA_GROUP, NSA_QB, NSA_HEAD_DIM
    GQ = G * QB
    nb = kc_ref.shape[2]
    ns = ovt_ref.shape[0]
    h = pl.program_id(1)
    qb = pl.program_id(2)
    s0 = qb * QB
    q4 = jnp.concatenate([qt_ref[g] for g in range(G)], axis=1)
    zeros = jnp.zeros_like(q4)
    qpad = jnp.concatenate([jnp.where(h == hh, q4, zeros) for hh in range(NSA_KV_HEADS)], axis=0)
    tq = s0 + lax.broadcasted_iota(jnp.int32, (1, QB), 1)
    lanes4 = lambda a: jnp.concatenate([a] * G, axis=1)

    s_c = _dot(kc_ref[0, 0], q4)
    nrow = lax.broadcasted_iota(jnp.int32, (nb, QB), 0)
    mask_c = lanes4((nrow * CMP_STRIDE + (CMP_BLOCK - 1)) <= tq)
    s_c = jnp.where(mask_c, s_c, NEG)
    m_c = jnp.max(s_c, 0, keepdims=True)
    e_c = jnp.where(mask_c, jnp.exp2(s_c - m_c), 0.0)
    l_c = jnp.sum(e_c, 0, keepdims=True)
    p_c = e_c / jnp.where(l_c > 0, l_c, 1.0)
    o_c = _dot(vct_ref[0, 0], p_c.astype(BF16))

    psum = p_c[:, 0:QB]
    for g in range(1, G):
        psum = psum + p_c[:, g * QB:(g + 1) * QB]
    p_hi = psum.astype(BF16)
    p_lo = (psum - p_hi.astype(F32)).astype(BF16)
    imp_t = _dot(ovt_ref[...], p_hi) + _dot(ovt_ref[...], p_lo)

    w0 = pl.multiple_of(jnp.maximum(s0 + QB - win_w, 0), QB)
    s_w = _dot(kw_ref[0, pl.ds(w0, win_w), :], qpad)
    rel = tq - (w0 + lax.broadcasted_iota(jnp.int32, (win_w, QB), 0))
    s_w = s_w + lanes4(jnp.where((rel >= 0) & (rel < WINDOW), 0.0, NEG))
    p_w = jnp.exp2((s_w - jnp.max(s_w, 0, keepdims=True)).astype(BF16))
    acc_w = _dot(vwt_ref[0, :, pl.ds(w0, win_w)], p_w)
    o_w = acc_w[:D] / acc_w[D:D + 1]

    jrow = lax.broadcasted_iota(jnp.int32, (ns, QB), 0)
    tcol = s0 + lax.broadcasted_iota(jnp.int32, (ns, QB), 1)
    cur = tcol // SLC_BLOCK
    forced = (jrow == 0) | ((jrow <= cur) & (jrow > cur - SLC_LOCAL))
    score = jnp.where((jrow * SLC_BLOCK) <= tcol, jnp.where(forced, FORCED_SCORE, imp_t), -1.0)
    jf = jrow.astype(F32)
    sel_t = jnp.zeros((ns, QB), F32)
    for _ in range(n_sel):
        mx = jnp.max(score, 0, keepdims=True)
        first = jnp.min(jnp.where(score == mx, jf, float(ns)), 0, keepdims=True)
        pick = jf == first
        sel_t = jnp.where(pick, 1.0, sel_t)
        score = jnp.where(pick, -2.0, score)
    sel_scr[...] = sel_t

    sub_w = min(SEL_SUB, kt_w)
    subs = kt_w // sub_w
    krow = lax.broadcasted_iota(jnp.int32, (sub_w, QB), 0)
    blocks_per_step = kt_w // SLC_BLOCK
    blocks_per_sub = sub_w // SLC_BLOCK

    n_kt = (s0 + QB + kt_w - 1) // kt_w

    def sel_step(kt, carry):
        k0s = [pl.multiple_of(kt * kt_w + c * sub_w, sub_w) for c in range(subs)]
        scores = [_dot(ks_ref[0, pl.ds(k0, sub_w), :], qpad) for k0 in k0s]
        probs = []
        for c, (m, _) in enumerate(carry):
            chosen = jnp.concatenate(
                [jnp.broadcast_to(sel_scr[pl.ds(kt * blocks_per_step + c * blocks_per_sub + j, 1), :],
                                  (SLC_BLOCK, QB)) for j in range(blocks_per_sub)], axis=0)
            bias = jnp.where((chosen > 0.5) & ((k0s[c] + krow) <= tq), 0.0, NEG)
            s = scores[c] + lanes4(bias)
            m_new = jnp.maximum(m, jnp.max(s, 0, keepdims=True))
            probs.append((m_new, jnp.exp2(m - m_new), jnp.exp2((s - m_new).astype(BF16))))
        return tuple((m_new, alpha * acc + _dot(vst_ref[0, :, pl.ds(k0, sub_w)], p))
                     for k0, (m_new, alpha, p), (_, acc) in zip(k0s, probs, carry))

    chains = lax.fori_loop(
        0, n_kt, sel_step,
        tuple((jnp.full((1, GQ), NEG, F32), jnp.zeros((2 * D, GQ), F32)) for _ in range(subs)))
    m_s = chains[0][0]
    for m, _ in chains[1:]:
        m_s = jnp.maximum(m_s, m)
    acc_s = sum(jnp.exp2(m - m_s) * acc for m, acc in chains)
    o_s = acc_s[:D] / acc_s[D:D + 1]

    gate = _sigmoid(gate_ref[0, 0])
    for g in range(G):
        cols = slice(g * QB, (g + 1) * QB)
        o_ref[g] = (gate[3 * g:3 * g + 1] * o_c[:, cols] + gate[3 * g + 1:3 * g + 2] * o_s[:, cols]
                    + gate[3 * g + 2:3 * g + 3] * o_w[:, cols]).astype(BF16)


def _nsa_attention(qt, ks, vst, kw, vwt, kc, vc, small, B, S):
    H, HKV, G, D = NSA_HEADS, NSA_KV_HEADS, NSA_GROUP, NSA_HEAD_DIM
    nb = S // CMP_STRIDE
    ns = S // SLC_BLOCK
    nq = S // NSA_QB
    n_sel = min(SLC_TOP_N, ns)
    kt_w = min(SEL_KT, S)
    win_w = min(WINDOW + NSA_QB, S)
    kcb = kc.reshape(B, HKV, nb, D).astype(BF16)
    vct = vc.reshape(B, HKV, nb, D).transpose(0, 1, 3, 2).astype(BF16)
    gates = small[:, :3 * H].reshape(B, S, HKV, 3 * G).transpose(0, 2, 3, 1)
    m_i = jnp.arange(ns)[:, None]
    n_i = jnp.arange(nb)[None, :]
    overlap_t = ((n_i * CMP_STRIDE <= m_i * SLC_BLOCK + SLC_BLOCK - 1)
                 & (n_i * CMP_STRIDE + CMP_BLOCK - 1 >= m_i * SLC_BLOCK)).astype(BF16)
    per_bh = lambda shape: pl.BlockSpec((1, 1) + shape, lambda b, h, i: (b, h, 0, 0))
    keys = pl.BlockSpec((1, S, NSA_KV), lambda b, h, i: (b, 0, 0))
    vals = pl.BlockSpec((1, 2 * D, S), lambda b, h, i: (h, 0, b))
    heads = pl.BlockSpec((G, D, NSA_QB), lambda b, h, i: (h, 0, b * nq + i))
    return pl.pallas_call(
        functools.partial(_nsa_kernel, n_sel=n_sel, kt_w=kt_w, win_w=win_w),
        grid=(B, HKV, nq),
        in_specs=[
            heads, per_bh((nb, D)), per_bh((D, nb)),
            pl.BlockSpec((ns, nb), lambda b, h, i: (0, 0)),
            keys, vals, keys, vals,
            pl.BlockSpec((1, 1, 3 * G, NSA_QB), lambda b, h, i: (b, h, 0, i)),
        ],
        out_specs=heads,
        out_shape=jax.ShapeDtypeStruct((H, D, B * S), BF16),
        scratch_shapes=[pltpu.VMEM((ns, NSA_QB), F32)],
        compiler_params=_cparams("parallel", "parallel", "arbitrary"),
        name="nsa_attention",
    )(qt, kcb, vct, overlap_t, ks.reshape(B, S, NSA_KV), vst, kw.reshape(B, S, NSA_KV), vwt, gates)


GDN_TB = 256
GDN_HALO = 8


def _gdn_prep_kernel(qkv_ref, halo_ref, sm_ref, cw_ref, alog_ref, dtb_ref,
                     u_ref, wq_ref, kd_ref, intra_ref, dec_ref, xs_scr, y_scr, bg_scr):
    C, HD, NH, BD = GDN_CHUNK, GDN_HEAD_DIM, GDN_HEADS, GDN_BD
    tb = qkv_ref.shape[1]
    first = pl.program_id(1) == 0
    xs_scr[GDN_HALO:GDN_HALO + tb, :] = qkv_ref[0]
    xs_scr[0:GDN_HALO, :] = jnp.where(first, 0.0, halo_ref[0])
    for cb in range(3 * GDN_W // LANES):
        cols = slice(cb * LANES, (cb + 1) * LANES)
        y = cw_ref[GDN_CONV - 1:GDN_CONV, cols] * xs_scr[GDN_HALO:GDN_HALO + tb, cols]
        for d in range(1, GDN_CONV):
            y = y + cw_ref[GDN_CONV - 1 - d:GDN_CONV - d, cols] * xs_scr[GDN_HALO - d:GDN_HALO - d + tb, cols]
        y_scr[:, cols] = _silu(y)
    sm = sm_ref[0]
    b_logit = sm[:, 3 * NSA_HEADS:3 * NSA_HEADS + NH]
    a_logit = sm[:, 3 * NSA_HEADS + NH:3 * NSA_HEADS + 2 * NH] + dtb_ref[...]
    softplus = jnp.maximum(a_logit, 0.0) + jnp.log(1.0 + jnp.exp(-jnp.abs(a_logit)))
    bg_scr[:, 0:NH] = _sigmoid(b_logit)
    bg_scr[:, NH:2 * NH] = -jnp.exp(alog_ref[...]) * softplus

    r = lax.broadcasted_iota(jnp.int32, (BD, BD), 0)
    c = lax.broadcasted_iota(jnp.int32, (BD, BD), 1)
    same = (r // C) == (c // C)
    eye = r == c
    tril = same & (r >= c)
    triu = same & (r <= c)
    strict = same & (r > c)
    eye_f = jnp.where(eye, 1.0, 0.0)

    def stack_heads(m, c0):
        return jnp.concatenate([m[:, c0 + h * HD:c0 + (h + 1) * HD] for h in range(NH)], axis=0)

    def chunk(ci, carry):
        r0 = pl.multiple_of(ci * C, C)
        yc = y_scr[pl.ds(r0, C), :]
        bg = bg_scr[pl.ds(r0, C), :]
        q4 = stack_heads(yc, 0)
        k4 = stack_heads(yc, GDN_W)
        v4 = stack_heads(yc, 2 * GDN_W)
        q4 = q4 * lax.rsqrt(jnp.sum(q4 * q4, -1, keepdims=True) + NORM_EPS) * (HD ** -0.5)
        k4 = k4 * lax.rsqrt(jnp.sum(k4 * k4, -1, keepdims=True) + NORM_EPS)
        b4 = jnp.concatenate([bg[:, h:h + 1] for h in range(NH)], axis=0)
        g4 = jnp.concatenate([bg[:, NH + h:NH + h + 1] for h in range(NH)], axis=0)
        g_row = jnp.sum(jnp.where(eye, g4, 0.0), axis=0, keepdims=True)
        gc_col = jnp.sum(jnp.where(tril, g_row, 0.0), axis=1, keepdims=True)
        gc_row = jnp.sum(jnp.where(triu, g4, 0.0), axis=0, keepdims=True)
        gl_col = jnp.sum(jnp.where(same, g_row, 0.0), axis=1, keepdims=True)
        decay = jnp.where(tril, jnp.exp(jnp.minimum(gc_col - gc_row, 0.0)), 0.0)
        kb4 = k4 * b4
        k4b = k4.astype(BF16)
        lmat = jnp.where(strict, _dot_nt(kb4.astype(BF16), k4b) * decay, 0.0)
        intra = jnp.where(tril, _dot_nt(q4.astype(BF16), k4b) * decay, 0.0)
        inv = eye_f - lmat
        pw = lmat
        for _ in range(int(math.log2(C)) - 1):
            pw = _dot2(pw, pw)
            inv = _dot2(inv, eye_f + pw)
        egc = jnp.exp(gc_col)
        uw = _dot2(inv, jnp.concatenate([v4 * b4, kb4 * egc], axis=1))
        u_ref[0, ci] = uw[:, :HD]
        w4 = uw[:, HD:].astype(BF16).reshape(NH, C, HD)
        qg4 = (q4 * egc).astype(BF16).reshape(NH, C, HD)
        wq_ref[0, ci] = jnp.concatenate([w4, qg4], axis=1)
        kd_ref[0, ci] = (k4 * jnp.exp(gl_col - gc_col)).astype(BF16)
        intra_ref[0, ci] = intra.astype(BF16)
        dec = jnp.broadcast_to(jnp.exp(gl_col), (BD, LANES)).reshape(NH, C, LANES)
        dec_ref[0, ci] = dec[:, 0:8, :]
        return carry

    for ci in range(tb // C):
        chunk(ci, 0)


def _gdn_scan_kernel(u_ref, wq_ref, kd_ref, intra_ref, dec_ref, z_ref, nw_ref, o_ref, s_scr):
    C, HD, NH = GDN_CHUNK, GDN_HEAD_DIM, GDN_HEADS
    nbatch = u_ref.shape[0]

    @pl.when(pl.program_id(0) == 0)
    def _():
        s_scr[...] = jnp.zeros_like(s_scr)

    nw = nw_ref[...]
    for b in range(nbatch):
        vnew, qs = [], []
        for h in range(NH):
            st = s_scr[b * NH + h]
            res = _dot(wq_ref[b, 0, h], st.astype(BF16))
            vnew.append(u_ref[b, 0, h * C:(h + 1) * C, :] - res[:C])
            qs.append(res[C:])
        vnew4 = jnp.concatenate(vnew, axis=0).astype(BF16)
        o4 = jnp.concatenate(qs, axis=0) + _dot(intra_ref[b, 0], vnew4)
        for h in range(NH):
            rows = slice(h * C, (h + 1) * C)
            st = s_scr[b * NH + h]
            s_scr[b * NH + h] = st * dec_ref[b, 0, h, 0:1, :] + _dot_tn(kd_ref[b, 0, rows, :], vnew4[rows])
            oh = o4[rows]
            oh = oh * lax.rsqrt(jnp.mean(oh * oh, -1, keepdims=True) + NORM_EPS) * nw
            o_ref[b, :, h * HD:(h + 1) * HD] = oh * _silu(z_ref[b, :, h * HD:(h + 1) * HD])


def _gdn_mixer(gqkv, z, small, conv_w, a_log, dt_bias, norm_w, B, S):
    C, HD, NH, BD = GDN_CHUNK, GDN_HEAD_DIM, GDN_HEADS, GDN_BD
    nc = S // C
    tb = min(GDN_TB, S)
    cb = tb // C
    qkv3 = gqkv.reshape(B, S, 3 * GDN_W)
    sm3 = small.reshape(B, S, LANES)
    hb = tb // GDN_HALO
    u, wq, kd, intra, dec = pl.pallas_call(
        _gdn_prep_kernel,
        grid=(B, S // tb),
        in_specs=[
            pl.BlockSpec((1, tb, 3 * GDN_W), lambda b, i: (b, i, 0)),
            pl.BlockSpec((1, GDN_HALO, 3 * GDN_W), lambda b, i: (b, jnp.maximum(i * hb - 1, 0), 0)),
            pl.BlockSpec((1, tb, LANES), lambda b, i: (b, i, 0)),
            pl.BlockSpec((GDN_CONV, 3 * GDN_W), lambda b, i: (0, 0)),
            pl.BlockSpec((1, NH), lambda b, i: (0, 0)),
            pl.BlockSpec((1, NH), lambda b, i: (0, 0)),
        ],
        out_specs=[
            pl.BlockSpec((1, cb, BD, HD), lambda b, i: (b, i, 0, 0)),
            pl.BlockSpec((1, cb, NH, 2 * C, HD), lambda b, i: (b, i, 0, 0, 0)),
            pl.BlockSpec((1, cb, BD, HD), lambda b, i: (b, i, 0, 0)),
            pl.BlockSpec((1, cb, BD, BD), lambda b, i: (b, i, 0, 0)),
            pl.BlockSpec((1, cb, NH, 8, LANES), lambda b, i: (b, i, 0, 0, 0)),
        ],
        out_shape=[
            jax.ShapeDtypeStruct((B, nc, BD, HD), F32),
            jax.ShapeDtypeStruct((B, nc, NH, 2 * C, HD), BF16),
            jax.ShapeDtypeStruct((B, nc, BD, HD), BF16),
            jax.ShapeDtypeStruct((B, nc, BD, BD), BF16),
            jax.ShapeDtypeStruct((B, nc, NH, 8, LANES), F32),
        ],
        scratch_shapes=[
            pltpu.VMEM((GDN_HALO + tb, 3 * GDN_W), F32),
            pltpu.VMEM((tb, 3 * GDN_W), F32),
            pltpu.VMEM((tb, 2 * NH), F32),
        ],
        compiler_params=_cparams("parallel", "parallel"),
        name="gdn_prepare",
    )(qkv3, qkv3, sm3, conv_w, a_log.reshape(1, NH), dt_bias.reshape(1, NH))

    o = pl.pallas_call(
        _gdn_scan_kernel,
        grid=(nc,),
        in_specs=[
            pl.BlockSpec((B, 1, BD, HD), lambda c: (0, c, 0, 0)),
            pl.BlockSpec((B, 1, NH, 2 * C, HD), lambda c: (0, c, 0, 0, 0)),
            pl.BlockSpec((B, 1, BD, HD), lambda c: (0, c, 0, 0)),
            pl.BlockSpec((B, 1, BD, BD), lambda c: (0, c, 0, 0)),
            pl.BlockSpec((B, 1, NH, 8, LANES), lambda c: (0, c, 0, 0, 0)),
            pl.BlockSpec((B, C, GDN_W), lambda c: (0, c, 0)),
            pl.BlockSpec((1, HD), lambda c: (0, 0)),
        ],
        out_specs=pl.BlockSpec((B, C, GDN_W), lambda c: (0, c, 0)),
        out_shape=jax.ShapeDtypeStruct((B, S, GDN_W), F32),
        scratch_shapes=[pltpu.VMEM((B * NH, HD, HD), F32)],
        compiler_params=_cparams("arbitrary"),
        name="gdn_scan",
    )(u, wq, kd, intra, dec, z.reshape(B, S, GDN_W), norm_w.reshape(1, HD))
    return o.reshape(B * S, GDN_W)


MIX_TM = 512


def _mix_ln_kernel(x_ref, at_ref, b_ref, w_ref, g_ref, be_ref, o_ref):
    a_t = at_ref[...].reshape(NSA_Q, x_ref.shape[0])
    mix = (_dot_tn(a_t, w_ref[0:NSA_Q, :])
           + _dot(b_ref[...].astype(BF16), w_ref[NSA_Q:NSA_Q + GDN_W, :]))
    o_ref[...] = _layer_norm(DN_ALPHA * x_ref[...] + mix, g_ref[...], be_ref[...])


def _mix_out_ln(x2d, o_at, o_b, w_out, g, b):
    T = x2d.shape[0]
    row = lambda i: (i, 0)
    const = lambda i: (0, 0)
    return pl.pallas_call(
        _mix_ln_kernel,
        grid=(T // MIX_TM,),
        in_specs=[
            pl.BlockSpec((MIX_TM, D_MODEL), row),
            pl.BlockSpec((NSA_HEADS, NSA_HEAD_DIM, MIX_TM), lambda i: (0, 0, i)),
            pl.BlockSpec((MIX_TM, GDN_W), row), pl.BlockSpec((NSA_Q + GDN_W, D_MODEL), const),
            pl.BlockSpec((1, D_MODEL), const), pl.BlockSpec((1, D_MODEL), const),
        ],
        out_specs=pl.BlockSpec((MIX_TM, D_MODEL), row),
        out_shape=jax.ShapeDtypeStruct((T, D_MODEL), F32),
        compiler_params=_cparams("parallel"),
        name="mix_out_ln",
    )(x2d, o_at, o_b, w_out.astype(BF16), g.reshape(1, D_MODEL), b.reshape(1, D_MODEL))


FFN_TM = 512
FFN_TF = 256


def _ffn_ln_kernel(x_ref, wg_ref, wu_ref, wd_ref, g_ref, be_ref, o_ref, hid_scr):
    x = x_ref[...]
    xb = x.astype(BF16)
    for j in range(hid_scr.shape[1] // FFN_TF):
        cols = slice(j * FFN_TF, (j + 1) * FFN_TF)
        hid_scr[:, cols] = (_silu(_dot(xb, wg_ref[:, cols])) * _dot(xb, wu_ref[:, cols])).astype(BF16)
    o_ref[...] = _layer_norm(DN_ALPHA * x + _dot(hid_scr[...], wd_ref[...]), g_ref[...], be_ref[...])


def _ffn_ln(x2d, wg, wu, wd, g, b):
    T = x2d.shape[0]
    dff = wg.shape[1]
    tm = min(FFN_TM, T)
    const = lambda i: (0, 0)
    resident = lambda shape: pl.BlockSpec(shape, const, pipeline_mode=pl.Buffered(1))
    return pl.pallas_call(
        _ffn_ln_kernel,
        grid=(T // tm,),
        in_specs=[
            pl.BlockSpec((tm, D_MODEL), lambda i: (i, 0)),
            resident((D_MODEL, dff)), resident((D_MODEL, dff)), resident((dff, D_MODEL)),
            pl.BlockSpec((1, D_MODEL), const), pl.BlockSpec((1, D_MODEL), const),
        ],
        out_specs=pl.BlockSpec((tm, D_MODEL), lambda i: (i, 0)),
        out_shape=jax.ShapeDtypeStruct((T, D_MODEL), F32),
        scratch_shapes=[pltpu.VMEM((tm, dff), BF16)],
        compiler_params=_cparams("parallel"),
        name="ffn_ln",
    )(x2d, wg.astype(BF16), wu.astype(BF16), wd.astype(BF16), g.reshape(1, D_MODEL), b.reshape(1, D_MODEL))


POOL_TM = 512
POOL_HALO = 16


def _route_top2(x1, rw):
    lane = lax.broadcasted_iota(jnp.int32, (x1.shape[0], LANES), 1)
    lanef = lane.astype(F32)
    logits = jnp.where(lane < N_EXPERTS, _dot3(x1, rw), NEG)
    ex = jnp.exp(logits - jnp.max(logits, -1, keepdims=True))
    probs = jnp.where(lane < N_EXPERTS, ex / jnp.sum(ex, -1, keepdims=True), -1.0)
    p1 = jnp.max(probs, -1, keepdims=True)
    i1 = jnp.min(jnp.where(probs == p1, lanef, float(LANES)), -1, keepdims=True)
    rest = jnp.where(lanef == i1, -1.0, probs)
    p2 = jnp.max(rest, -1, keepdims=True)
    i2 = jnp.min(jnp.where(rest == p2, lanef, float(LANES)), -1, keepdims=True)
    den = p1 + p2
    return jnp.where(lane == 0, i1, jnp.where(lane == 1, i2, jnp.where(
        lane == 2, p1 / den, jnp.where(lane == 3, p2 / den, 0.0))))


def _pool_ln_kernel(x_ref, halo_ref, pw_ref, ps_ref, g_ref, be_ref, rw_ref, o_ref, route_ref, xs_scr):
    tm = x_ref.shape[1]
    i = pl.program_id(1)
    x = x_ref[0]
    xs_scr[POOL_HALO:POOL_HALO + tm, :] = x
    xs_scr[0:POOL_HALO, :] = jnp.where(i == 0, 0.0, halo_ref[0])
    t1 = (i * tm + 1 + lax.broadcasted_iota(jnp.int32, (tm, 1), 0)).astype(F32)
    for gi, win in enumerate(POOL_SIZES):
        cols = slice(gi * POOL_GROUP, (gi + 1) * POOL_GROUP)
        acc = x[:, cols]
        for d in range(1, win):
            acc = acc + xs_scr[POOL_HALO - d:POOL_HALO - d + tm, cols]
        mean = acc / jnp.minimum(t1, float(win))
        mix = _dot((mean - x[:, cols]).astype(BF16), pw_ref[gi]) * ps_ref[:, cols]
        xs_scr[POOL_HALO:POOL_HALO + tm, cols] = DN_ALPHA * x[:, cols] + mix
    x1 = _layer_norm(xs_scr[POOL_HALO:POOL_HALO + tm, :], g_ref[...], be_ref[...])
    o_ref[0] = x1
    route_ref[0] = _route_top2(x1, rw_ref[...])


def _pool_ln(x3d, pool_w, pool_scale, g, b, router_w):
    B, S, _ = x3d.shape
    tm = min(POOL_TM, S)
    hb = tm // POOL_HALO
    const = lambda bb, i: (0, 0)
    rw = jnp.concatenate([router_w, jnp.zeros((D_MODEL, LANES - N_EXPERTS), F32)], axis=1)
    return pl.pallas_call(
        _pool_ln_kernel,
        grid=(B, S // tm),
        in_specs=[
            pl.BlockSpec((1, tm, D_MODEL), lambda bb, i: (bb, i, 0)),
            pl.BlockSpec((1, POOL_HALO, D_MODEL), lambda bb, i: (bb, jnp.maximum(i * hb - 1, 0), 0)),
            pl.BlockSpec((len(POOL_SIZES), POOL_GROUP, POOL_GROUP), lambda bb, i: (0, 0, 0)),
            pl.BlockSpec((1, D_MODEL), const), pl.BlockSpec((1, D_MODEL), const),
            pl.BlockSpec((1, D_MODEL), const), pl.BlockSpec((D_MODEL, LANES), const),
        ],
        out_specs=[pl.BlockSpec((1, tm, D_MODEL), lambda bb, i: (bb, i, 0)),
                   pl.BlockSpec((1, tm, LANES), lambda bb, i: (bb, i, 0))],
        out_shape=[jax.ShapeDtypeStruct((B, S, D_MODEL), F32), jax.ShapeDtypeStruct((B, S, LANES), F32)],
        scratch_shapes=[pltpu.VMEM((POOL_HALO + tm, D_MODEL), F32)],
        compiler_params=_cparams("parallel", "parallel"),
        name="pool_ln",
    )(x3d, x3d, pool_w.astype(BF16), pool_scale.reshape(1, D_MODEL), g.reshape(1, D_MODEL),
      b.reshape(1, D_MODEL), rw)


MOE_TM = 1024
MOE_TF = 512
MOE_TC = 512


def _moe_plan(route, T, tm):
    ne = N_EXPERTS
    e_flat = jnp.concatenate([route[:, 0], route[:, 1]]).astype(jnp.int32)
    onehot = (e_flat[:, None] == jnp.arange(ne)[None, :]).astype(jnp.int32)
    csum = jnp.cumsum(onehot, axis=0)
    rank = jnp.sum(onehot * csum, axis=1) - 1
    counts = csum[-1]
    padded = (counts + tm - 1) // tm * tm
    gend = jnp.cumsum(padded)
    gstart = gend - padded
    pos = gstart[e_flat] + rank
    n_tiles = (2 * T) // tm + ne
    tile_e = jnp.minimum(jnp.searchsorted(gend, jnp.arange(n_tiles) * tm, side="right"), ne - 1)
    tile_e = tile_e.astype(jnp.int32)
    n_used = (gend[-1] // tm).astype(jnp.int32).reshape(1)
    order = jnp.argsort(e_flat, stable=True)
    cstart = jnp.cumsum(counts) - counts
    e_p = jnp.repeat(tile_e, tm)
    r = jnp.clip(jnp.arange(n_tiles * tm) - gstart[e_p], 0, jnp.maximum(counts[e_p] - 1, 0))
    src = (order[jnp.clip(cstart[e_p] + r, 0, 2 * T - 1)] % T).astype(jnp.int32)
    return pos.astype(jnp.int32), src, tile_e, n_used, n_tiles


SUBLANES = 8


def _gather_rows(src_hbm, idx, dst, sem):
    def issue(blk, c):
        for u in range(SUBLANES):
            row = idx(blk * SUBLANES + u)
            pltpu.make_async_copy(src_hbm.at[row >> 3, pl.ds(row & (SUBLANES - 1), 1)],
                                  dst.at[blk, pl.ds(u, 1)], sem).start()
        return c

    lax.fori_loop(0, dst.shape[0], issue, 0)


def _gather_wait(src_hbm, dst, sem):
    pltpu.make_async_copy(src_hbm.at[pl.ds(0, dst.shape[0])], dst, sem).wait()


def _moe_expert_kernel(te_ref, nu_ref, src_ref, nxt_ref, x_hbm, wg_ref, wu_ref, wd_ref, o_ref,
                       xbuf, xb_scr, acc_scr, sem):
    i = pl.program_id(0)
    j = pl.program_id(1)
    n_used = nu_ref[0]
    used = i < n_used
    slot = i % 2

    @pl.when((i == 0) & (j == 0))
    def _():
        _gather_rows(x_hbm, lambda r: src_ref[0, 0, r], xbuf.at[0], sem.at[0])

    @pl.when((i + 1 < n_used) & (j == 0))
    def _():
        _gather_rows(x_hbm, lambda r: nxt_ref[0, 0, r], xbuf.at[1 - slot], sem.at[1 - slot])

    @pl.when(used & (j == 0))
    def _():
        _gather_wait(x_hbm, xbuf.at[slot], sem.at[slot])
        xb_scr[...] = xbuf[slot].reshape(xb_scr.shape).astype(BF16)
        acc_scr[...] = jnp.zeros_like(acc_scr)

    @pl.when(used)
    def _():
        xb = xb_scr[...]
        hid = _silu(_dot(xb, wg_ref[0])) * _dot(xb, wu_ref[0])
        acc_scr[...] += _dot(hid.astype(BF16), wd_ref[0])

    last = j == pl.num_programs(1) - 1

    @pl.when(used & last)
    def _():
        o_ref[...] = acc_scr[...]

    @pl.when(jnp.logical_not(used) & last)
    def _():
        o_ref[...] = jnp.zeros_like(o_ref)


def _moe_combine_kernel(pos_ref, nxt_ref, x_ref, route_ref, ys_hbm, g_ref, be_ref, o_ref, ybuf, sem):
    i = pl.program_id(0)
    slot = i % 2

    def fetch(p_ref, s):
        for k in range(2):
            _gather_rows(ys_hbm, lambda r: p_ref[0, k, r], ybuf.at[s, k], sem.at[s])

    @pl.when(i == 0)
    def _():
        fetch(pos_ref, 0)

    @pl.when(i + 1 < pl.num_programs(0))
    def _():
        fetch(nxt_ref, 1 - slot)

    for k in range(2):
        _gather_wait(ys_hbm, ybuf.at[slot, k], sem.at[slot])
    route = route_ref[...]
    y0 = ybuf[slot, 0].reshape(x_ref.shape)
    y1 = ybuf[slot, 1].reshape(x_ref.shape)
    y = DN_ALPHA * x_ref[...] + route[:, 2:3] * y0 + route[:, 3:4] * y1
    o_ref[...] = _layer_norm(y, g_ref[...], be_ref[...])


def _moe_ln(x2d, route, wg, wu, wd, g, b):
    T = x2d.shape[0]
    ne, _, dff = wg.shape
    tm = min(MOE_TM, T)
    tc = min(MOE_TC, T)
    pos, src, tile_e, n_used, n_tiles = _moe_plan(route, T, tm)

    nj = dff // MOE_TF

    def w_col(i, j, te, nu):
        return (te[i], 0, jnp.where(i < nu[0], j, 0))

    def w_row(i, j, te, nu):
        return (te[i], jnp.where(i < nu[0], j, 0), 0)

    src3 = src.reshape(n_tiles, 1, tm)
    ys = pl.pallas_call(
        _moe_expert_kernel,
        grid_spec=pltpu.PrefetchScalarGridSpec(
            num_scalar_prefetch=2,
            grid=(n_tiles, nj),
            in_specs=[
                pl.BlockSpec((1, 1, tm), lambda i, j, te, nu: (i, 0, 0), memory_space=pltpu.SMEM),
                pl.BlockSpec((1, 1, tm), lambda i, j, te, nu: (jnp.minimum(i + 1, n_tiles - 1), 0, 0),
                             memory_space=pltpu.SMEM),
                pl.BlockSpec(memory_space=pl.ANY),
                pl.BlockSpec((1, D_MODEL, MOE_TF), w_col),
                pl.BlockSpec((1, D_MODEL, MOE_TF), w_col),
                pl.BlockSpec((1, MOE_TF, D_MODEL), w_row),
            ],
            out_specs=pl.BlockSpec((tm, D_MODEL), lambda i, j, te, nu: (i, 0)),
            scratch_shapes=[pltpu.VMEM((2, tm // SUBLANES, SUBLANES, D_MODEL), F32),
                            pltpu.VMEM((tm, D_MODEL), BF16), pltpu.VMEM((tm, D_MODEL), F32),
                            pltpu.SemaphoreType.DMA((2,))],
        ),
        out_shape=jax.ShapeDtypeStruct((n_tiles * tm, D_MODEL), F32),
        compiler_params=_cparams("arbitrary", "arbitrary"),
        name="moe_experts",
    )(tile_e, n_used, src3, src3, x2d.reshape(T // SUBLANES, SUBLANES, D_MODEL), wg.astype(BF16),
      wu.astype(BF16), wd.astype(BF16))

    n_steps = T // tc
    pos3 = pos.reshape(2, n_steps, tc).transpose(1, 0, 2)
    row = lambda i: (i, 0)
    const = lambda i: (0, 0)
    return pl.pallas_call(
        _moe_combine_kernel,
        grid=(n_steps,),
        in_specs=[
            pl.BlockSpec((1, 2, tc), lambda i: (i, 0, 0), memory_space=pltpu.SMEM),
            pl.BlockSpec((1, 2, tc), lambda i: (jnp.minimum(i + 1, n_steps - 1), 0, 0),
                         memory_space=pltpu.SMEM),
            pl.BlockSpec((tc, D_MODEL), row), pl.BlockSpec((tc, LANES), row),
            pl.BlockSpec(memory_space=pl.ANY),
            pl.BlockSpec((1, D_MODEL), const), pl.BlockSpec((1, D_MODEL), const),
        ],
        out_specs=pl.BlockSpec((tc, D_MODEL), row),
        out_shape=jax.ShapeDtypeStruct((T, D_MODEL), F32),
        scratch_shapes=[pltpu.VMEM((2, 2, tc // SUBLANES, SUBLANES, D_MODEL), F32),
                        pltpu.SemaphoreType.DMA((2,))],
        compiler_params=_cparams("arbitrary"),
        name="moe_combine_ln",
    )(pos3, pos3, x2d, route, ys.reshape(n_tiles * tm // SUBLANES, SUBLANES, D_MODEL),
      g.reshape(1, D_MODEL), b.reshape(1, D_MODEL))


def _even_layer(x2d, B, S, w_in, pe_k, w_k, pe_v, w_v, conv_w, a_log, dt_bias, gdn_norm, w_out,
                ln1_g, ln1_b, wg, wu, wd, ln2_g, ln2_b):
    qt, vst, vwt, ks, kw, cmp2d, gqkv, z, small = _in_projection(x2d, w_in, S)
    kc, vc = _compress(cmp2d, pe_k, w_k, pe_v, w_v, B, S)
    o_a = _nsa_attention(qt, ks, vst, kw, vwt, kc, vc, small, B, S)
    o_b = _gdn_mixer(gqkv, z, small, conv_w, a_log, dt_bias, gdn_norm, B, S)
    x2d = _mix_out_ln(x2d, o_a, o_b, w_out, ln1_g, ln1_b)
    return _ffn_ln(x2d, wg, wu, wd, ln2_g, ln2_b)


def _odd_layer(x2d, B, S, pool_w, pool_scale, ln1_g, ln1_b, router_w, wg, wu, wd, ln2_g, ln2_b):
    x3d, route = _pool_ln(x2d.reshape(B, S, D_MODEL), pool_w, pool_scale, ln1_g, ln1_b, router_w)
    return _moe_ln(x3d.reshape(B * S, D_MODEL), route.reshape(B * S, LANES), wg, wu, wd, ln2_g, ln2_b)


def kernel(x, ev_w_in, ev_cmp_pe_k, ev_cmp_w_k, ev_cmp_pe_v, ev_cmp_w_v, ev_conv_w, ev_a_log, ev_dt_bias,
           ev_gdn_norm, ev_w_out, ev_ln1_g, ev_ln1_b, ev_ffn_wg, ev_ffn_wu, ev_ffn_wd, ev_ln2_g, ev_ln2_b,
           od_pool_w, od_pool_scale, od_ln1_g, od_ln1_b, od_router_w, od_exp_wg, od_exp_wu, od_exp_wd,
           od_ln2_g, od_ln2_b):
    B, S, _ = x.shape
    h = x.reshape(B * S, D_MODEL)
    for layer in range(DEPTH):
        i = layer // 2
        if layer % 2 == 0:
            h = _even_layer(h, B, S, ev_w_in[i], ev_cmp_pe_k[i], ev_cmp_w_k[i], ev_cmp_pe_v[i],
                            ev_cmp_w_v[i], ev_conv_w[i], ev_a_log[i], ev_dt_bias[i], ev_gdn_norm[i],
                            ev_w_out[i], ev_ln1_g[i], ev_ln1_b[i], ev_ffn_wg[i], ev_ffn_wu[i],
                            ev_ffn_wd[i], ev_ln2_g[i], ev_ln2_b[i])
        else:
            h = _odd_layer(h, B, S, od_pool_w[i], od_pool_scale[i], od_ln1_g[i], od_ln1_b[i],
                           od_router_w[i], od_exp_wg[i], od_exp_wu[i], od_exp_wd[i], od_ln2_g[i],
                           od_ln2_b[i])
    return h.reshape(B, S, D_MODEL)
```

```python
import functools
import math

import jax
import jax.numpy as jnp
from jax import lax
from jax.experimental import pallas as pl
from jax.experimental.pallas import tpu as pltpu

F32 = jnp.float32
BF16 = jnp.bfloat16

D_MODEL = 1024
DEPTH = 2
NSA_HEADS = 8
NSA_KV_HEADS = 2
NSA_GROUP = NSA_HEADS // NSA_KV_HEADS
NSA_HEAD_DIM = 64
CMP_BLOCK = 32
CMP_STRIDE = 16
SLC_BLOCK = 64
SLC_TOP_N = 16
SLC_LOCAL = 2
WINDOW = 512
Q_BLOCK = 128
FORCED_SCORE = 1e6
GDN_HEADS = 4
GDN_HEAD_DIM = 128
GDN_CONV = 4
GDN_CHUNK = 64
POOL_SIZES = (2, 4, 8, 16)
POOL_GROUP = D_MODEL // 4
D_FF = 2816
N_EXPERTS = 8
D_FF_EXPERT = 3584
ROPE_THETA = 10000.0
LN_EPS = 1e-5
NORM_EPS = 1e-6
DN_ALPHA = (2 * DEPTH) ** 0.25

NSA_Q = NSA_HEADS * NSA_HEAD_DIM
NSA_KV = NSA_KV_HEADS * NSA_HEAD_DIM
GDN_W = GDN_HEADS * GDN_HEAD_DIM
GDN_BD = GDN_HEADS * GDN_CHUNK

LANES = 128
NEG = -1e30
LOG2E = 1.4426950408889634
VMEM_LIMIT = 56 * 1024 * 1024


def _cparams(*sem):
    return pltpu.CompilerParams(dimension_semantics=sem, vmem_limit_bytes=VMEM_LIMIT)


def _dot(a, b):
    return jnp.dot(a, b, preferred_element_type=F32)


def _dot_nt(a, b):
    return lax.dot_general(a, b, (((1,), (1,)), ((), ())), preferred_element_type=F32)


def _dot_tn(a, b):
    return lax.dot_general(a, b, (((0,), (0,)), ((), ())), preferred_element_type=F32)


def _dot3(a, b):
    ah = a.astype(BF16)
    al = (a - ah.astype(F32)).astype(BF16)
    bh = b.astype(BF16)
    bl = (b - bh.astype(F32)).astype(BF16)
    return _dot(ah, bh) + (_dot(ah, bl) + _dot(al, bh))


def _dot2(a, b):
    ah = a.astype(BF16)
    al = (a - ah.astype(F32)).astype(BF16)
    bh = b.astype(BF16)
    return _dot(ah, bh) + _dot(al, bh)


def _layer_norm(y, g, b):
    mu = jnp.mean(y, -1, keepdims=True)
    d = y - mu
    var = jnp.mean(d * d, -1, keepdims=True)
    return d * lax.rsqrt(var + LN_EPS) * g + b


def _sigmoid(x):
    return 1.0 / (1.0 + jnp.exp(-x))


def _silu(x):
    return x * _sigmoid(x)


IN_TM = 512
_R_Q, _R_QR, _R_VS, _R_VW, _R_END = 0, 512, 1024, 1152, 1280
_C_KS, _C_KSR, _C_KW, _C_KWR, _C_CMP, _C_GQKV, _C_Z, _C_SM, _C_END = 0, 128, 256, 384, 512, 768, 2304, 2816, 2944


def _inproj_kernel(x_ref, wt_ref, wn_ref, cos_ref, sin_ref, cost_ref, sint_ref,
                   qt_ref, vst_ref, vwt_ref, ks_ref, kw_ref, cmp_ref, gqkv_ref, z_ref, sm_ref):
    D, HKV = NSA_HEAD_DIM, NSA_KV_HEADS
    tm = x_ref.shape[0]
    xb = x_ref[...].astype(BF16)

    def mt(r0, n):
        return _dot_nt(wt_ref[r0:r0 + n, :], xb)

    def mm(c0, n):
        return _dot(xb, wn_ref[:, c0:c0 + n])

    scale = D ** -0.5 * LOG2E
    cost = cost_ref[...]
    sint = sint_ref[...]
    for h0 in range(0, NSA_HEADS, NSA_GROUP):
        a = mt(_R_Q + h0 * D, NSA_GROUP * D)
        r = mt(_R_QR + h0 * D, NSA_GROUP * D)
        for g in range(NSA_GROUP):
            rows = slice(g * D, (g + 1) * D)
            qt_ref[h0 + g] = ((a[rows] * cost + r[rows] * sint) * scale).astype(BF16)
    ones_rows = jnp.where(lax.broadcasted_iota(jnp.int32, (D, tm), 0) == 0, 1.0, 0.0).astype(BF16)
    for r0, out in ((_R_VS, vst_ref), (_R_VW, vwt_ref)):
        vt = mt(r0, NSA_KV).astype(BF16)
        for h in range(HKV):
            out[h, 0:D, :] = vt[h * D:(h + 1) * D]
            out[h, D:2 * D, :] = ones_rows
    cos = cos_ref[...]
    sin = sin_ref[...]
    ks_ref[:, 0:LANES] = (mm(_C_KS, LANES) * cos + mm(_C_KSR, LANES) * sin).astype(BF16)
    r_i = lax.broadcasted_iota(jnp.int32, (tm, NSA_KAUG - LANES), 0)
    c_i = lax.broadcasted_iota(jnp.int32, (tm, NSA_KAUG - LANES), 1)
    ks_ref[:, LANES:NSA_KAUG] = jnp.where((r_i // SLC_BLOCK) % SEL_BLOCKS == c_i, 1.0, 0.0).astype(BF16)
    kw_ref[...] = (mm(_C_KW, LANES) * cos + mm(_C_KWR, LANES) * sin).astype(BF16)
    cmp_ref[...] = mm(_C_CMP, 2 * NSA_KV)
    gqkv_ref[...] = mm(_C_GQKV, 3 * GDN_W)
    z_ref[...] = mm(_C_Z, GDN_W)
    sm_ref[...] = mm(_C_SM, LANES)


def _rot_cols(w):
    k, n = w.shape
    half = NSA_HEAD_DIM // 2
    w4 = w.reshape(k, n // NSA_HEAD_DIM, 2, half)
    return jnp.stack([-w4[:, :, 1], w4[:, :, 0]], axis=2).reshape(k, n)


def _rope_tables(pos, reps):
    half = NSA_HEAD_DIM // 2
    inv = jnp.power(ROPE_THETA, -jnp.arange(half, dtype=F32) / half)
    ang = pos.astype(F32)[:, None] * inv[None, :]
    cos = jnp.tile(jnp.cos(ang), (1, 2 * reps))
    sin = jnp.tile(jnp.sin(ang), (1, 2 * reps))
    return cos, sin


def _in_projection(x2d, w_in, S):
    T = x2d.shape[0]
    o = 0
    parts = {}
    for name, n in (("q", NSA_Q), ("kc", NSA_KV), ("vc", NSA_KV), ("ks", NSA_KV), ("vs", NSA_KV),
                    ("kw", NSA_KV), ("vw", NSA_KV), ("gl", 3 * NSA_HEADS), ("gq", GDN_W), ("gk", GDN_W),
                    ("gv", GDN_W), ("gz", GDN_W), ("gb", GDN_HEADS), ("ga", GDN_HEADS)):
        parts[name] = w_in[:, o:o + n]
        o += n
    pad = jnp.zeros((D_MODEL, LANES - 3 * NSA_HEADS - 2 * GDN_HEADS), F32)
    w_t = jnp.concatenate([parts["q"], _rot_cols(parts["q"]), parts["vs"], parts["vw"]], axis=1).T.astype(BF16)
    w_n = jnp.concatenate([
        parts["ks"], _rot_cols(parts["ks"]), parts["kw"], _rot_cols(parts["kw"]), parts["kc"], parts["vc"],
        parts["gq"], parts["gk"], parts["gv"], parts["gz"], parts["gl"], parts["gb"], parts["ga"], pad,
    ], axis=1).astype(BF16)
    assert w_t.shape[0] == _R_END and w_n.shape[1] == _C_END
    cos, sin = _rope_tables(jnp.arange(S), LANES // NSA_HEAD_DIM)
    cos_t, sin_t = cos[:, :NSA_HEAD_DIM].T, sin[:, :NSA_HEAD_DIM].T
    tm = min(IN_TM, S)
    nblk = S // tm
    row = lambda i: (i, 0)
    col3 = lambda i: (0, 0, i)
    const = lambda i: (0, 0)
    return pl.pallas_call(
        _inproj_kernel,
        grid=(T // tm,),
        in_specs=[
            pl.BlockSpec((tm, D_MODEL), row),
            pl.BlockSpec((_R_END, D_MODEL), const), pl.BlockSpec((D_MODEL, _C_END), const),
            pl.BlockSpec((tm, LANES), lambda i: (i % nblk, 0)),
            pl.BlockSpec((tm, LANES), lambda i: (i % nblk, 0)),
            pl.BlockSpec((NSA_HEAD_DIM, tm), lambda i: (0, i % nblk)),
            pl.BlockSpec((NSA_HEAD_DIM, tm), lambda i: (0, i % nblk)),
        ],
        out_specs=[
            pl.BlockSpec((NSA_HEADS, NSA_HEAD_DIM, tm), col3),
            pl.BlockSpec((NSA_KV_HEADS, 2 * NSA_HEAD_DIM, tm), col3),
            pl.BlockSpec((NSA_KV_HEADS, 2 * NSA_HEAD_DIM, tm), col3),
            pl.BlockSpec((tm, NSA_KAUG), row), pl.BlockSpec((tm, NSA_KV), row),
            pl.BlockSpec((tm, 2 * NSA_KV), row), pl.BlockSpec((tm, 3 * GDN_W), row),
            pl.BlockSpec((tm, GDN_W), row), pl.BlockSpec((tm, LANES), row),
        ],
        out_shape=[
            jax.ShapeDtypeStruct((NSA_HEADS, NSA_HEAD_DIM, T), BF16),
            jax.ShapeDtypeStruct((NSA_KV_HEADS, 2 * NSA_HEAD_DIM, T), BF16),
            jax.ShapeDtypeStruct((NSA_KV_HEADS, 2 * NSA_HEAD_DIM, T), BF16),
            jax.ShapeDtypeStruct((T, NSA_KAUG), BF16), jax.ShapeDtypeStruct((T, NSA_KV), BF16),
            jax.ShapeDtypeStruct((T, 2 * NSA_KV), F32), jax.ShapeDtypeStruct((T, 3 * GDN_W), F32),
            jax.ShapeDtypeStruct((T, GDN_W), F32), jax.ShapeDtypeStruct((T, LANES), F32),
        ],
        compiler_params=_cparams("parallel"),
        name="in_projection",
    )(x2d, w_t, w_n, cos, sin, cos_t, sin_t)


def _compress_kernel(k_ref, v_ref, wk_ref, wv_ref, pe_ref, cos_ref, sin_ref, kc_ref, vc_ref):
    nb = k_ref.shape[2]
    k = k_ref[0, 0]
    v = v_ref[0, 0]
    klo = (k + pe_ref[0:1, :]).astype(BF16)
    khi = (k + pe_ref[1:2, :]).astype(BF16)
    vlo = (v + pe_ref[2:3, :]).astype(BF16)
    vhi = (v + pe_ref[3:4, :]).astype(BF16)

    def up(m):
        return pltpu.roll(m, nb - 1, axis=0)

    a = _dot(klo, wk_ref[0]) + up(_dot(khi, wk_ref[1]))
    ar = _dot(klo, wk_ref[2]) + up(_dot(khi, wk_ref[3]))
    kc_ref[0] = a * cos_ref[...] + ar * sin_ref[...]
    vc_ref[0] = _dot(vlo, wv_ref[0]) + up(_dot(vhi, wv_ref[1]))


def _compress(cmp2d, pe_k, w_k, pe_v, w_v, B, S):
    nb = S // CMP_STRIDE
    bh = B * NSA_KV_HEADS
    row_w = CMP_STRIDE * NSA_HEAD_DIM
    c = cmp2d.reshape(B, S, 2, NSA_KV_HEADS, NSA_HEAD_DIM).transpose(2, 0, 3, 1, 4)
    c = c.reshape(2, bh, nb, row_w)
    wk = w_k.reshape(2, row_w, NSA_HEAD_DIM)
    wk_all = jnp.concatenate([wk, jnp.stack([_rot_cols(wk[0]), _rot_cols(wk[1])])], 0).astype(BF16)
    wv_all = w_v.reshape(2, row_w, NSA_HEAD_DIM).astype(BF16)
    pe = jnp.concatenate([pe_k.reshape(2, row_w), pe_v.reshape(2, row_w)], 0)
    cos, sin = _rope_tables(jnp.arange(nb) * CMP_STRIDE + (CMP_BLOCK - 1) / 2, 1)
    full = lambda shape: pl.BlockSpec(shape, lambda i: (0,) * len(shape))
    kc, vc = pl.pallas_call(
        _compress_kernel,
        grid=(bh,),
        in_specs=[
            pl.BlockSpec((1, 1, nb, row_w), lambda i: (0, i, 0, 0)),
            pl.BlockSpec((1, 1, nb, row_w), lambda i: (1, i, 0, 0)),
            full((4, row_w, NSA_HEAD_DIM)), full((2, row_w, NSA_HEAD_DIM)), full((4, row_w)),
            full((nb, NSA_HEAD_DIM)), full((nb, NSA_HEAD_DIM)),
        ],
        out_specs=[pl.BlockSpec((1, nb, NSA_HEAD_DIM), lambda i: (i, 0, 0))] * 2,
        out_shape=[jax.ShapeDtypeStruct((bh, nb, NSA_HEAD_DIM), F32)] * 2,
        compiler_params=_cparams("parallel"),
        name="nsa_compress",
    )(c, c, wk_all, wv_all, pe, cos, sin)
    return kc, vc


NSA_QB = 512
SEL_KT = 1024
SEL_SUB = 512
SEL_BLOCKS = SEL_SUB // SLC_BLOCK
NSA_KAUG = 2 * LANES


def _nsa_kernel(qt_ref, kc_ref, vct_ref, ovt_ref, ksa_ref, vst_ref, kw_ref, vwt_ref, gate_ref, o_ref,
                sel_scr, s_a, s_b, p_a, p_b, acc_a, acc_b, *, n_sel, kt_w, win_w):
    G, QB, D = NSA_GROUP, NSA_QB, NSA_HEAD_DIM
    GQ = G * QB
    nb = kc_ref.shape[2]
    ns = ovt_ref.shape[0]
    h = pl.program_id(1)
    qb = pl.program_id(2)
    s0 = qb * QB
    q4 = jnp.concatenate([qt_ref[g] for g in range(G)], axis=1)
    zeros = jnp.zeros_like(q4)
    qpad = jnp.concatenate([jnp.where(h == hh, q4, zeros) for hh in range(NSA_KV_HEADS)], axis=0)
    tq = s0 + lax.broadcasted_iota(jnp.int32, (1, QB), 1)
    lanes4 = lambda a: jnp.concatenate([a] * G, axis=1)

    s_c = _dot(kc_ref[0, 0], q4)
    nrow = lax.broadcasted_iota(jnp.int32, (nb, QB), 0)
    mask_c = lanes4((nrow * CMP_STRIDE + (CMP_BLOCK - 1)) <= tq)
    s_c = jnp.where(mask_c, s_c, NEG)
    m_c = jnp.max(s_c, 0, keepdims=True)
    e_c = jnp.where(mask_c, jnp.exp2(s_c - m_c), 0.0)
    l_c = jnp.sum(e_c, 0, keepdims=True)
    p_c = e_c / jnp.where(l_c > 0, l_c, 1.0)
    o_c = _dot(vct_ref[0, 0], p_c.astype(BF16))

    psum = p_c[:, 0:QB]
    for g in range(1, G):
        psum = psum + p_c[:, g * QB:(g + 1) * QB]
    p_hi = psum.astype(BF16)
    p_lo = (psum - p_hi.astype(F32)).astype(BF16)
    imp_t = _dot(ovt_ref[...], p_hi) + _dot(ovt_ref[...], p_lo)

    w0 = pl.multiple_of(jnp.maximum(s0 + QB - win_w, 0), QB)
    s_w = _dot(kw_ref[0, pl.ds(w0, win_w), :], qpad)
    rel = tq - (w0 + lax.broadcasted_iota(jnp.int32, (win_w, QB), 0))
    s_w = s_w + lanes4(jnp.where((rel >= 0) & (rel < WINDOW), 0.0, NEG))
    p_w = jnp.exp2((s_w - jnp.max(s_w, 0, keepdims=True)).astype(BF16))
    acc_w = _dot(vwt_ref[0, :, pl.ds(w0, win_w)], p_w)
    o_w = acc_w[:D] / acc_w[D:D + 1]

    jrow = lax.broadcasted_iota(jnp.int32, (ns, QB), 0)
    tcol = s0 + lax.broadcasted_iota(jnp.int32, (ns, QB), 1)
    cur = tcol // SLC_BLOCK
    forced = (jrow == 0) | ((jrow <= cur) & (jrow > cur - SLC_LOCAL))
    score = jnp.where((jrow * SLC_BLOCK) <= tcol, jnp.where(forced, FORCED_SCORE, imp_t), -1.0)
    jf = jrow.astype(F32)
    sel_t = jnp.zeros((ns, QB), F32)
    for _ in range(n_sel):
        mx = jnp.max(score, 0, keepdims=True)
        first = jnp.min(jnp.where(score == mx, jf, float(ns)), 0, keepdims=True)
        pick = jf == first
        sel_t = jnp.where(pick, 1.0, sel_t)
        score = jnp.where(pick, -2.0, score)
    sel_scr[...] = sel_t

    krow = lax.broadcasted_iota(jnp.int32, (SEL_SUB, QB), 0)
    n_kt = (s0 + QB + kt_w - 1) // kt_w
    pad_rows = ksa_ref.shape[2] - qpad.shape[0] - 2 * SEL_BLOCKS

    def q_with_mask(k0):
        chosen = sel_scr[pl.ds(pl.multiple_of(k0 // SLC_BLOCK, SEL_BLOCKS), SEL_BLOCKS), :]
        bias = jnp.concatenate([jnp.where(chosen > 0.5, 0.0, NEG), jnp.zeros((SEL_BLOCKS, QB), F32)], axis=0)
        return jnp.concatenate([qpad, lanes4(bias).astype(BF16), jnp.zeros((pad_rows, GQ), BF16)], axis=0)

    def score_into(k0, s_ref):
        s_ref[...] = _dot(ksa_ref[0, pl.ds(k0, SEL_SUB), :], q_with_mask(k0))

    def softmax_into(s_ref, p_ref, k0, m, causal):
        if causal:
            bias = lanes4(jnp.where((k0 + krow) <= tq, 0.0, NEG))
            s_ref[...] = s_ref[...] + bias
        m_new = jnp.maximum(m, jnp.max(s_ref[...], 0, keepdims=True))
        p_ref[...] = jnp.exp2((s_ref[...] - m_new).astype(BF16))
        return jnp.exp2(m - m_new), m_new

    def values_into(k0, p_ref, acc_ref, alpha):
        acc_ref[...] = alpha * acc_ref[...] + _dot(vst_ref[0, :, pl.ds(k0, SEL_SUB)], p_ref[...])

    def sel_trip(t, carry, last):
        m_a, m_b, alpha_b = carry
        k_a = pl.multiple_of(t * kt_w, kt_w)
        k_b = pl.multiple_of(k_a + SEL_SUB, SEL_SUB)
        score_into(k_b, s_b)
        alpha_a, m_a = softmax_into(s_a, p_a, k_a, m_a, last)
        values_into(pl.multiple_of(jnp.maximum(k_b - kt_w, SEL_SUB), SEL_SUB), p_b, acc_b, alpha_b)
        if not last:
            score_into(pl.multiple_of(k_a + kt_w, kt_w), s_a)
        alpha_b, m_b = softmax_into(s_b, p_b, k_b, m_b, last)
        values_into(k_a, p_a, acc_a, alpha_a)
        return m_a, m_b, alpha_b

    acc_a[...] = jnp.zeros_like(acc_a)
    acc_b[...] = jnp.zeros_like(acc_b)
    p_b[...] = jnp.zeros_like(p_b)
    score_into(0, s_a)
    neg_row = jnp.full((1, GQ), NEG, F32)
    carry = lax.fori_loop(0, n_kt - 1, functools.partial(sel_trip, last=False),
                          (neg_row, neg_row, jnp.ones((1, GQ), F32)))
    m_a, m_b, alpha_b = sel_trip(n_kt - 1, carry, True)
    values_into(pl.multiple_of((n_kt - 1) * kt_w + SEL_SUB, SEL_SUB), p_b, acc_b, alpha_b)
    m_s = jnp.maximum(m_a, m_b)
    acc_s = jnp.exp2(m_a - m_s) * acc_a[...] + jnp.exp2(m_b - m_s) * acc_b[...]
    o_s = acc_s[:D] / acc_s[D:D + 1]

    gate = _sigmoid(gate_ref[0, 0])
    for g in range(G):
        cols = slice(g * QB, (g + 1) * QB)
        o_ref[g] = (gate[3 * g:3 * g + 1] * o_c[:, cols] + gate[3 * g + 1:3 * g + 2] * o_s[:, cols]
                    + gate[3 * g + 2:3 * g + 3] * o_w[:, cols]).astype(BF16)


def _nsa_attention(qt, ks, vst, kw, vwt, kc, vc, small, B, S):
    H, HKV, G, D = NSA_HEADS, NSA_KV_HEADS, NSA_GROUP, NSA_HEAD_DIM
    nb = S // CMP_STRIDE
    ns = S // SLC_BLOCK
    nq = S // NSA_QB
    n_sel = min(SLC_TOP_N, ns)
    kt_w = min(SEL_KT, S)
    win_w = min(WINDOW + NSA_QB, S)
    kcb = kc.reshape(B, HKV, nb, D).astype(BF16)
    vct = vc.reshape(B, HKV, nb, D).transpose(0, 1, 3, 2).astype(BF16)
    gates = small[:, :3 * H].reshape(B, S, HKV, 3 * G).transpose(0, 2, 3, 1)
    m_i = jnp.arange(ns)[:, None]
    n_i = jnp.arange(nb)[None, :]
    overlap_t = ((n_i * CMP_STRIDE <= m_i * SLC_BLOCK + SLC_BLOCK - 1)
                 & (n_i * CMP_STRIDE + CMP_BLOCK - 1 >= m_i * SLC_BLOCK)).astype(BF16)
    per_bh = lambda shape: pl.BlockSpec((1, 1) + shape, lambda b, h, i: (b, h, 0, 0))
    keys = pl.BlockSpec((1, S, NSA_KV), lambda b, h, i: (b, 0, 0))
    keys_aug = pl.BlockSpec((1, S, NSA_KAUG), lambda b, h, i: (b, 0, 0))
    vals = pl.BlockSpec((1, 2 * D, S), lambda b, h, i: (h, 0, b))
    heads = pl.BlockSpec((G, D, NSA_QB), lambda b, h, i: (h, 0, b * nq + i))
    return pl.pallas_call(
        functools.partial(_nsa_kernel, n_sel=n_sel, kt_w=kt_w, win_w=win_w),
        grid=(B, HKV, nq),
        in_specs=[
            heads, per_bh((nb, D)), per_bh((D, nb)),
            pl.BlockSpec((ns, nb), lambda b, h, i: (0, 0)),
            keys_aug, vals, keys, vals,
            pl.BlockSpec((1, 1, 3 * G, NSA_QB), lambda b, h, i: (b, h, 0, i)),
        ],
        out_specs=heads,
        out_shape=jax.ShapeDtypeStruct((H, D, B * S), BF16),
        scratch_shapes=[pltpu.VMEM((ns, NSA_QB), F32)]
        + [pltpu.VMEM((SEL_SUB, G * NSA_QB), F32)] * 2 + [pltpu.VMEM((SEL_SUB, G * NSA_QB), BF16)] * 2
        + [pltpu.VMEM((2 * D, G * NSA_QB), F32)] * 2,
        compiler_params=_cparams("parallel", "parallel", "arbitrary"),
        name="nsa_attention",
    )(qt, kcb, vct, overlap_t, ks.reshape(B, S, NSA_KAUG), vst, kw.reshape(B, S, NSA_KV), vwt, gates)


GDN_TB = 512
GDN_HALO = 8


def _gdn_prep_kernel(qkv_ref, halo_ref, sm_ref, cw_ref, alog_ref, dtb_ref,
                     u_ref, wq_ref, kd_ref, intra_ref, dec_ref, xs_scr, y_scr, bg_scr):
    C, HD, NH, BD = GDN_CHUNK, GDN_HEAD_DIM, GDN_HEADS, GDN_BD
    tb = qkv_ref.shape[1]
    first = pl.program_id(1) == 0
    xs_scr[GDN_HALO:GDN_HALO + tb, :] = qkv_ref[0]
    xs_scr[0:GDN_HALO, :] = jnp.where(first, 0.0, halo_ref[0])
    for cb in range(3 * GDN_W // LANES):
        cols = slice(cb * LANES, (cb + 1) * LANES)
        y = cw_ref[GDN_CONV - 1:GDN_CONV, cols] * xs_scr[GDN_HALO:GDN_HALO + tb, cols]
        for d in range(1, GDN_CONV):
            y = y + cw_ref[GDN_CONV - 1 - d:GDN_CONV - d, cols] * xs_scr[GDN_HALO - d:GDN_HALO - d + tb, cols]
        y_scr[:, cols] = _silu(y)
    sm = sm_ref[0]
    b_logit = sm[:, 3 * NSA_HEADS:3 * NSA_HEADS + NH]
    a_logit = sm[:, 3 * NSA_HEADS + NH:3 * NSA_HEADS + 2 * NH] + dtb_ref[...]
    softplus = jnp.maximum(a_logit, 0.0) + jnp.log(1.0 + jnp.exp(-jnp.abs(a_logit)))
    bg_scr[:, 0:NH] = _sigmoid(b_logit)
    bg_scr[:, NH:2 * NH] = -jnp.exp(alog_ref[...]) * softplus

    r = lax.broadcasted_iota(jnp.int32, (BD, BD), 0)
    c = lax.broadcasted_iota(jnp.int32, (BD, BD), 1)
    same = (r // C) == (c // C)
    eye = r == c
    tril = same & (r >= c)
    triu = same & (r <= c)
    strict = same & (r > c)
    eye_f = jnp.where(eye, 1.0, 0.0)

    def stack_heads(m, c0):
        return jnp.concatenate([m[:, c0 + h * HD:c0 + (h + 1) * HD] for h in range(NH)], axis=0)

    def chunk(ci, carry):
        r0 = pl.multiple_of(ci * C, C)
        yc = y_scr[pl.ds(r0, C), :]
        bg = bg_scr[pl.ds(r0, C), :]
        q4 = stack_heads(yc, 0)
        k4 = stack_heads(yc, GDN_W)
        v4 = stack_heads(yc, 2 * GDN_W)
        q4 = q4 * lax.rsqrt(jnp.sum(q4 * q4, -1, keepdims=True) + NORM_EPS) * (HD ** -0.5)
        k4 = k4 * lax.rsqrt(jnp.sum(k4 * k4, -1, keepdims=True) + NORM_EPS)
        b4 = jnp.concatenate([bg[:, h:h + 1] for h in range(NH)], axis=0)
        g4 = jnp.concatenate([bg[:, NH + h:NH + h + 1] for h in range(NH)], axis=0)
        g_row = jnp.sum(jnp.where(eye, g4, 0.0), axis=0, keepdims=True)
        gc_col = jnp.sum(jnp.where(tril, g_row, 0.0), axis=1, keepdims=True)
        gc_row = jnp.sum(jnp.where(triu, g4, 0.0), axis=0, keepdims=True)
        gl_col = jnp.sum(jnp.where(same, g_row, 0.0), axis=1, keepdims=True)
        decay = jnp.where(tril, jnp.exp(jnp.minimum(gc_col - gc_row, 0.0)), 0.0)
        kb4 = k4 * b4
        k4b = k4.astype(BF16)
        lmat = jnp.where(strict, _dot_nt(kb4.astype(BF16), k4b) * decay, 0.0)
        intra = jnp.where(tril, _dot_nt(q4.astype(BF16), k4b) * decay, 0.0)
        inv = eye_f - lmat
        pw = lmat
        for _ in range(int(math.log2(C)) - 1):
            pw = _dot2(pw, pw)
            inv = _dot2(inv, eye_f + pw)
        egc = jnp.exp(gc_col)
        uw = _dot2(inv, jnp.concatenate([v4 * b4, kb4 * egc], axis=1))
        u_ref[0, ci] = uw[:, :HD]
        w4 = uw[:, HD:].astype(BF16).reshape(NH, C, HD)
        qg4 = (q4 * egc).astype(BF16).reshape(NH, C, HD)
        wq_ref[0, ci] = jnp.concatenate([w4, qg4], axis=1)
        kd_ref[0, ci] = (k4 * jnp.exp(gl_col - gc_col)).astype(BF16)
        intra_ref[0, ci] = intra.astype(BF16)
        dec = jnp.broadcast_to(jnp.exp(gl_col), (BD, LANES)).reshape(NH, C, LANES)
        dec_ref[0, ci] = dec[:, 0:8, :]
        return carry

    for ci in range(tb // C):
        chunk(ci, 0)


def _gdn_scan_kernel(u_ref, wq_ref, kd_ref, intra_ref, dec_ref, z_ref, nw_ref, o_ref, s_scr):
    C, HD, NH = GDN_CHUNK, GDN_HEAD_DIM, GDN_HEADS
    nbatch = u_ref.shape[0]

    @pl.when(pl.program_id(0) == 0)
    def _():
        s_scr[...] = jnp.zeros_like(s_scr)

    nw = nw_ref[...]
    for b in range(nbatch):
        vnew, qs = [], []
        for h in range(NH):
            st = s_scr[b * NH + h]
            res = _dot(wq_ref[b, 0, h], st.astype(BF16))
            vnew.append(u_ref[b, 0, h * C:(h + 1) * C, :] - res[:C])
            qs.append(res[C:])
        vnew4 = jnp.concatenate(vnew, axis=0).astype(BF16)
        o4 = jnp.concatenate(qs, axis=0) + _dot(intra_ref[b, 0], vnew4)
        for h in range(NH):
            rows = slice(h * C, (h + 1) * C)
            st = s_scr[b * NH + h]
            s_scr[b * NH + h] = st * dec_ref[b, 0, h, 0:1, :] + _dot_tn(kd_ref[b, 0, rows, :], vnew4[rows])
            oh = o4[rows]
            oh = oh * lax.rsqrt(jnp.mean(oh * oh, -1, keepdims=True) + NORM_EPS) * nw
            o_ref[b, :, h * HD:(h + 1) * HD] = oh * _silu(z_ref[b, :, h * HD:(h + 1) * HD])


def _gdn_mixer(gqkv, z, small, conv_w, a_log, dt_bias, norm_w, B, S):
    C, HD, NH, BD = GDN_CHUNK, GDN_HEAD_DIM, GDN_HEADS, GDN_BD
    nc = S // C
    tb = min(GDN_TB, S)
    cb = tb // C
    qkv3 = gqkv.reshape(B, S, 3 * GDN_W)
    sm3 = small.reshape(B, S, LANES)
    hb = tb // GDN_HALO
    u, wq, kd, intra, dec = pl.pallas_call(
        _gdn_prep_kernel,
        grid=(B, S // tb),
        in_specs=[
            pl.BlockSpec((1, tb, 3 * GDN_W), lambda b, i: (b, i, 0)),
            pl.BlockSpec((1, GDN_HALO, 3 * GDN_W), lambda b, i: (b, jnp.maximum(i * hb - 1, 0), 0)),
            pl.BlockSpec((1, tb, LANES), lambda b, i: (b, i, 0)),
            pl.BlockSpec((GDN_CONV, 3 * GDN_W), lambda b, i: (0, 0)),
            pl.BlockSpec((1, NH), lambda b, i: (0, 0)),
            pl.BlockSpec((1, NH), lambda b, i: (0, 0)),
        ],
        out_specs=[
            pl.BlockSpec((1, cb, BD, HD), lambda b, i: (b, i, 0, 0)),
            pl.BlockSpec((1, cb, NH, 2 * C, HD), lambda b, i: (b, i, 0, 0, 0)),
            pl.BlockSpec((1, cb, BD, HD), lambda b, i: (b, i, 0, 0)),
            pl.BlockSpec((1, cb, BD, BD), lambda b, i: (b, i, 0, 0)),
            pl.BlockSpec((1, cb, NH, 8, LANES), lambda b, i: (b, i, 0, 0, 0)),
        ],
        out_shape=[
            jax.ShapeDtypeStruct((B, nc, BD, HD), F32),
            jax.ShapeDtypeStruct((B, nc, NH, 2 * C, HD), BF16),
            jax.ShapeDtypeStruct((B, nc, BD, HD), BF16),
            jax.ShapeDtypeStruct((B, nc, BD, BD), BF16),
            jax.ShapeDtypeStruct((B, nc, NH, 8, LANES), F32),
        ],
        scratch_shapes=[
            pltpu.VMEM((GDN_HALO + tb, 3 * GDN_W), F32),
            pltpu.VMEM((tb, 3 * GDN_W), F32),
            pltpu.VMEM((tb, 2 * NH), F32),
        ],
        compiler_params=_cparams("parallel", "parallel"),
        name="gdn_prepare",
    )(qkv3, qkv3, sm3, conv_w, a_log.reshape(1, NH), dt_bias.reshape(1, NH))

    o = pl.pallas_call(
        _gdn_scan_kernel,
        grid=(nc,),
        in_specs=[
            pl.BlockSpec((B, 1, BD, HD), lambda c: (0, c, 0, 0)),
            pl.BlockSpec((B, 1, NH, 2 * C, HD), lambda c: (0, c, 0, 0, 0)),
            pl.BlockSpec((B, 1, BD, HD), lambda c: (0, c, 0, 0)),
            pl.BlockSpec((B, 1, BD, BD), lambda c: (0, c, 0, 0)),
            pl.BlockSpec((B, 1, NH, 8, LANES), lambda c: (0, c, 0, 0, 0)),
            pl.BlockSpec((B, C, GDN_W), lambda c: (0, c, 0)),
            pl.BlockSpec((1, HD), lambda c: (0, 0)),
        ],
        out_specs=pl.BlockSpec((B, C, GDN_W), lambda c: (0, c, 0)),
        out_shape=jax.ShapeDtypeStruct((B, S, GDN_W), F32),
        scratch_shapes=[pltpu.VMEM((B * NH, HD, HD), F32)],
        compiler_params=_cparams("arbitrary"),
        name="gdn_scan",
    )(u, wq, kd, intra, dec, z.reshape(B, S, GDN_W), norm_w.reshape(1, HD))
    return o.reshape(B * S, GDN_W)


MIX_TM = 512


def _mix_ln_kernel(x_ref, at_ref, b_ref, w_ref, g_ref, be_ref, o_ref):
    a_t = at_ref[...].reshape(NSA_Q, x_ref.shape[0])
    mix = (_dot_tn(a_t, w_ref[0:NSA_Q, :])
           + _dot(b_ref[...].astype(BF16), w_ref[NSA_Q:NSA_Q + GDN_W, :]))
    o_ref[...] = _layer_norm(DN_ALPHA * x_ref[...] + mix, g_ref[...], be_ref[...])


def _mix_out_ln(x2d, o_at, o_b, w_out, g, b):
    T = x2d.shape[0]
    row = lambda i: (i, 0)
    const = lambda i: (0, 0)
    return pl.pallas_call(
        _mix_ln_kernel,
        grid=(T // MIX_TM,),
        in_specs=[
            pl.BlockSpec((MIX_TM, D_MODEL), row),
            pl.BlockSpec((NSA_HEADS, NSA_HEAD_DIM, MIX_TM), lambda i: (0, 0, i)),
            pl.BlockSpec((MIX_TM, GDN_W), row), pl.BlockSpec((NSA_Q + GDN_W, D_MODEL), const),
            pl.BlockSpec((1, D_MODEL), const), pl.BlockSpec((1, D_MODEL), const),
        ],
        out_specs=pl.BlockSpec((MIX_TM, D_MODEL), row),
        out_shape=jax.ShapeDtypeStruct((T, D_MODEL), F32),
        compiler_params=_cparams("parallel"),
        name="mix_out_ln",
    )(x2d, o_at, o_b, w_out.astype(BF16), g.reshape(1, D_MODEL), b.reshape(1, D_MODEL))


FFN_TM = 512
FFN_TF = 256


def _ffn_ln_kernel(x_ref, wg_ref, wu_ref, wd_ref, g_ref, be_ref, o_ref, hid_scr):
    x = x_ref[...]
    xb = x.astype(BF16)
    for j in range(hid_scr.shape[1] // FFN_TF):
        cols = slice(j * FFN_TF, (j + 1) * FFN_TF)
        hid_scr[:, cols] = (_silu(_dot(xb, wg_ref[:, cols])) * _dot(xb, wu_ref[:, cols])).astype(BF16)
    o_ref[...] = _layer_norm(DN_ALPHA * x + _dot(hid_scr[...], wd_ref[...]), g_ref[...], be_ref[...])


def _ffn_ln(x2d, wg, wu, wd, g, b):
    T = x2d.shape[0]
    dff = wg.shape[1]
    tm = min(FFN_TM, T)
    const = lambda i: (0, 0)
    resident = lambda shape: pl.BlockSpec(shape, const, pipeline_mode=pl.Buffered(1))
    return pl.pallas_call(
        _ffn_ln_kernel,
        grid=(T // tm,),
        in_specs=[
            pl.BlockSpec((tm, D_MODEL), lambda i: (i, 0)),
            resident((D_MODEL, dff)), resident((D_MODEL, dff)), resident((dff, D_MODEL)),
            pl.BlockSpec((1, D_MODEL), const), pl.BlockSpec((1, D_MODEL), const),
        ],
        out_specs=pl.BlockSpec((tm, D_MODEL), lambda i: (i, 0)),
        out_shape=jax.ShapeDtypeStruct((T, D_MODEL), F32),
        scratch_shapes=[pltpu.VMEM((tm, dff), BF16)],
        compiler_params=_cparams("parallel"),
        name="ffn_ln",
    )(x2d, wg.astype(BF16), wu.astype(BF16), wd.astype(BF16), g.reshape(1, D_MODEL), b.reshape(1, D_MODEL))


POOL_TM = 512
POOL_HALO = 16


def _route_top2(x1, rw):
    lane = lax.broadcasted_iota(jnp.int32, (x1.shape[0], LANES), 1)
    lanef = lane.astype(F32)
    logits = jnp.where(lane < N_EXPERTS, _dot3(x1, rw), NEG)
    ex = jnp.exp(logits - jnp.max(logits, -1, keepdims=True))
    probs = jnp.where(lane < N_EXPERTS, ex / jnp.sum(ex, -1, keepdims=True), -1.0)
    p1 = jnp.max(probs, -1, keepdims=True)
    i1 = jnp.min(jnp.where(probs == p1, lanef, float(LANES)), -1, keepdims=True)
    rest = jnp.where(lanef == i1, -1.0, probs)
    p2 = jnp.max(rest, -1, keepdims=True)
    i2 = jnp.min(jnp.where(rest == p2, lanef, float(LANES)), -1, keepdims=True)
    den = p1 + p2
    return jnp.where(lane == 0, i1, jnp.where(lane == 1, i2, jnp.where(
        lane == 2, p1 / den, jnp.where(lane == 3, p2 / den, 0.0))))


def _pool_ln_kernel(x_ref, halo_ref, pw_ref, ps_ref, g_ref, be_ref, rw_ref, o_ref, route_ref, xs_scr):
    tm = x_ref.shape[1]
    i = pl.program_id(1)
    x = x_ref[0]
    xs_scr[POOL_HALO:POOL_HALO + tm, :] = x
    xs_scr[0:POOL_HALO, :] = jnp.where(i == 0, 0.0, halo_ref[0])
    t1 = (i * tm + 1 + lax.broadcasted_iota(jnp.int32, (tm, 1), 0)).astype(F32)
    for gi, win in enumerate(POOL_SIZES):
        cols = slice(gi * POOL_GROUP, (gi + 1) * POOL_GROUP)
        acc = x[:, cols]
        for d in range(1, win):
            acc = acc + xs_scr[POOL_HALO - d:POOL_HALO - d + tm, cols]
        mean = acc / jnp.minimum(t1, float(win))
        mix = _dot((mean - x[:, cols]).astype(BF16), pw_ref[gi]) * ps_ref[:, cols]
        xs_scr[POOL_HALO:POOL_HALO + tm, cols] = DN_ALPHA * x[:, cols] + mix
    x1 = _layer_norm(xs_scr[POOL_HALO:POOL_HALO + tm, :], g_ref[...], be_ref[...])
    o_ref[0] = x1
    route_ref[0] = _route_top2(x1, rw_ref[...])


def _pool_ln(x3d, pool_w, pool_scale, g, b, router_w):
    B, S, _ = x3d.shape
    tm = min(POOL_TM, S)
    hb = tm // POOL_HALO
    const = lambda bb, i: (0, 0)
    rw = jnp.concatenate([router_w, jnp.zeros((D_MODEL, LANES - N_EXPERTS), F32)], axis=1)
    return pl.pallas_call(
        _pool_ln_kernel,
        grid=(B, S // tm),
        in_specs=[
            pl.BlockSpec((1, tm, D_MODEL), lambda bb, i: (bb, i, 0)),
            pl.BlockSpec((1, POOL_HALO, D_MODEL), lambda bb, i: (bb, jnp.maximum(i * hb - 1, 0), 0)),
            pl.BlockSpec((len(POOL_SIZES), POOL_GROUP, POOL_GROUP), lambda bb, i: (0, 0, 0)),
            pl.BlockSpec((1, D_MODEL), const), pl.BlockSpec((1, D_MODEL), const),
            pl.BlockSpec((1, D_MODEL), const), pl.BlockSpec((D_MODEL, LANES), const),
        ],
        out_specs=[pl.BlockSpec((1, tm, D_MODEL), lambda bb, i: (bb, i, 0)),
                   pl.BlockSpec((1, tm, LANES), lambda bb, i: (bb, i, 0))],
        out_shape=[jax.ShapeDtypeStruct((B, S, D_MODEL), F32), jax.ShapeDtypeStruct((B, S, LANES), F32)],
        scratch_shapes=[pltpu.VMEM((POOL_HALO + tm, D_MODEL), F32)],
        compiler_params=_cparams("parallel", "parallel"),
        name="pool_ln",
    )(x3d, x3d, pool_w.astype(BF16), pool_scale.reshape(1, D_MODEL), g.reshape(1, D_MODEL),
      b.reshape(1, D_MODEL), rw)


MOE_TM = 1024
MOE_TF = 512
MOE_TC = 512


def _moe_plan(route, T, tm):
    ne = N_EXPERTS
    e_flat = jnp.concatenate([route[:, 0], route[:, 1]]).astype(jnp.int32)
    onehot = (e_flat[:, None] == jnp.arange(ne)[None, :]).astype(jnp.int32)
    csum = jnp.cumsum(onehot, axis=0)
    rank = jnp.sum(onehot * csum, axis=1) - 1
    counts = csum[-1]
    padded = (counts + tm - 1) // tm * tm
    gend = jnp.cumsum(padded)
    gstart = gend - padded
    pos = gstart[e_flat] + rank
    n_tiles = (2 * T) // tm + ne
    tile_e = jnp.minimum(jnp.searchsorted(gend, jnp.arange(n_tiles) * tm, side="right"), ne - 1)
    tile_e = tile_e.astype(jnp.int32)
    n_used = (gend[-1] // tm).astype(jnp.int32).reshape(1)
    order = jnp.argsort(e_flat, stable=True)
    cstart = jnp.cumsum(counts) - counts
    e_p = jnp.repeat(tile_e, tm)
    r = jnp.clip(jnp.arange(n_tiles * tm) - gstart[e_p], 0, jnp.maximum(counts[e_p] - 1, 0))
    src = (order[jnp.clip(cstart[e_p] + r, 0, 2 * T - 1)] % T).astype(jnp.int32)
    return pos.astype(jnp.int32), src, tile_e, n_used, n_tiles


SUBLANES = 8


def _gather_rows(src_hbm, idx, dst, sem):
    def issue(blk, c):
        for u in range(SUBLANES):
            row = idx(blk * SUBLANES + u)
            pltpu.make_async_copy(src_hbm.at[row >> 3, pl.ds(row & (SUBLANES - 1), 1)],
                                  dst.at[blk, pl.ds(u, 1)], sem).start()
        return c

    lax.fori_loop(0, dst.shape[0], issue, 0)


def _gather_wait(src_hbm, dst, sem):
    pltpu.make_async_copy(src_hbm.at[pl.ds(0, dst.shape[0])], dst, sem).wait()


def _moe_expert_kernel(te_ref, nu_ref, src_ref, nxt_ref, x_hbm, wg_ref, wu_ref, wd_ref, o_ref,
                       xbuf, xb_scr, acc_scr, sem):
    i = pl.program_id(0)
    j = pl.program_id(1)
    n_used = nu_ref[0]
    used = i < n_used
    slot = i % 2

    @pl.when((i == 0) & (j == 0))
    def _():
        _gather_rows(x_hbm, lambda r: src_ref[0, 0, r], xbuf.at[0], sem.at[0])

    @pl.when((i + 1 < n_used) & (j == 0))
    def _():
        _gather_rows(x_hbm, lambda r: nxt_ref[0, 0, r], xbuf.at[1 - slot], sem.at[1 - slot])

    @pl.when(used & (j == 0))
    def _():
        _gather_wait(x_hbm, xbuf.at[slot], sem.at[slot])
        xb_scr[...] = xbuf[slot].reshape(xb_scr.shape).astype(BF16)
        acc_scr[...] = jnp.zeros_like(acc_scr)

    @pl.when(used)
    def _():
        xb = xb_scr[...]
        hid = _silu(_dot(xb, wg_ref[0])) * _dot(xb, wu_ref[0])
        acc_scr[...] += _dot(hid.astype(BF16), wd_ref[0])

    last = j == pl.num_programs(1) - 1

    @pl.when(used & last)
    def _():
        o_ref[...] = acc_scr[...]

    @pl.when(jnp.logical_not(used) & last)
    def _():
        o_ref[...] = jnp.zeros_like(o_ref)


def _moe_combine_kernel(pos_ref, nxt_ref, x_ref, route_ref, ys_hbm, g_ref, be_ref, o_ref, ybuf, sem):
    i = pl.program_id(0)
    slot = i % 2

    def fetch(p_ref, s):
        for k in range(2):
            _gather_rows(ys_hbm, lambda r: p_ref[0, k, r], ybuf.at[s, k], sem.at[s])

    @pl.when(i == 0)
    def _():
        fetch(pos_ref, 0)

    @pl.when(i + 1 < pl.num_programs(0))
    def _():
        fetch(nxt_ref, 1 - slot)

    for k in range(2):
        _gather_wait(ys_hbm, ybuf.at[slot, k], sem.at[slot])
    route = route_ref[...]
    y0 = ybuf[slot, 0].reshape(x_ref.shape)
    y1 = ybuf[slot, 1].reshape(x_ref.shape)
    y = DN_ALPHA * x_ref[...] + route[:, 2:3] * y0 + route[:, 3:4] * y1
    o_ref[...] = _layer_norm(y, g_ref[...], be_ref[...])


def _moe_ln(x2d, route, wg, wu, wd, g, b):
    T = x2d.shape[0]
    ne, _, dff = wg.shape
    tm = min(MOE_TM, T)
    tc = min(MOE_TC, T)
    pos, src, tile_e, n_used, n_tiles = _moe_plan(route, T, tm)

    nj = dff // MOE_TF

    def w_col(i, j, te, nu):
        return (te[i], 0, jnp.where(i < nu[0], j, 0))

    def w_row(i, j, te, nu):
        return (te[i], jnp.where(i < nu[0], j, 0), 0)

    src3 = src.reshape(n_tiles, 1, tm)
    ys = pl.pallas_call(
        _moe_expert_kernel,
        grid_spec=pltpu.PrefetchScalarGridSpec(
            num_scalar_prefetch=2,
            grid=(n_tiles, nj),
            in_specs=[
                pl.BlockSpec((1, 1, tm), lambda i, j, te, nu: (i, 0, 0), memory_space=pltpu.SMEM),
                pl.BlockSpec((1, 1, tm), lambda i, j, te, nu: (jnp.minimum(i + 1, n_tiles - 1), 0, 0),
                             memory_space=pltpu.SMEM),
                pl.BlockSpec(memory_space=pl.ANY),
                pl.BlockSpec((1, D_MODEL, MOE_TF), w_col),
                pl.BlockSpec((1, D_MODEL, MOE_TF), w_col),
                pl.BlockSpec((1, MOE_TF, D_MODEL), w_row),
            ],
            out_specs=pl.BlockSpec((tm, D_MODEL), lambda i, j, te, nu: (i, 0)),
            scratch_shapes=[pltpu.VMEM((2, tm // SUBLANES, SUBLANES, D_MODEL), F32),
                            pltpu.VMEM((tm, D_MODEL), BF16), pltpu.VMEM((tm, D_MODEL), F32),
                            pltpu.SemaphoreType.DMA((2,))],
        ),
        out_shape=jax.ShapeDtypeStruct((n_tiles * tm, D_MODEL), F32),
        compiler_params=_cparams("arbitrary", "arbitrary"),
        name="moe_experts",
    )(tile_e, n_used, src3, src3, x2d.reshape(T // SUBLANES, SUBLANES, D_MODEL), wg.astype(BF16),
      wu.astype(BF16), wd.astype(BF16))

    n_steps = T // tc
    pos3 = pos.reshape(2, n_steps, tc).transpose(1, 0, 2)
    row = lambda i: (i, 0)
    const = lambda i: (0, 0)
    return pl.pallas_call(
        _moe_combine_kernel,
        grid=(n_steps,),
        in_specs=[
            pl.BlockSpec((1, 2, tc), lambda i: (i, 0, 0), memory_space=pltpu.SMEM),
            pl.BlockSpec((1, 2, tc), lambda i: (jnp.minimum(i + 1, n_steps - 1), 0, 0),
                         memory_space=pltpu.SMEM),
            pl.BlockSpec((tc, D_MODEL), row), pl.BlockSpec((tc, LANES), row),
            pl.BlockSpec(memory_space=pl.ANY),
            pl.BlockSpec((1, D_MODEL), const), pl.BlockSpec((1, D_MODEL), const),
        ],
        out_specs=pl.BlockSpec((tc, D_MODEL), row),
        out_shape=jax.ShapeDtypeStruct((T, D_MODEL), F32),
        scratch_shapes=[pltpu.VMEM((2, 2, tc // SUBLANES, SUBLANES, D_MODEL), F32),
                        pltpu.SemaphoreType.DMA((2,))],
        compiler_params=_cparams("arbitrary"),
        name="moe_combine_ln",
    )(pos3, pos3, x2d, route, ys.reshape(n_tiles * tm // SUBLANES, SUBLANES, D_MODEL),
      g.reshape(1, D_MODEL), b.reshape(1, D_MODEL))


def _even_layer(x2d, B, S, w_in, pe_k, w_k, pe_v, w_v, conv_w, a_log, dt_bias, gdn_norm, w_out,
                ln1_g, ln1_b, wg, wu, wd, ln2_g, ln2_b):
    qt, vst, vwt, ks, kw, cmp2d, gqkv, z, small = _in_projection(x2d, w_in, S)
    kc, vc = _compress(cmp2d, pe_k, w_k, pe_v, w_v, B, S)
    o_a = _nsa_attention(qt, ks, vst, kw, vwt, kc, vc, small, B, S)
    o_b = _gdn_mixer(gqkv, z, small, conv_w, a_log, dt_bias, gdn_norm, B, S)
    x2d = _mix_out_ln(x2d, o_a, o_b, w_out, ln1_g, ln1_b)
    return _ffn_ln(x2d, wg, wu, wd, ln2_g, ln2_b)


def _odd_layer(x2d, B, S, pool_w, pool_scale, ln1_g, ln1_b, router_w, wg, wu, wd, ln2_g, ln2_b):
    x3d, route = _pool_ln(x2d.reshape(B, S, D_MODEL), pool_w, pool_scale, ln1_g, ln1_b, router_w)
    return _moe_ln(x3d.reshape(B * S, D_MODEL), route.reshape(B * S, LANES), wg, wu, wd, ln2_g, ln2_b)


def kernel(x, ev_w_in, ev_cmp_pe_k, ev_cmp_w_k, ev_cmp_pe_v, ev_cmp_w_v, ev_conv_w, ev_a_log, ev_dt_bias,
           ev_gdn_norm, ev_w_out, ev_ln1_g, ev_ln1_b, ev_ffn_wg, ev_ffn_wu, ev_ffn_wd, ev_ln2_g, ev_ln2_b,
           od_pool_w, od_pool_scale, od_ln1_g, od_ln1_b, od_router_w, od_exp_wg, od_exp_wu, od_exp_wd,
           od_ln2_g, od_ln2_b):
    B, S, _ = x.shape
    h = x.reshape(B * S, D_MODEL)
    for layer in range(DEPTH):
        i = layer // 2
        if layer % 2 == 0:
            h = _even_layer(h, B, S, ev_w_in[i], ev_cmp_pe_k[i], ev_cmp_w_k[i], ev_cmp_pe_v[i],
                            ev_cmp_w_v[i], ev_conv_w[i], ev_a_log[i], ev_dt_bias[i], ev_gdn_norm[i],
                            ev_w_out[i], ev_ln1_g[i], ev_ln1_b[i], ev_ffn_wg[i], ev_ffn_wu[i],
                            ev_ffn_wd[i], ev_ln2_g[i], ev_ln2_b[i])
        else:
            h = _odd_layer(h, B, S, od_pool_w[i], od_pool_scale[i], od_ln1_g[i], od_ln1_b[i],
                           od_router_w[i], od_exp_wg[i], od_exp_wu[i], od_exp_wd[i], od_ln2_g[i],
                           od_ln2_b[i])
    return h.reshape(B, S, D_MODEL)
```

```python
import functools
import math

import jax
import jax.numpy as jnp
from jax import lax
from jax.experimental import pallas as pl
from jax.experimental.pallas import tpu as pltpu

F32 = jnp.float32
BF16 = jnp.bfloat16

D_MODEL = 1024
DEPTH = 2
NSA_HEADS = 8
NSA_KV_HEADS = 2
NSA_GROUP = NSA_HEADS // NSA_KV_HEADS
NSA_HEAD_DIM = 64
CMP_BLOCK = 32
CMP_STRIDE = 16
SLC_BLOCK = 64
SLC_TOP_N = 16
SLC_LOCAL = 2
WINDOW = 512
Q_BLOCK = 128
FORCED_SCORE = 1e6
GDN_HEADS = 4
GDN_HEAD_DIM = 128
GDN_CONV = 4
GDN_CHUNK = 64
POOL_SIZES = (2, 4, 8, 16)
POOL_GROUP = D_MODEL // 4
D_FF = 2816
N_EXPERTS = 8
D_FF_EXPERT = 3584
ROPE_THETA = 10000.0
LN_EPS = 1e-5
NORM_EPS = 1e-6
DN_ALPHA = (2 * DEPTH) ** 0.25

NSA_Q = NSA_HEADS * NSA_HEAD_DIM
NSA_KV = NSA_KV_HEADS * NSA_HEAD_DIM
GDN_W = GDN_HEADS * GDN_HEAD_DIM
GDN_BD = GDN_HEADS * GDN_CHUNK

LANES = 128
NEG = -1e30
LOG2E = 1.4426950408889634
VMEM_LIMIT = 56 * 1024 * 1024


def _cparams(*sem):
    return pltpu.CompilerParams(dimension_semantics=sem, vmem_limit_bytes=VMEM_LIMIT)


def _dot(a, b):
    return jnp.dot(a, b, preferred_element_type=F32)


def _dot_nt(a, b):
    return lax.dot_general(a, b, (((1,), (1,)), ((), ())), preferred_element_type=F32)


def _dot_tn(a, b):
    return lax.dot_general(a, b, (((0,), (0,)), ((), ())), preferred_element_type=F32)


def _dot3(a, b):
    ah = a.astype(BF16)
    al = (a - ah.astype(F32)).astype(BF16)
    bh = b.astype(BF16)
    bl = (b - bh.astype(F32)).astype(BF16)
    return _dot(ah, bh) + (_dot(ah, bl) + _dot(al, bh))


def _dot2(a, b):
    ah = a.astype(BF16)
    al = (a - ah.astype(F32)).astype(BF16)
    bh = b.astype(BF16)
    return _dot(ah, bh) + _dot(al, bh)


def _layer_norm(y, g, b):
    mu = jnp.mean(y, -1, keepdims=True)
    d = y - mu
    var = jnp.mean(d * d, -1, keepdims=True)
    return d * lax.rsqrt(var + LN_EPS) * g + b


def _sigmoid(x):
    return 1.0 / (1.0 + jnp.exp(-x))


def _silu(x):
    return x * _sigmoid(x)


IN_TM = 512
_R_Q, _R_QR, _R_VS, _R_VW, _R_END = 0, 512, 1024, 1152, 1280
_C_KS, _C_KSR, _C_KW, _C_KWR, _C_CMP, _C_GQKV, _C_Z, _C_SM, _C_END = 0, 128, 256, 384, 512, 768, 2304, 2816, 2944


def _inproj_kernel(x_ref, wt_ref, wn_ref, cos_ref, sin_ref, cost_ref, sint_ref,
                   qt_ref, vst_ref, vwt_ref, ks_ref, kw_ref, cmp_ref, gqkv_ref, z_ref, sm_ref):
    D, HKV = NSA_HEAD_DIM, NSA_KV_HEADS
    tm = x_ref.shape[0]
    xb = x_ref[...].astype(BF16)

    def mt(r0, n):
        return _dot_nt(wt_ref[r0:r0 + n, :], xb)

    def mm(c0, n):
        return _dot(xb, wn_ref[:, c0:c0 + n])

    scale = D ** -0.5 * LOG2E
    cost = cost_ref[...]
    sint = sint_ref[...]
    for h0 in range(0, NSA_HEADS, NSA_GROUP):
        a = mt(_R_Q + h0 * D, NSA_GROUP * D)
        r = mt(_R_QR + h0 * D, NSA_GROUP * D)
        for g in range(NSA_GROUP):
            rows = slice(g * D, (g + 1) * D)
            qt_ref[h0 + g] = ((a[rows] * cost + r[rows] * sint) * scale).astype(BF16)
    ones_rows = jnp.where(lax.broadcasted_iota(jnp.int32, (D, tm), 0) == 0, 1.0, 0.0).astype(BF16)
    for r0, out in ((_R_VS, vst_ref), (_R_VW, vwt_ref)):
        vt = mt(r0, NSA_KV).astype(BF16)
        for h in range(HKV):
            out[h, 0:D, :] = vt[h * D:(h + 1) * D]
            out[h, D:2 * D, :] = ones_rows
    cos = cos_ref[...]
    sin = sin_ref[...]
    ks_ref[:, 0:LANES] = (mm(_C_KS, LANES) * cos + mm(_C_KSR, LANES) * sin).astype(BF16)
    r_i = lax.broadcasted_iota(jnp.int32, (tm, NSA_KAUG - LANES), 0)
    c_i = lax.broadcasted_iota(jnp.int32, (tm, NSA_KAUG - LANES), 1)
    ks_ref[:, LANES:NSA_KAUG] = jnp.where((r_i // SLC_BLOCK) % SEL_BLOCKS == c_i, 1.0, 0.0).astype(BF16)
    kw_ref[...] = (mm(_C_KW, LANES) * cos + mm(_C_KWR, LANES) * sin).astype(BF16)
    cmp_ref[...] = mm(_C_CMP, 2 * NSA_KV)
    gqkv_ref[...] = mm(_C_GQKV, 3 * GDN_W)
    z_ref[...] = mm(_C_Z, GDN_W)
    sm_ref[...] = mm(_C_SM, LANES)


def _rot_cols(w):
    k, n = w.shape
    half = NSA_HEAD_DIM // 2
    w4 = w.reshape(k, n // NSA_HEAD_DIM, 2, half)
    return jnp.stack([-w4[:, :, 1], w4[:, :, 0]], axis=2).reshape(k, n)


def _rope_tables(pos, reps):
    half = NSA_HEAD_DIM // 2
    inv = jnp.power(ROPE_THETA, -jnp.arange(half, dtype=F32) / half)
    ang = pos.astype(F32)[:, None] * inv[None, :]
    cos = jnp.tile(jnp.cos(ang), (1, 2 * reps))
    sin = jnp.tile(jnp.sin(ang), (1, 2 * reps))
    return cos, sin


def _in_projection(x2d, w_in, S):
    T = x2d.shape[0]
    o = 0
    parts = {}
    for name, n in (("q", NSA_Q), ("kc", NSA_KV), ("vc", NSA_KV), ("ks", NSA_KV), ("vs", NSA_KV),
                    ("kw", NSA_KV), ("vw", NSA_KV), ("gl", 3 * NSA_HEADS), ("gq", GDN_W), ("gk", GDN_W),
                    ("gv", GDN_W), ("gz", GDN_W), ("gb", GDN_HEADS), ("ga", GDN_HEADS)):
        parts[name] = w_in[:, o:o + n]
        o += n
    pad = jnp.zeros((D_MODEL, LANES - 3 * NSA_HEADS - 2 * GDN_HEADS), F32)
    w_t = jnp.concatenate([parts["q"], _rot_cols(parts["q"]), parts["vs"], parts["vw"]], axis=1).T.astype(BF16)
    w_n = jnp.concatenate([
        parts["ks"], _rot_cols(parts["ks"]), parts["kw"], _rot_cols(parts["kw"]), parts["kc"], parts["vc"],
        parts["gq"], parts["gk"], parts["gv"], parts["gz"], parts["gl"], parts["gb"], parts["ga"], pad,
    ], axis=1).astype(BF16)
    assert w_t.shape[0] == _R_END and w_n.shape[1] == _C_END
    cos, sin = _rope_tables(jnp.arange(S), LANES // NSA_HEAD_DIM)
    cos_t, sin_t = cos[:, :NSA_HEAD_DIM].T, sin[:, :NSA_HEAD_DIM].T
    tm = min(IN_TM, S)
    nblk = S // tm
    row = lambda i: (i, 0)
    col3 = lambda i: (0, 0, i)
    const = lambda i: (0, 0)
    return pl.pallas_call(
        _inproj_kernel,
        grid=(T // tm,),
        in_specs=[
            pl.BlockSpec((tm, D_MODEL), row),
            pl.BlockSpec((_R_END, D_MODEL), const), pl.BlockSpec((D_MODEL, _C_END), const),
            pl.BlockSpec((tm, LANES), lambda i: (i % nblk, 0)),
            pl.BlockSpec((tm, LANES), lambda i: (i % nblk, 0)),
            pl.BlockSpec((NSA_HEAD_DIM, tm), lambda i: (0, i % nblk)),
            pl.BlockSpec((NSA_HEAD_DIM, tm), lambda i: (0, i % nblk)),
        ],
        out_specs=[
            pl.BlockSpec((NSA_HEADS, NSA_HEAD_DIM, tm), col3),
            pl.BlockSpec((NSA_KV_HEADS, 2 * NSA_HEAD_DIM, tm), col3),
            pl.BlockSpec((NSA_KV_HEADS, 2 * NSA_HEAD_DIM, tm), col3),
            pl.BlockSpec((tm, NSA_KAUG), row), pl.BlockSpec((tm, NSA_KV), row),
            pl.BlockSpec((tm, 2 * NSA_KV), row), pl.BlockSpec((tm, 3 * GDN_W), row),
            pl.BlockSpec((tm, GDN_W), row), pl.BlockSpec((tm, LANES), row),
        ],
        out_shape=[
            jax.ShapeDtypeStruct((NSA_HEADS, NSA_HEAD_DIM, T), BF16),
            jax.ShapeDtypeStruct((NSA_KV_HEADS, 2 * NSA_HEAD_DIM, T), BF16),
            jax.ShapeDtypeStruct((NSA_KV_HEADS, 2 * NSA_HEAD_DIM, T), BF16),
            jax.ShapeDtypeStruct((T, NSA_KAUG), BF16), jax.ShapeDtypeStruct((T, NSA_KV), BF16),
            jax.ShapeDtypeStruct((T, 2 * NSA_KV), F32), jax.ShapeDtypeStruct((T, 3 * GDN_W), F32),
            jax.ShapeDtypeStruct((T, GDN_W), F32), jax.ShapeDtypeStruct((T, LANES), F32),
        ],
        compiler_params=_cparams("parallel"),
        name="in_projection",
    )(x2d, w_t, w_n, cos, sin, cos_t, sin_t)


def _compress_kernel(k_ref, v_ref, wk_ref, wv_ref, pe_ref, cos_ref, sin_ref, kc_ref, vc_ref):
    nb = k_ref.shape[2]
    k = k_ref[0, 0]
    v = v_ref[0, 0]
    klo = (k + pe_ref[0:1, :]).astype(BF16)
    khi = (k + pe_ref[1:2, :]).astype(BF16)
    vlo = (v + pe_ref[2:3, :]).astype(BF16)
    vhi = (v + pe_ref[3:4, :]).astype(BF16)

    def up(m):
        return pltpu.roll(m, nb - 1, axis=0)

    a = _dot(klo, wk_ref[0]) + up(_dot(khi, wk_ref[1]))
    ar = _dot(klo, wk_ref[2]) + up(_dot(khi, wk_ref[3]))
    kc_ref[0] = a * cos_ref[...] + ar * sin_ref[...]
    vc_ref[0] = _dot(vlo, wv_ref[0]) + up(_dot(vhi, wv_ref[1]))


def _compress(cmp2d, pe_k, w_k, pe_v, w_v, B, S):
    nb = S // CMP_STRIDE
    bh = B * NSA_KV_HEADS
    row_w = CMP_STRIDE * NSA_HEAD_DIM
    c = cmp2d.reshape(B, S, 2, NSA_KV_HEADS, NSA_HEAD_DIM).transpose(2, 0, 3, 1, 4)
    c = c.reshape(2, bh, nb, row_w)
    wk = w_k.reshape(2, row_w, NSA_HEAD_DIM)
    wk_all = jnp.concatenate([wk, jnp.stack([_rot_cols(wk[0]), _rot_cols(wk[1])])], 0).astype(BF16)
    wv_all = w_v.reshape(2, row_w, NSA_HEAD_DIM).astype(BF16)
    pe = jnp.concatenate([pe_k.reshape(2, row_w), pe_v.reshape(2, row_w)], 0)
    cos, sin = _rope_tables(jnp.arange(nb) * CMP_STRIDE + (CMP_BLOCK - 1) / 2, 1)
    full = lambda shape: pl.BlockSpec(shape, lambda i: (0,) * len(shape))
    kc, vc = pl.pallas_call(
        _compress_kernel,
        grid=(bh,),
        in_specs=[
            pl.BlockSpec((1, 1, nb, row_w), lambda i: (0, i, 0, 0)),
            pl.BlockSpec((1, 1, nb, row_w), lambda i: (1, i, 0, 0)),
            full((4, row_w, NSA_HEAD_DIM)), full((2, row_w, NSA_HEAD_DIM)), full((4, row_w)),
            full((nb, NSA_HEAD_DIM)), full((nb, NSA_HEAD_DIM)),
        ],
        out_specs=[pl.BlockSpec((1, nb, NSA_HEAD_DIM), lambda i: (i, 0, 0))] * 2,
        out_shape=[jax.ShapeDtypeStruct((bh, nb, NSA_HEAD_DIM), F32)] * 2,
        compiler_params=_cparams("parallel"),
        name="nsa_compress",
    )(c, c, wk_all, wv_all, pe, cos, sin)
    return kc, vc


NSA_QB = 512
SEL_KT = 1024
SEL_SUB = 512
SEL_BLOCKS = SEL_SUB // SLC_BLOCK
NSA_KAUG = 2 * LANES


def _nsa_kernel(qt_ref, kc_ref, vct_ref, ovt_ref, ksa_ref, vst_ref, kw_ref, vwt_ref, gate_ref, o_ref,
                sel_scr, s_a, s_b, p_a, p_b, acc_a, acc_b, *, n_sel, kt_w, win_w):
    G, QB, D = NSA_GROUP, NSA_QB, NSA_HEAD_DIM
    GQ = G * QB
    nb = kc_ref.shape[2]
    ns = ovt_ref.shape[0]
    h = pl.program_id(1)
    qb = pl.program_id(2)
    s0 = qb * QB
    q4 = jnp.concatenate([qt_ref[g] for g in range(G)], axis=1)
    zeros = jnp.zeros_like(q4)
    qpad = jnp.concatenate([jnp.where(h == hh, q4, zeros) for hh in range(NSA_KV_HEADS)], axis=0)
    tq = s0 + lax.broadcasted_iota(jnp.int32, (1, QB), 1)
    lanes4 = lambda a: jnp.concatenate([a] * G, axis=1)

    s_c = _dot(kc_ref[0, 0], q4)
    nrow = lax.broadcasted_iota(jnp.int32, (nb, QB), 0)
    mask_c = lanes4((nrow * CMP_STRIDE + (CMP_BLOCK - 1)) <= tq)
    s_c = jnp.where(mask_c, s_c, NEG)
    m_c = jnp.max(s_c, 0, keepdims=True)
    e_c = jnp.where(mask_c, jnp.exp2(s_c - m_c), 0.0)
    l_c = jnp.sum(e_c, 0, keepdims=True)
    p_c = e_c / jnp.where(l_c > 0, l_c, 1.0)
    o_c = _dot(vct_ref[0, 0], p_c.astype(BF16))

    psum = p_c[:, 0:QB]
    for g in range(1, G):
        psum = psum + p_c[:, g * QB:(g + 1) * QB]
    p_hi = psum.astype(BF16)
    p_lo = (psum - p_hi.astype(F32)).astype(BF16)
    imp_t = _dot(ovt_ref[...], p_hi) + _dot(ovt_ref[...], p_lo)

    w0 = pl.multiple_of(jnp.maximum(s0 + QB - win_w, 0), QB)
    s_w = _dot(kw_ref[0, pl.ds(w0, win_w), :], qpad)
    rel = tq - (w0 + lax.broadcasted_iota(jnp.int32, (win_w, QB), 0))
    s_w = s_w + lanes4(jnp.where((rel >= 0) & (rel < WINDOW), 0.0, NEG))
    p_w = jnp.exp2((s_w - jnp.max(s_w, 0, keepdims=True)).astype(BF16))
    acc_w = _dot(vwt_ref[0, :, pl.ds(w0, win_w)], p_w)
    o_w = acc_w[:D] / acc_w[D:D + 1]

    jrow = lax.broadcasted_iota(jnp.int32, (ns, QB), 0)
    tcol = s0 + lax.broadcasted_iota(jnp.int32, (ns, QB), 1)
    cur = tcol // SLC_BLOCK
    forced = (jrow == 0) | ((jrow <= cur) & (jrow > cur - SLC_LOCAL))
    score = jnp.where((jrow * SLC_BLOCK) <= tcol, jnp.where(forced, -2.0, imp_t), -1.0)
    jf = jrow.astype(F32)
    for _ in range(n_sel - 1 - SLC_LOCAL):
        mx = jnp.max(score, 0, keepdims=True)
        first = jnp.min(jnp.where(score == mx, jf, float(ns)), 0, keepdims=True)
        score = jnp.where(jf == first, -2.0, score)
    sel_scr[...] = jnp.where(score == -2.0, 1.0, 0.0)

    krow = lax.broadcasted_iota(jnp.int32, (SEL_SUB, QB), 0)
    n_kt = (s0 + QB + kt_w - 1) // kt_w
    pad_rows = ksa_ref.shape[2] - qpad.shape[0] - 2 * SEL_BLOCKS

    def q_with_mask(k0):
        chosen = sel_scr[pl.ds(pl.multiple_of(k0 // SLC_BLOCK, SEL_BLOCKS), SEL_BLOCKS), :]
        bias = jnp.concatenate([jnp.where(chosen > 0.5, 0.0, NEG), jnp.zeros((SEL_BLOCKS, QB), F32)], axis=0)
        return jnp.concatenate([qpad, lanes4(bias).astype(BF16), jnp.zeros((pad_rows, GQ), BF16)], axis=0)

    def score_into(k0, s_ref):
        s_ref[...] = _dot(ksa_ref[0, pl.ds(k0, SEL_SUB), :], q_with_mask(k0))

    def softmax_into(s_ref, p_ref, k0, m, causal):
        if causal:
            bias = lanes4(jnp.where((k0 + krow) <= tq, 0.0, NEG))
            s_ref[...] = s_ref[...] + bias
        m_new = jnp.maximum(m, jnp.max(s_ref[...], 0, keepdims=True))
        p_ref[...] = jnp.exp2((s_ref[...] - m_new).astype(BF16))
        return jnp.exp2(m - m_new), m_new

    def values_into(k0, p_ref, acc_ref, alpha):
        acc_ref[...] = alpha * acc_ref[...] + _dot(vst_ref[0, :, pl.ds(k0, SEL_SUB)], p_ref[...])

    def sel_trip(t, carry, last):
        m_a, m_b, alpha_b = carry
        k_a = pl.multiple_of(t * kt_w, kt_w)
        k_b = pl.multiple_of(k_a + SEL_SUB, SEL_SUB)
        score_into(k_b, s_b)
        alpha_a, m_a = softmax_into(s_a, p_a, k_a, m_a, last)
        values_into(pl.multiple_of(jnp.maximum(k_b - kt_w, SEL_SUB), SEL_SUB), p_b, acc_b, alpha_b)
        if not last:
            score_into(pl.multiple_of(k_a + kt_w, kt_w), s_a)
        alpha_b, m_b = softmax_into(s_b, p_b, k_b, m_b, last)
        values_into(k_a, p_a, acc_a, alpha_a)
        return m_a, m_b, alpha_b

    acc_a[...] = jnp.zeros_like(acc_a)
    acc_b[...] = jnp.zeros_like(acc_b)
    p_b[...] = jnp.zeros_like(p_b)
    score_into(0, s_a)
    neg_row = jnp.full((1, GQ), NEG, F32)
    carry = lax.fori_loop(0, n_kt - 1, functools.partial(sel_trip, last=False),
                          (neg_row, neg_row, jnp.ones((1, GQ), F32)))
    m_a, m_b, alpha_b = sel_trip(n_kt - 1, carry, True)
    values_into(pl.multiple_of((n_kt - 1) * kt_w + SEL_SUB, SEL_SUB), p_b, acc_b, alpha_b)
    m_s = jnp.maximum(m_a, m_b)
    acc_s = jnp.exp2(m_a - m_s) * acc_a[...] + jnp.exp2(m_b - m_s) * acc_b[...]
    o_s = acc_s[:D] / acc_s[D:D + 1]

    gate = _sigmoid(gate_ref[0, 0])
    for g in range(G):
        cols = slice(g * QB, (g + 1) * QB)
        o_ref[g] = (gate[3 * g:3 * g + 1] * o_c[:, cols] + gate[3 * g + 1:3 * g + 2] * o_s[:, cols]
                    + gate[3 * g + 2:3 * g + 3] * o_w[:, cols]).astype(BF16)


def _nsa_attention(qt, ks, vst, kw, vwt, kc, vc, small, B, S):
    H, HKV, G, D = NSA_HEADS, NSA_KV_HEADS, NSA_GROUP, NSA_HEAD_DIM
    nb = S // CMP_STRIDE
    ns = S // SLC_BLOCK
    nq = S // NSA_QB
    n_sel = min(SLC_TOP_N, ns)
    kt_w = min(SEL_KT, S)
    win_w = min(WINDOW + NSA_QB, S)
    kcb = kc.reshape(B, HKV, nb, D).astype(BF16)
    vct = vc.reshape(B, HKV, nb, D).transpose(0, 1, 3, 2).astype(BF16)
    gates = small[:, :3 * H].reshape(B, S, HKV, 3 * G).transpose(0, 2, 3, 1)
    m_i = jnp.arange(ns)[:, None]
    n_i = jnp.arange(nb)[None, :]
    overlap_t = ((n_i * CMP_STRIDE <= m_i * SLC_BLOCK + SLC_BLOCK - 1)
                 & (n_i * CMP_STRIDE + CMP_BLOCK - 1 >= m_i * SLC_BLOCK)).astype(BF16)
    per_bh = lambda shape: pl.BlockSpec((1, 1) + shape, lambda b, h, i: (b, h, 0, 0))
    keys = pl.BlockSpec((1, S, NSA_KV), lambda b, h, i: (b, 0, 0))
    keys_aug = pl.BlockSpec((1, S, NSA_KAUG), lambda b, h, i: (b, 0, 0))
    vals = pl.BlockSpec((1, 2 * D, S), lambda b, h, i: (h, 0, b))
    heads = pl.BlockSpec((G, D, NSA_QB), lambda b, h, i: (h, 0, b * nq + i))
    return pl.pallas_call(
        functools.partial(_nsa_kernel, n_sel=n_sel, kt_w=kt_w, win_w=win_w),
        grid=(B, HKV, nq),
        in_specs=[
            heads, per_bh((nb, D)), per_bh((D, nb)),
            pl.BlockSpec((ns, nb), lambda b, h, i: (0, 0)),
            keys_aug, vals, keys, vals,
            pl.BlockSpec((1, 1, 3 * G, NSA_QB), lambda b, h, i: (b, h, 0, i)),
        ],
        out_specs=heads,
        out_shape=jax.ShapeDtypeStruct((H, D, B * S), BF16),
        scratch_shapes=[pltpu.VMEM((ns, NSA_QB), F32)]
        + [pltpu.VMEM((SEL_SUB, G * NSA_QB), F32)] * 2 + [pltpu.VMEM((SEL_SUB, G * NSA_QB), BF16)] * 2
        + [pltpu.VMEM((2 * D, G * NSA_QB), F32)] * 2,
        compiler_params=_cparams("parallel", "parallel", "arbitrary"),
        name="nsa_attention",
    )(qt, kcb, vct, overlap_t, ks.reshape(B, S, NSA_KAUG), vst, kw.reshape(B, S, NSA_KV), vwt, gates)


GDN_TB = 512
GDN_HALO = 8


def _gdn_prep_kernel(qkv_ref, halo_ref, sm_ref, cw_ref, alog_ref, dtb_ref,
                     u_ref, wq_ref, kd_ref, intra_ref, dec_ref, xs_scr, y_scr, bg_scr):
    C, HD, NH, BD = GDN_CHUNK, GDN_HEAD_DIM, GDN_HEADS, GDN_BD
    tb = qkv_ref.shape[1]
    first = pl.program_id(1) == 0
    xs_scr[GDN_HALO:GDN_HALO + tb, :] = qkv_ref[0]
    xs_scr[0:GDN_HALO, :] = jnp.where(first, 0.0, halo_ref[0])
    for cb in range(3 * GDN_W // LANES):
        cols = slice(cb * LANES, (cb + 1) * LANES)
        y = cw_ref[GDN_CONV - 1:GDN_CONV, cols] * xs_scr[GDN_HALO:GDN_HALO + tb, cols]
        for d in range(1, GDN_CONV):
            y = y + cw_ref[GDN_CONV - 1 - d:GDN_CONV - d, cols] * xs_scr[GDN_HALO - d:GDN_HALO - d + tb, cols]
        y_scr[:, cols] = _silu(y)
    sm = sm_ref[0]
    b_logit = sm[:, 3 * NSA_HEADS:3 * NSA_HEADS + NH]
    a_logit = sm[:, 3 * NSA_HEADS + NH:3 * NSA_HEADS + 2 * NH] + dtb_ref[...]
    softplus = jnp.maximum(a_logit, 0.0) + jnp.log(1.0 + jnp.exp(-jnp.abs(a_logit)))
    bg_scr[:, 0:NH] = _sigmoid(b_logit)
    bg_scr[:, NH:2 * NH] = -jnp.exp(alog_ref[...]) * softplus

    r = lax.broadcasted_iota(jnp.int32, (BD, BD), 0)
    c = lax.broadcasted_iota(jnp.int32, (BD, BD), 1)
    same = (r // C) == (c // C)
    eye = r == c
    tril = same & (r >= c)
    triu = same & (r <= c)
    strict = same & (r > c)
    eye_f = jnp.where(eye, 1.0, 0.0)

    def stack_heads(m, c0):
        return jnp.concatenate([m[:, c0 + h * HD:c0 + (h + 1) * HD] for h in range(NH)], axis=0)

    def chunk(ci, carry):
        r0 = pl.multiple_of(ci * C, C)
        yc = y_scr[pl.ds(r0, C), :]
        bg = bg_scr[pl.ds(r0, C), :]
        q4 = stack_heads(yc, 0)
        k4 = stack_heads(yc, GDN_W)
        v4 = stack_heads(yc, 2 * GDN_W)
        q4 = q4 * lax.rsqrt(jnp.sum(q4 * q4, -1, keepdims=True) + NORM_EPS) * (HD ** -0.5)
        k4 = k4 * lax.rsqrt(jnp.sum(k4 * k4, -1, keepdims=True) + NORM_EPS)
        b4 = jnp.concatenate([bg[:, h:h + 1] for h in range(NH)], axis=0)
        g4 = jnp.concatenate([bg[:, NH + h:NH + h + 1] for h in range(NH)], axis=0)
        g_row = jnp.sum(jnp.where(eye, g4, 0.0), axis=0, keepdims=True)
        gc_col = jnp.sum(jnp.where(tril, g_row, 0.0), axis=1, keepdims=True)
        gc_row = jnp.sum(jnp.where(triu, g4, 0.0), axis=0, keepdims=True)
        gl_col = jnp.sum(jnp.where(same, g_row, 0.0), axis=1, keepdims=True)
        decay = jnp.where(tril, jnp.exp(jnp.minimum(gc_col - gc_row, 0.0)), 0.0)
        kb4 = k4 * b4
        k4b = k4.astype(BF16)
        lmat = jnp.where(strict, _dot_nt(kb4.astype(BF16), k4b) * decay, 0.0)
        intra = jnp.where(tril, _dot_nt(q4.astype(BF16), k4b) * decay, 0.0)
        inv = eye_f - lmat
        pw = lmat
        for _ in range(int(math.log2(C)) - 1):
            pwb = pw.astype(BF16)
            pw = _dot(pwb, pwb)
            inv = _dot2(inv, eye_f + pw)
        egc = jnp.exp(gc_col)
        uw = _dot2(inv, jnp.concatenate([v4 * b4, kb4 * egc], axis=1))
        u_ref[0, ci] = uw[:, :HD]
        w4 = uw[:, HD:].astype(BF16).reshape(NH, C, HD)
        qg4 = (q4 * egc).astype(BF16).reshape(NH, C, HD)
        wq_ref[0, ci] = jnp.concatenate([w4, qg4], axis=1)
        kd_ref[0, ci] = (k4 * jnp.exp(gl_col - gc_col)).astype(BF16)
        intra_ref[0, ci] = intra.astype(BF16)
        dec = jnp.broadcast_to(jnp.exp(gl_col), (BD, LANES)).reshape(NH, C, LANES)
        dec_ref[0, ci] = dec[:, 0:8, :]
        return carry

    for ci in range(tb // C):
        chunk(ci, 0)


def _gdn_scan_kernel(u_ref, wq_ref, kd_ref, intra_ref, dec_ref, z_ref, nw_ref, o_ref, s_scr):
    C, HD, NH = GDN_CHUNK, GDN_HEAD_DIM, GDN_HEADS
    nbatch = u_ref.shape[0]

    @pl.when(pl.program_id(0) == 0)
    def _():
        s_scr[...] = jnp.zeros_like(s_scr)

    nw = nw_ref[...]
    for b in range(nbatch):
        vnew, qs = [], []
        for h in range(NH):
            st = s_scr[b * NH + h]
            res = _dot(wq_ref[b, 0, h], st.astype(BF16))
            vnew.append(u_ref[b, 0, h * C:(h + 1) * C, :] - res[:C])
            qs.append(res[C:])
        vnew4 = jnp.concatenate(vnew, axis=0).astype(BF16)
        o4 = jnp.concatenate(qs, axis=0) + _dot(intra_ref[b, 0], vnew4)
        for h in range(NH):
            rows = slice(h * C, (h + 1) * C)
            st = s_scr[b * NH + h]
            s_scr[b * NH + h] = st * dec_ref[b, 0, h, 0:1, :] + _dot_tn(kd_ref[b, 0, rows, :], vnew4[rows])
            oh = o4[rows]
            oh = oh * lax.rsqrt(jnp.mean(oh * oh, -1, keepdims=True) + NORM_EPS) * nw
            o_ref[b, :, h * HD:(h + 1) * HD] = oh * _silu(z_ref[b, :, h * HD:(h + 1) * HD])


def _gdn_mixer(gqkv, z, small, conv_w, a_log, dt_bias, norm_w, B, S):
    C, HD, NH, BD = GDN_CHUNK, GDN_HEAD_DIM, GDN_HEADS, GDN_BD
    nc = S // C
    tb = min(GDN_TB, S)
    cb = tb // C
    qkv3 = gqkv.reshape(B, S, 3 * GDN_W)
    sm3 = small.reshape(B, S, LANES)
    hb = tb // GDN_HALO
    u, wq, kd, intra, dec = pl.pallas_call(
        _gdn_prep_kernel,
        grid=(B, S // tb),
        in_specs=[
            pl.BlockSpec((1, tb, 3 * GDN_W), lambda b, i: (b, i, 0)),
            pl.BlockSpec((1, GDN_HALO, 3 * GDN_W), lambda b, i: (b, jnp.maximum(i * hb - 1, 0), 0)),
            pl.BlockSpec((1, tb, LANES), lambda b, i: (b, i, 0)),
            pl.BlockSpec((GDN_CONV, 3 * GDN_W), lambda b, i: (0, 0)),
            pl.BlockSpec((1, NH), lambda b, i: (0, 0)),
            pl.BlockSpec((1, NH), lambda b, i: (0, 0)),
        ],
        out_specs=[
            pl.BlockSpec((1, cb, BD, HD), lambda b, i: (b, i, 0, 0)),
            pl.BlockSpec((1, cb, NH, 2 * C, HD), lambda b, i: (b, i, 0, 0, 0)),
            pl.BlockSpec((1, cb, BD, HD), lambda b, i: (b, i, 0, 0)),
            pl.BlockSpec((1, cb, BD, BD), lambda b, i: (b, i, 0, 0)),
            pl.BlockSpec((1, cb, NH, 8, LANES), lambda b, i: (b, i, 0, 0, 0)),
        ],
        out_shape=[
            jax.ShapeDtypeStruct((B, nc, BD, HD), F32),
            jax.ShapeDtypeStruct((B, nc, NH, 2 * C, HD), BF16),
            jax.ShapeDtypeStruct((B, nc, BD, HD), BF16),
            jax.ShapeDtypeStruct((B, nc, BD, BD), BF16),
            jax.ShapeDtypeStruct((B, nc, NH, 8, LANES), F32),
        ],
        scratch_shapes=[
            pltpu.VMEM((GDN_HALO + tb, 3 * GDN_W), F32),
            pltpu.VMEM((tb, 3 * GDN_W), F32),
            pltpu.VMEM((tb, 2 * NH), F32),
        ],
        compiler_params=_cparams("parallel", "parallel"),
        name="gdn_prepare",
    )(qkv3, qkv3, sm3, conv_w, a_log.reshape(1, NH), dt_bias.reshape(1, NH))

    o = pl.pallas_call(
        _gdn_scan_kernel,
        grid=(nc,),
        in_specs=[
            pl.BlockSpec((B, 1, BD, HD), lambda c: (0, c, 0, 0)),
            pl.BlockSpec((B, 1, NH, 2 * C, HD), lambda c: (0, c, 0, 0, 0)),
            pl.BlockSpec((B, 1, BD, HD), lambda c: (0, c, 0, 0)),
            pl.BlockSpec((B, 1, BD, BD), lambda c: (0, c, 0, 0)),
            pl.BlockSpec((B, 1, NH, 8, LANES), lambda c: (0, c, 0, 0, 0)),
            pl.BlockSpec((B, C, GDN_W), lambda c: (0, c, 0)),
            pl.BlockSpec((1, HD), lambda c: (0, 0)),
        ],
        out_specs=pl.BlockSpec((B, C, GDN_W), lambda c: (0, c, 0)),
        out_shape=jax.ShapeDtypeStruct((B, S, GDN_W), F32),
        scratch_shapes=[pltpu.VMEM((B * NH, HD, HD), F32)],
        compiler_params=_cparams("arbitrary"),
        name="gdn_scan",
    )(u, wq, kd, intra, dec, z.reshape(B, S, GDN_W), norm_w.reshape(1, HD))
    return o.reshape(B * S, GDN_W)


MIX_TM = 512


def _mix_ln_kernel(x_ref, at_ref, b_ref, w_ref, g_ref, be_ref, o_ref):
    a_t = at_ref[...].reshape(NSA_Q, x_ref.shape[0])
    mix = (_dot_tn(a_t, w_ref[0:NSA_Q, :])
           + _dot(b_ref[...].astype(BF16), w_ref[NSA_Q:NSA_Q + GDN_W, :]))
    o_ref[...] = _layer_norm(DN_ALPHA * x_ref[...] + mix, g_ref[...], be_ref[...])


def _mix_out_ln(x2d, o_at, o_b, w_out, g, b):
    T = x2d.shape[0]
    row = lambda i: (i, 0)
    const = lambda i: (0, 0)
    return pl.pallas_call(
        _mix_ln_kernel,
        grid=(T // MIX_TM,),
        in_specs=[
            pl.BlockSpec((MIX_TM, D_MODEL), row),
            pl.BlockSpec((NSA_HEADS, NSA_HEAD_DIM, MIX_TM), lambda i: (0, 0, i)),
            pl.BlockSpec((MIX_TM, GDN_W), row), pl.BlockSpec((NSA_Q + GDN_W, D_MODEL), const),
            pl.BlockSpec((1, D_MODEL), const), pl.BlockSpec((1, D_MODEL), const),
        ],
        out_specs=pl.BlockSpec((MIX_TM, D_MODEL), row),
        out_shape=jax.ShapeDtypeStruct((T, D_MODEL), F32),
        compiler_params=_cparams("parallel"),
        name="mix_out_ln",
    )(x2d, o_at, o_b, w_out.astype(BF16), g.reshape(1, D_MODEL), b.reshape(1, D_MODEL))


FFN_TM = 512
FFN_TF = 256


def _ffn_ln_kernel(x_ref, wg_ref, wu_ref, wd_ref, g_ref, be_ref, o_ref, hid_scr):
    x = x_ref[...]
    xb = x.astype(BF16)
    for j in range(hid_scr.shape[1] // FFN_TF):
        cols = slice(j * FFN_TF, (j + 1) * FFN_TF)
        hid_scr[:, cols] = (_silu(_dot(xb, wg_ref[:, cols])) * _dot(xb, wu_ref[:, cols])).astype(BF16)
    o_ref[...] = _layer_norm(DN_ALPHA * x + _dot(hid_scr[...], wd_ref[...]), g_ref[...], be_ref[...])


def _ffn_ln(x2d, wg, wu, wd, g, b):
    T = x2d.shape[0]
    dff = wg.shape[1]
    tm = min(FFN_TM, T)
    const = lambda i: (0, 0)
    resident = lambda shape: pl.BlockSpec(shape, const, pipeline_mode=pl.Buffered(1))
    return pl.pallas_call(
        _ffn_ln_kernel,
        grid=(T // tm,),
        in_specs=[
            pl.BlockSpec((tm, D_MODEL), lambda i: (i, 0)),
            resident((D_MODEL, dff)), resident((D_MODEL, dff)), resident((dff, D_MODEL)),
            pl.BlockSpec((1, D_MODEL), const), pl.BlockSpec((1, D_MODEL), const),
        ],
        out_specs=pl.BlockSpec((tm, D_MODEL), lambda i: (i, 0)),
        out_shape=jax.ShapeDtypeStruct((T, D_MODEL), F32),
        scratch_shapes=[pltpu.VMEM((tm, dff), BF16)],
        compiler_params=_cparams("parallel"),
        name="ffn_ln",
    )(x2d, wg.astype(BF16), wu.astype(BF16), wd.astype(BF16), g.reshape(1, D_MODEL), b.reshape(1, D_MODEL))


POOL_TM = 512
POOL_HALO = 16


def _route_top2(x1, rw):
    lane = lax.broadcasted_iota(jnp.int32, (x1.shape[0], LANES), 1)
    lanef = lane.astype(F32)
    logits = jnp.where(lane < N_EXPERTS, _dot3(x1, rw), NEG)
    ex = jnp.exp(logits - jnp.max(logits, -1, keepdims=True))
    probs = jnp.where(lane < N_EXPERTS, ex / jnp.sum(ex, -1, keepdims=True), -1.0)
    p1 = jnp.max(probs, -1, keepdims=True)
    i1 = jnp.min(jnp.where(probs == p1, lanef, float(LANES)), -1, keepdims=True)
    rest = jnp.where(lanef == i1, -1.0, probs)
    p2 = jnp.max(rest, -1, keepdims=True)
    i2 = jnp.min(jnp.where(rest == p2, lanef, float(LANES)), -1, keepdims=True)
    den = p1 + p2
    return jnp.where(lane == 0, i1, jnp.where(lane == 1, i2, jnp.where(
        lane == 2, p1 / den, jnp.where(lane == 3, p2 / den, 0.0))))


def _pool_ln_kernel(x_ref, halo_ref, pw_ref, ps_ref, g_ref, be_ref, rw_ref, o_ref, route_ref, xs_scr):
    tm = x_ref.shape[1]
    i = pl.program_id(1)
    x = x_ref[0]
    xs_scr[POOL_HALO:POOL_HALO + tm, :] = x
    xs_scr[0:POOL_HALO, :] = jnp.where(i == 0, 0.0, halo_ref[0])
    t1 = (i * tm + 1 + lax.broadcasted_iota(jnp.int32, (tm, 1), 0)).astype(F32)
    for gi, win in enumerate(POOL_SIZES):
        cols = slice(gi * POOL_GROUP, (gi + 1) * POOL_GROUP)
        acc = x[:, cols]
        for d in range(1, win):
            acc = acc + xs_scr[POOL_HALO - d:POOL_HALO - d + tm, cols]
        mean = acc / jnp.minimum(t1, float(win))
        mix = _dot((mean - x[:, cols]).astype(BF16), pw_ref[gi]) * ps_ref[:, cols]
        xs_scr[POOL_HALO:POOL_HALO + tm, cols] = DN_ALPHA * x[:, cols] + mix
    x1 = _layer_norm(xs_scr[POOL_HALO:POOL_HALO + tm, :], g_ref[...], be_ref[...])
    o_ref[0] = x1
    route_ref[0] = _route_top2(x1, rw_ref[...])


def _pool_ln(x3d, pool_w, pool_scale, g, b, router_w):
    B, S, _ = x3d.shape
    tm = min(POOL_TM, S)
    hb = tm // POOL_HALO
    const = lambda bb, i: (0, 0)
    rw = jnp.concatenate([router_w, jnp.zeros((D_MODEL, LANES - N_EXPERTS), F32)], axis=1)
    return pl.pallas_call(
        _pool_ln_kernel,
        grid=(B, S // tm),
        in_specs=[
            pl.BlockSpec((1, tm, D_MODEL), lambda bb, i: (bb, i, 0)),
            pl.BlockSpec((1, POOL_HALO, D_MODEL), lambda bb, i: (bb, jnp.maximum(i * hb - 1, 0), 0)),
            pl.BlockSpec((len(POOL_SIZES), POOL_GROUP, POOL_GROUP), lambda bb, i: (0, 0, 0)),
            pl.BlockSpec((1, D_MODEL), const), pl.BlockSpec((1, D_MODEL), const),
            pl.BlockSpec((1, D_MODEL), const), pl.BlockSpec((D_MODEL, LANES), const),
        ],
        out_specs=[pl.BlockSpec((1, tm, D_MODEL), lambda bb, i: (bb, i, 0)),
                   pl.BlockSpec((1, tm, LANES), lambda bb, i: (bb, i, 0))],
        out_shape=[jax.ShapeDtypeStruct((B, S, D_MODEL), F32), jax.ShapeDtypeStruct((B, S, LANES), F32)],
        scratch_shapes=[pltpu.VMEM((POOL_HALO + tm, D_MODEL), F32)],
        compiler_params=_cparams("parallel", "parallel"),
        name="pool_ln",
    )(x3d, x3d, pool_w.astype(BF16), pool_scale.reshape(1, D_MODEL), g.reshape(1, D_MODEL),
      b.reshape(1, D_MODEL), rw)


MOE_TM = 1024
MOE_TF = 896
MOE_TC = 512


def _moe_plan(route, T, tm):
    ne = N_EXPERTS
    e_flat = jnp.concatenate([route[:, 0], route[:, 1]]).astype(jnp.int32)
    onehot = (e_flat[:, None] == jnp.arange(ne)[None, :]).astype(jnp.int32)
    csum = jnp.cumsum(onehot, axis=0)
    rank = jnp.sum(onehot * csum, axis=1) - 1
    counts = csum[-1]
    padded = (counts + tm - 1) // tm * tm
    gend = jnp.cumsum(padded)
    gstart = gend - padded
    pos = gstart[e_flat] + rank
    n_tiles = (2 * T) // tm + ne
    tile_e = jnp.minimum(jnp.searchsorted(gend, jnp.arange(n_tiles) * tm, side="right"), ne - 1)
    tile_e = tile_e.astype(jnp.int32)
    n_used = (gend[-1] // tm).astype(jnp.int32).reshape(1)
    order = jnp.argsort(e_flat, stable=True)
    cstart = jnp.cumsum(counts) - counts
    e_p = jnp.repeat(tile_e, tm)
    r = jnp.clip(jnp.arange(n_tiles * tm) - gstart[e_p], 0, jnp.maximum(counts[e_p] - 1, 0))
    src = (order[jnp.clip(cstart[e_p] + r, 0, 2 * T - 1)] % T).astype(jnp.int32)
    return pos.astype(jnp.int32), src, tile_e, n_used, n_tiles


SUBLANES = 8


def _gather_rows(src_hbm, idx, dst, sem):
    def issue(blk, c):
        for u in range(SUBLANES):
            row = idx(blk * SUBLANES + u)
            pltpu.make_async_copy(src_hbm.at[row >> 3, pl.ds(row & (SUBLANES - 1), 1)],
                                  dst.at[blk, pl.ds(u, 1)], sem).start()
        return c

    lax.fori_loop(0, dst.shape[0], issue, 0)


def _gather_wait(src_hbm, dst, sem):
    pltpu.make_async_copy(src_hbm.at[pl.ds(0, dst.shape[0])], dst, sem).wait()


def _moe_expert_kernel(te_ref, nu_ref, src_ref, nxt_ref, x_hbm, wg_ref, wu_ref, wd_ref, o_ref,
                       xbuf, xb_scr, acc_scr, sem):
    i = pl.program_id(0)
    j = pl.program_id(1)
    n_used = nu_ref[0]
    used = i < n_used
    slot = i % 2

    @pl.when((i == 0) & (j == 0))
    def _():
        _gather_rows(x_hbm, lambda r: src_ref[0, 0, r], xbuf.at[0], sem.at[0])

    @pl.when((i + 1 < n_used) & (j == 0))
    def _():
        _gather_rows(x_hbm, lambda r: nxt_ref[0, 0, r], xbuf.at[1 - slot], sem.at[1 - slot])

    @pl.when(used & (j == 0))
    def _():
        _gather_wait(x_hbm, xbuf.at[slot], sem.at[slot])
        xb_scr[...] = xbuf[slot].reshape(xb_scr.shape).astype(BF16)
        acc_scr[...] = jnp.zeros_like(acc_scr)

    @pl.when(used)
    def _():
        xb = xb_scr[...]
        hid = _silu(_dot(xb, wg_ref[0])) * _dot(xb, wu_ref[0])
        acc_scr[...] += _dot(hid.astype(BF16), wd_ref[0])

    last = j == pl.num_programs(1) - 1

    @pl.when(used & last)
    def _():
        o_ref[...] = acc_scr[...]

    @pl.when(jnp.logical_not(used) & last)
    def _():
        o_ref[...] = jnp.zeros_like(o_ref)


def _moe_combine_kernel(pos_ref, nxt_ref, x_ref, route_ref, ys_hbm, g_ref, be_ref, o_ref, ybuf, sem):
    i = pl.program_id(0)
    slot = i % 2

    def fetch(p_ref, s):
        for k in range(2):
            _gather_rows(ys_hbm, lambda r: p_ref[0, k, r], ybuf.at[s, k], sem.at[s])

    @pl.when(i == 0)
    def _():
        fetch(pos_ref, 0)

    @pl.when(i + 1 < pl.num_programs(0))
    def _():
        fetch(nxt_ref, 1 - slot)

    for k in range(2):
        _gather_wait(ys_hbm, ybuf.at[slot, k], sem.at[slot])
    route = route_ref[...]
    y0 = ybuf[slot, 0].reshape(x_ref.shape)
    y1 = ybuf[slot, 1].reshape(x_ref.shape)
    y = DN_ALPHA * x_ref[...] + route[:, 2:3] * y0 + route[:, 3:4] * y1
    o_ref[...] = _layer_norm(y, g_ref[...], be_ref[...])


def _moe_ln(x2d, route, wg, wu, wd, g, b):
    T = x2d.shape[0]
    ne, _, dff = wg.shape
    tm = min(MOE_TM, T)
    tc = min(MOE_TC, T)
    pos, src, tile_e, n_used, n_tiles = _moe_plan(route, T, tm)

    nj = dff // MOE_TF

    def w_col(i, j, te, nu):
        return (te[i], 0, jnp.where(i < nu[0], j, 0))

    def w_row(i, j, te, nu):
        return (te[i], jnp.where(i < nu[0], j, 0), 0)

    src3 = src.reshape(n_tiles, 1, tm)
    ys = pl.pallas_call(
        _moe_expert_kernel,
        grid_spec=pltpu.PrefetchScalarGridSpec(
            num_scalar_prefetch=2,
            grid=(n_tiles, nj),
            in_specs=[
                pl.BlockSpec((1, 1, tm), lambda i, j, te, nu: (i, 0, 0), memory_space=pltpu.SMEM),
                pl.BlockSpec((1, 1, tm), lambda i, j, te, nu: (jnp.minimum(i + 1, n_tiles - 1), 0, 0),
                             memory_space=pltpu.SMEM),
                pl.BlockSpec(memory_space=pl.ANY),
                pl.BlockSpec((1, D_MODEL, MOE_TF), w_col),
                pl.BlockSpec((1, D_MODEL, MOE_TF), w_col),
                pl.BlockSpec((1, MOE_TF, D_MODEL), w_row),
            ],
            out_specs=pl.BlockSpec((tm, D_MODEL), lambda i, j, te, nu: (i, 0)),
            scratch_shapes=[pltpu.VMEM((2, tm // SUBLANES, SUBLANES, D_MODEL), F32),
                            pltpu.VMEM((tm, D_MODEL), BF16), pltpu.VMEM((tm, D_MODEL), F32),
                            pltpu.SemaphoreType.DMA((2,))],
        ),
        out_shape=jax.ShapeDtypeStruct((n_tiles * tm, D_MODEL), F32),
        compiler_params=_cparams("arbitrary", "arbitrary"),
        name="moe_experts",
    )(tile_e, n_used, src3, src3, x2d.reshape(T // SUBLANES, SUBLANES, D_MODEL), wg.astype(BF16),
      wu.astype(BF16), wd.astype(BF16))

    n_steps = T // tc
    pos3 = pos.reshape(2, n_steps, tc).transpose(1, 0, 2)
    row = lambda i: (i, 0)
    const = lambda i: (0, 0)
    return pl.pallas_call(
        _moe_combine_kernel,
        grid=(n_steps,),
        in_specs=[
            pl.BlockSpec((1, 2, tc), lambda i: (i, 0, 0), memory_space=pltpu.SMEM),
            pl.BlockSpec((1, 2, tc), lambda i: (jnp.minimum(i + 1, n_steps - 1), 0, 0),
                         memory_space=pltpu.SMEM),
            pl.BlockSpec((tc, D_MODEL), row), pl.BlockSpec((tc, LANES), row),
            pl.BlockSpec(memory_space=pl.ANY),
            pl.BlockSpec((1, D_MODEL), const), pl.BlockSpec((1, D_MODEL), const),
        ],
        out_specs=pl.BlockSpec((tc, D_MODEL), row),
        out_shape=jax.ShapeDtypeStruct((T, D_MODEL), F32),
        scratch_shapes=[pltpu.VMEM((2, 2, tc // SUBLANES, SUBLANES, D_MODEL), F32),
                        pltpu.SemaphoreType.DMA((2,))],
        compiler_params=_cparams("arbitrary"),
        name="moe_combine_ln",
    )(pos3, pos3, x2d, route, ys.reshape(n_tiles * tm // SUBLANES, SUBLANES, D_MODEL),
      g.reshape(1, D_MODEL), b.reshape(1, D_MODEL))


def _even_layer(x2d, B, S, w_in, pe_k, w_k, pe_v, w_v, conv_w, a_log, dt_bias, gdn_norm, w_out,
                ln1_g, ln1_b, wg, wu, wd, ln2_g, ln2_b):
    qt, vst, vwt, ks, kw, cmp2d, gqkv, z, small = _in_projection(x2d, w_in, S)
    kc, vc = _compress(cmp2d, pe_k, w_k, pe_v, w_v, B, S)
    o_a = _nsa_attention(qt, ks, vst, kw, vwt, kc, vc, small, B, S)
    o_b = _gdn_mixer(gqkv, z, small, conv_w, a_log, dt_bias, gdn_norm, B, S)
    x2d = _mix_out_ln(x2d, o_a, o_b, w_out, ln1_g, ln1_b)
    return _ffn_ln(x2d, wg, wu, wd, ln2_g, ln2_b)


def _odd_layer(x2d, B, S, pool_w, pool_scale, ln1_g, ln1_b, router_w, wg, wu, wd, ln2_g, ln2_b):
    x3d, route = _pool_ln(x2d.reshape(B, S, D_MODEL), pool_w, pool_scale, ln1_g, ln1_b, router_w)
    return _moe_ln(x3d.reshape(B * S, D_MODEL), route.reshape(B * S, LANES), wg, wu, wd, ln2_g, ln2_b)


def kernel(x, ev_w_in, ev_cmp_pe_k, ev_cmp_w_k, ev_cmp_pe_v, ev_cmp_w_v, ev_conv_w, ev_a_log, ev_dt_bias,
           ev_gdn_norm, ev_w_out, ev_ln1_g, ev_ln1_b, ev_ffn_wg, ev_ffn_wu, ev_ffn_wd, ev_ln2_g, ev_ln2_b,
           od_pool_w, od_pool_scale, od_ln1_g, od_ln1_b, od_router_w, od_exp_wg, od_exp_wu, od_exp_wd,
           od_ln2_g, od_ln2_b):
    B, S, _ = x.shape
    h = x.reshape(B * S, D_MODEL)
    for layer in range(DEPTH):
        i = layer // 2
        if layer % 2 == 0:
            h = _even_layer(h, B, S, ev_w_in[i], ev_cmp_pe_k[i], ev_cmp_w_k[i], ev_cmp_pe_v[i],
                            ev_cmp_w_v[i], ev_conv_w[i], ev_a_log[i], ev_dt_bias[i], ev_gdn_norm[i],
                            ev_w_out[i], ev_ln1_g[i], ev_ln1_b[i], ev_ffn_wg[i], ev_ffn_wu[i],
                            ev_ffn_wd[i], ev_ln2_g[i], ev_ln2_b[i])
        else:
            h = _odd_layer(h, B, S, od_pool_w[i], od_pool_scale[i], od_ln1_g[i], od_ln1_b[i],
                           od_router_w[i], od_exp_wg[i], od_exp_wu[i], od_exp_wd[i], od_ln2_g[i],
                           od_ln2_b[i])
    return h.reshape(B, S, D_MODEL)
```

```python
import functools
import math

import jax
import jax.numpy as jnp
from jax import lax
from jax.experimental import pallas as pl
from jax.experimental.pallas import tpu as pltpu

F32 = jnp.float32
BF16 = jnp.bfloat16

D_MODEL = 1024
DEPTH = 2
NSA_HEADS = 8
NSA_KV_HEADS = 2
NSA_GROUP = NSA_HEADS // NSA_KV_HEADS
NSA_HEAD_DIM = 64
CMP_BLOCK = 32
CMP_STRIDE = 16
SLC_BLOCK = 64
SLC_TOP_N = 16
SLC_LOCAL = 2
WINDOW = 512
Q_BLOCK = 128
FORCED_SCORE = 1e6
GDN_HEADS = 4
GDN_HEAD_DIM = 128
GDN_CONV = 4
GDN_CHUNK = 64
POOL_SIZES = (2, 4, 8, 16)
POOL_GROUP = D_MODEL // 4
D_FF = 2816
N_EXPERTS = 8
D_FF_EXPERT = 3584
ROPE_THETA = 10000.0
LN_EPS = 1e-5
NORM_EPS = 1e-6
DN_ALPHA = (2 * DEPTH) ** 0.25

NSA_Q = NSA_HEADS * NSA_HEAD_DIM
NSA_KV = NSA_KV_HEADS * NSA_HEAD_DIM
GDN_W = GDN_HEADS * GDN_HEAD_DIM
GDN_BD = GDN_HEADS * GDN_CHUNK

LANES = 128
NEG = -1e30
LOG2E = 1.4426950408889634
VMEM_LIMIT = 56 * 1024 * 1024


def _cparams(*sem):
    return pltpu.CompilerParams(dimension_semantics=sem, vmem_limit_bytes=VMEM_LIMIT)


def _dot(a, b):
    return jnp.dot(a, b, preferred_element_type=F32)


def _dot_nt(a, b):
    return lax.dot_general(a, b, (((1,), (1,)), ((), ())), preferred_element_type=F32)


def _dot_tn(a, b):
    return lax.dot_general(a, b, (((0,), (0,)), ((), ())), preferred_element_type=F32)


def _dot3(a, b):
    ah = a.astype(BF16)
    al = (a - ah.astype(F32)).astype(BF16)
    bh = b.astype(BF16)
    bl = (b - bh.astype(F32)).astype(BF16)
    return _dot(ah, bh) + (_dot(ah, bl) + _dot(al, bh))


def _dot2(a, b):
    ah = a.astype(BF16)
    al = (a - ah.astype(F32)).astype(BF16)
    bh = b.astype(BF16)
    return _dot(ah, bh) + _dot(al, bh)


def _layer_norm(y, g, b):
    mu = jnp.mean(y, -1, keepdims=True)
    d = y - mu
    var = jnp.mean(d * d, -1, keepdims=True)
    return d * lax.rsqrt(var + LN_EPS) * g + b


def _sigmoid(x):
    return 1.0 / (1.0 + jnp.exp(-x))


def _silu(x):
    return x * _sigmoid(x)


IN_TM = 512
_R_Q, _R_QR, _R_VS, _R_VW, _R_END = 0, 512, 1024, 1152, 1280
_C_KS, _C_KSR, _C_KW, _C_KWR, _C_CMP, _C_GQKV, _C_Z, _C_SM, _C_END = 0, 128, 256, 384, 512, 768, 2304, 2816, 2944


def _inproj_kernel(x_ref, wt_ref, wn_ref, cos_ref, sin_ref, cost_ref, sint_ref,
                   qt_ref, vst_ref, vwt_ref, ks_ref, kw_ref, cmp_ref, gqkv_ref, z_ref, sm_ref):
    D, HKV = NSA_HEAD_DIM, NSA_KV_HEADS
    tm = x_ref.shape[0]
    xb = x_ref[...].astype(BF16)

    def mt(r0, n):
        return _dot_nt(wt_ref[r0:r0 + n, :], xb)

    def mm(c0, n):
        return _dot(xb, wn_ref[:, c0:c0 + n])

    scale = D ** -0.5 * LOG2E
    cost = cost_ref[...]
    sint = sint_ref[...]
    for h0 in range(0, NSA_HEADS, NSA_GROUP):
        a = mt(_R_Q + h0 * D, NSA_GROUP * D)
        r = mt(_R_QR + h0 * D, NSA_GROUP * D)
        for g in range(NSA_GROUP):
            rows = slice(g * D, (g + 1) * D)
            qt_ref[h0 + g] = ((a[rows] * cost + r[rows] * sint) * scale).astype(BF16)
    ones_rows = jnp.where(lax.broadcasted_iota(jnp.int32, (D, tm), 0) == 0, 1.0, 0.0).astype(BF16)
    for r0, out in ((_R_VS, vst_ref), (_R_VW, vwt_ref)):
        vt = mt(r0, NSA_KV).astype(BF16)
        for h in range(HKV):
            out[h, 0:D, :] = vt[h * D:(h + 1) * D]
            out[h, D:2 * D, :] = ones_rows
    cos = cos_ref[...]
    sin = sin_ref[...]
    ks_ref[:, 0:LANES] = (mm(_C_KS, LANES) * cos + mm(_C_KSR, LANES) * sin).astype(BF16)
    r_i = lax.broadcasted_iota(jnp.int32, (tm, NSA_KAUG - LANES), 0)
    c_i = lax.broadcasted_iota(jnp.int32, (tm, NSA_KAUG - LANES), 1)
    ks_ref[:, LANES:NSA_KAUG] = jnp.where((r_i // SLC_BLOCK) % SEL_BLOCKS == c_i, 1.0, 0.0).astype(BF16)
    kw_ref[...] = (mm(_C_KW, LANES) * cos + mm(_C_KWR, LANES) * sin).astype(BF16)
    cmp_ref[...] = mm(_C_CMP, 2 * NSA_KV)
    gqkv_ref[...] = mm(_C_GQKV, 3 * GDN_W)
    z_ref[...] = mm(_C_Z, GDN_W)
    sm_ref[...] = mm(_C_SM, LANES)


def _rot_cols(w):
    k, n = w.shape
    half = NSA_HEAD_DIM // 2
    w4 = w.reshape(k, n // NSA_HEAD_DIM, 2, half)
    return jnp.stack([-w4[:, :, 1], w4[:, :, 0]], axis=2).reshape(k, n)


def _rope_tables(pos, reps):
    half = NSA_HEAD_DIM // 2
    inv = jnp.power(ROPE_THETA, -jnp.arange(half, dtype=F32) / half)
    ang = pos.astype(F32)[:, None] * inv[None, :]
    cos = jnp.tile(jnp.cos(ang), (1, 2 * reps))
    sin = jnp.tile(jnp.sin(ang), (1, 2 * reps))
    return cos, sin


def _in_projection(x2d, w_in, S):
    T = x2d.shape[0]
    o = 0
    parts = {}
    for name, n in (("q", NSA_Q), ("kc", NSA_KV), ("vc", NSA_KV), ("ks", NSA_KV), ("vs", NSA_KV),
                    ("kw", NSA_KV), ("vw", NSA_KV), ("gl", 3 * NSA_HEADS), ("gq", GDN_W), ("gk", GDN_W),
                    ("gv", GDN_W), ("gz", GDN_W), ("gb", GDN_HEADS), ("ga", GDN_HEADS)):
        parts[name] = w_in[:, o:o + n]
        o += n
    pad = jnp.zeros((D_MODEL, LANES - 3 * NSA_HEADS - 2 * GDN_HEADS), F32)
    w_t = jnp.concatenate([parts["q"], _rot_cols(parts["q"]), parts["vs"], parts["vw"]], axis=1).T.astype(BF16)
    w_n = jnp.concatenate([
        parts["ks"], _rot_cols(parts["ks"]), parts["kw"], _rot_cols(parts["kw"]), parts["kc"], parts["vc"],
        parts["gq"], parts["gk"], parts["gv"], parts["gz"], parts["gl"], parts["gb"], parts["ga"], pad,
    ], axis=1).astype(BF16)
    assert w_t.shape[0] == _R_END and w_n.shape[1] == _C_END
    cos, sin = _rope_tables(jnp.arange(S), LANES // NSA_HEAD_DIM)
    cos_t, sin_t = cos[:, :NSA_HEAD_DIM].T, sin[:, :NSA_HEAD_DIM].T
    tm = min(IN_TM, S)
    nblk = S // tm
    row = lambda i: (i, 0)
    col3 = lambda i: (0, 0, i)
    const = lambda i: (0, 0)
    return pl.pallas_call(
        _inproj_kernel,
        grid=(T // tm,),
        in_specs=[
            pl.BlockSpec((tm, D_MODEL), row),
            pl.BlockSpec((_R_END, D_MODEL), const), pl.BlockSpec((D_MODEL, _C_END), const),
            pl.BlockSpec((tm, LANES), lambda i: (i % nblk, 0)),
            pl.BlockSpec((tm, LANES), lambda i: (i % nblk, 0)),
            pl.BlockSpec((NSA_HEAD_DIM, tm), lambda i: (0, i % nblk)),
            pl.BlockSpec((NSA_HEAD_DIM, tm), lambda i: (0, i % nblk)),
        ],
        out_specs=[
            pl.BlockSpec((NSA_HEADS, NSA_HEAD_DIM, tm), col3),
            pl.BlockSpec((NSA_KV_HEADS, 2 * NSA_HEAD_DIM, tm), col3),
            pl.BlockSpec((NSA_KV_HEADS, 2 * NSA_HEAD_DIM, tm), col3),
            pl.BlockSpec((tm, NSA_KAUG), row), pl.BlockSpec((tm, NSA_KV), row),
            pl.BlockSpec((tm, 2 * NSA_KV), row), pl.BlockSpec((tm, 3 * GDN_W), row),
            pl.BlockSpec((tm, GDN_W), row), pl.BlockSpec((tm, LANES), row),
        ],
        out_shape=[
            jax.ShapeDtypeStruct((NSA_HEADS, NSA_HEAD_DIM, T), BF16),
            jax.ShapeDtypeStruct((NSA_KV_HEADS, 2 * NSA_HEAD_DIM, T), BF16),
            jax.ShapeDtypeStruct((NSA_KV_HEADS, 2 * NSA_HEAD_DIM, T), BF16),
            jax.ShapeDtypeStruct((T, NSA_KAUG), BF16), jax.ShapeDtypeStruct((T, NSA_KV), BF16),
            jax.ShapeDtypeStruct((T, 2 * NSA_KV), F32), jax.ShapeDtypeStruct((T, 3 * GDN_W), F32),
            jax.ShapeDtypeStruct((T, GDN_W), F32), jax.ShapeDtypeStruct((T, LANES), F32),
        ],
        compiler_params=_cparams("parallel"),
        name="in_projection",
    )(x2d, w_t, w_n, cos, sin, cos_t, sin_t)


def _compress_kernel(k_ref, v_ref, wk_ref, wv_ref, pe_ref, cos_ref, sin_ref, kc_ref, vc_ref):
    nb = k_ref.shape[2]
    k = k_ref[0, 0]
    v = v_ref[0, 0]
    klo = (k + pe_ref[0:1, :]).astype(BF16)
    khi = (k + pe_ref[1:2, :]).astype(BF16)
    vlo = (v + pe_ref[2:3, :]).astype(BF16)
    vhi = (v + pe_ref[3:4, :]).astype(BF16)

    def up(m):
        return pltpu.roll(m, nb - 1, axis=0)

    a = _dot(klo, wk_ref[0]) + up(_dot(khi, wk_ref[1]))
    ar = _dot(klo, wk_ref[2]) + up(_dot(khi, wk_ref[3]))
    kc_ref[0] = a * cos_ref[...] + ar * sin_ref[...]
    vc_ref[0] = _dot(vlo, wv_ref[0]) + up(_dot(vhi, wv_ref[1]))


def _compress(cmp2d, pe_k, w_k, pe_v, w_v, B, S):
    nb = S // CMP_STRIDE
    bh = B * NSA_KV_HEADS
    row_w = CMP_STRIDE * NSA_HEAD_DIM
    c = cmp2d.reshape(B, S, 2, NSA_KV_HEADS, NSA_HEAD_DIM).transpose(2, 0, 3, 1, 4)
    c = c.reshape(2, bh, nb, row_w)
    wk = w_k.reshape(2, row_w, NSA_HEAD_DIM)
    wk_all = jnp.concatenate([wk, jnp.stack([_rot_cols(wk[0]), _rot_cols(wk[1])])], 0).astype(BF16)
    wv_all = w_v.reshape(2, row_w, NSA_HEAD_DIM).astype(BF16)
    pe = jnp.concatenate([pe_k.reshape(2, row_w), pe_v.reshape(2, row_w)], 0)
    cos, sin = _rope_tables(jnp.arange(nb) * CMP_STRIDE + (CMP_BLOCK - 1) / 2, 1)
    full = lambda shape: pl.BlockSpec(shape, lambda i: (0,) * len(shape))
    kc, vc = pl.pallas_call(
        _compress_kernel,
        grid=(bh,),
        in_specs=[
            pl.BlockSpec((1, 1, nb, row_w), lambda i: (0, i, 0, 0)),
            pl.BlockSpec((1, 1, nb, row_w), lambda i: (1, i, 0, 0)),
            full((4, row_w, NSA_HEAD_DIM)), full((2, row_w, NSA_HEAD_DIM)), full((4, row_w)),
            full((nb, NSA_HEAD_DIM)), full((nb, NSA_HEAD_DIM)),
        ],
        out_specs=[pl.BlockSpec((1, nb, NSA_HEAD_DIM), lambda i: (i, 0, 0))] * 2,
        out_shape=[jax.ShapeDtypeStruct((bh, nb, NSA_HEAD_DIM), F32)] * 2,
        compiler_params=_cparams("parallel"),
        name="nsa_compress",
    )(c, c, wk_all, wv_all, pe, cos, sin)
    return kc, vc


NSA_QB = 512
SEL_KT = 1024
SEL_SUB = 512
SEL_BLOCKS = SEL_SUB // SLC_BLOCK
NSA_KAUG = 2 * LANES


def _nsa_kernel(qt_ref, kc_ref, vct_ref, ovt_ref, ksa_ref, vst_ref, kw_ref, vwt_ref, gate_ref, o_ref,
                sel_scr, s_a, s_b, p_a, p_b, acc_a, acc_b, *, n_sel, kt_w, win_w):
    G, QB, D = NSA_GROUP, NSA_QB, NSA_HEAD_DIM
    GQ = G * QB
    nb = kc_ref.shape[2]
    ns = ovt_ref.shape[0]
    h = pl.program_id(1)
    qb = pl.program_id(2)
    s0 = qb * QB
    q4 = jnp.concatenate([qt_ref[g] for g in range(G)], axis=1)
    zeros = jnp.zeros_like(q4)
    qpad = jnp.concatenate([jnp.where(h == hh, q4, zeros) for hh in range(NSA_KV_HEADS)], axis=0)
    tq = s0 + lax.broadcasted_iota(jnp.int32, (1, QB), 1)
    lanes4 = lambda a: jnp.concatenate([a] * G, axis=1)

    s_c = _dot(kc_ref[0, 0], q4)
    nrow = lax.broadcasted_iota(jnp.int32, (nb, QB), 0)
    mask_c = lanes4((nrow * CMP_STRIDE + (CMP_BLOCK - 1)) <= tq)
    s_c = jnp.where(mask_c, s_c, NEG)
    m_c = jnp.max(s_c, 0, keepdims=True)
    e_c = jnp.where(mask_c, jnp.exp2(s_c - m_c), 0.0)
    l_c = jnp.sum(e_c, 0, keepdims=True)
    p_c = e_c / jnp.where(l_c > 0, l_c, 1.0)
    o_c = _dot(vct_ref[0, 0], p_c.astype(BF16))

    psum = p_c[:, 0:QB]
    for g in range(1, G):
        psum = psum + p_c[:, g * QB:(g + 1) * QB]
    p_hi = psum.astype(BF16)
    p_lo = (psum - p_hi.astype(F32)).astype(BF16)
    imp_t = _dot(ovt_ref[...], p_hi) + _dot(ovt_ref[...], p_lo)

    w0 = pl.multiple_of(jnp.maximum(s0 + QB - win_w, 0), QB)
    s_w = _dot(kw_ref[0, pl.ds(w0, win_w), :], qpad)
    rel = tq - (w0 + lax.broadcasted_iota(jnp.int32, (win_w, QB), 0))
    s_w = s_w + lanes4(jnp.where((rel >= 0) & (rel < WINDOW), 0.0, NEG))
    p_w = jnp.exp2((s_w - jnp.max(s_w, 0, keepdims=True)).astype(BF16))
    acc_w = _dot(vwt_ref[0, :, pl.ds(w0, win_w)], p_w)
    o_w = acc_w[:D] / acc_w[D:D + 1]

    jrow = lax.broadcasted_iota(jnp.int32, (ns, QB), 0)
    tcol = s0 + lax.broadcasted_iota(jnp.int32, (ns, QB), 1)
    cur = tcol // SLC_BLOCK
    forced = (jrow == 0) | ((jrow <= cur) & (jrow > cur - SLC_LOCAL))
    score = jnp.where((jrow * SLC_BLOCK) <= tcol, jnp.where(forced, -2.0, imp_t), -1.0)
    jf = jrow.astype(F32)
    for _ in range(n_sel - 1 - SLC_LOCAL):
        mx = jnp.max(score, 0, keepdims=True)
        first = jnp.min(jnp.where(score == mx, jf, float(ns)), 0, keepdims=True)
        score = jnp.where(jf == first, -2.0, score)
    sel_scr[...] = jnp.where(score == -2.0, 1.0, 0.0)

    krow = lax.broadcasted_iota(jnp.int32, (SEL_SUB, QB), 0)
    n_kt = (s0 + QB + kt_w - 1) // kt_w
    pad_rows = ksa_ref.shape[2] - qpad.shape[0] - 2 * SEL_BLOCKS

    def q_with_mask(k0):
        chosen = sel_scr[pl.ds(pl.multiple_of(k0 // SLC_BLOCK, SEL_BLOCKS), SEL_BLOCKS), :]
        bias = jnp.concatenate([jnp.where(chosen > 0.5, 0.0, NEG), jnp.zeros((SEL_BLOCKS, QB), F32)], axis=0)
        return jnp.concatenate([qpad, lanes4(bias).astype(BF16), jnp.zeros((pad_rows, GQ), BF16)], axis=0)

    def score_into(k0, s_ref):
        s_ref[...] = _dot(ksa_ref[0, pl.ds(k0, SEL_SUB), :], q_with_mask(k0))

    def softmax_into(s_ref, p_ref, k0, m, causal):
        if causal:
            bias = lanes4(jnp.where((k0 + krow) <= tq, 0.0, NEG))
            s_ref[...] = s_ref[...] + bias
        m_new = jnp.maximum(m, jnp.max(s_ref[...], 0, keepdims=True))
        p_ref[...] = jnp.exp2((s_ref[...] - m_new).astype(BF16))
        return jnp.exp2(m - m_new), m_new

    def values_into(k0, p_ref, acc_ref, alpha):
        acc_ref[...] = alpha * acc_ref[...] + _dot(vst_ref[0, :, pl.ds(k0, SEL_SUB)], p_ref[...])

    def sel_trip(t, carry, last):
        m_a, m_b, alpha_b = carry
        k_a = pl.multiple_of(t * kt_w, kt_w)
        k_b = pl.multiple_of(k_a + SEL_SUB, SEL_SUB)
        score_into(k_b, s_b)
        alpha_a, m_a = softmax_into(s_a, p_a, k_a, m_a, last)
        values_into(pl.multiple_of(jnp.maximum(k_b - kt_w, SEL_SUB), SEL_SUB), p_b, acc_b, alpha_b)
        if not last:
            score_into(pl.multiple_of(k_a + kt_w, kt_w), s_a)
        alpha_b, m_b = softmax_into(s_b, p_b, k_b, m_b, last)
        values_into(k_a, p_a, acc_a, alpha_a)
        return m_a, m_b, alpha_b

    acc_a[...] = jnp.zeros_like(acc_a)
    acc_b[...] = jnp.zeros_like(acc_b)
    p_b[...] = jnp.zeros_like(p_b)
    score_into(0, s_a)
    neg_row = jnp.full((1, GQ), NEG, F32)
    carry = lax.fori_loop(0, n_kt - 1, functools.partial(sel_trip, last=False),
                          (neg_row, neg_row, jnp.ones((1, GQ), F32)))
    m_a, m_b, alpha_b = sel_trip(n_kt - 1, carry, True)
    values_into(pl.multiple_of((n_kt - 1) * kt_w + SEL_SUB, SEL_SUB), p_b, acc_b, alpha_b)
    m_s = jnp.maximum(m_a, m_b)
    acc_s = jnp.exp2(m_a - m_s) * acc_a[...] + jnp.exp2(m_b - m_s) * acc_b[...]
    o_s = acc_s[:D] / acc_s[D:D + 1]

    gate = _sigmoid(gate_ref[0, 0])
    for g in range(G):
        cols = slice(g * QB, (g + 1) * QB)
        o_ref[g] = (gate[3 * g:3 * g + 1] * o_c[:, cols] + gate[3 * g + 1:3 * g + 2] * o_s[:, cols]
                    + gate[3 * g + 2:3 * g + 3] * o_w[:, cols]).astype(BF16)


def _nsa_attention(qt, ks, vst, kw, vwt, kc, vc, small, B, S):
    H, HKV, G, D = NSA_HEADS, NSA_KV_HEADS, NSA_GROUP, NSA_HEAD_DIM
    nb = S // CMP_STRIDE
    ns = S // SLC_BLOCK
    nq = S // NSA_QB
    n_sel = min(SLC_TOP_N, ns)
    kt_w = min(SEL_KT, S)
    win_w = min(WINDOW + NSA_QB, S)
    kcb = kc.reshape(B, HKV, nb, D).astype(BF16)
    vct = vc.reshape(B, HKV, nb, D).transpose(0, 1, 3, 2).astype(BF16)
    gates = small[:, :3 * H].reshape(B, S, HKV, 3 * G).transpose(0, 2, 3, 1)
    m_i = jnp.arange(ns)[:, None]
    n_i = jnp.arange(nb)[None, :]
    overlap_t = ((n_i * CMP_STRIDE <= m_i * SLC_BLOCK + SLC_BLOCK - 1)
                 & (n_i * CMP_STRIDE + CMP_BLOCK - 1 >= m_i * SLC_BLOCK)).astype(BF16)
    per_bh = lambda shape: pl.BlockSpec((1, 1) + shape, lambda b, h, i: (b, h, 0, 0))
    keys = pl.BlockSpec((1, S, NSA_KV), lambda b, h, i: (b, 0, 0))
    keys_aug = pl.BlockSpec((1, S, NSA_KAUG), lambda b, h, i: (b, 0, 0))
    vals = pl.BlockSpec((1, 2 * D, S), lambda b, h, i: (h, 0, b))
    heads = pl.BlockSpec((G, D, NSA_QB), lambda b, h, i: (h, 0, b * nq + i))
    return pl.pallas_call(
        functools.partial(_nsa_kernel, n_sel=n_sel, kt_w=kt_w, win_w=win_w),
        grid=(B, HKV, nq),
        in_specs=[
            heads, per_bh((nb, D)), per_bh((D, nb)),
            pl.BlockSpec((ns, nb), lambda b, h, i: (0, 0)),
            keys_aug, vals, keys, vals,
            pl.BlockSpec((1, 1, 3 * G, NSA_QB), lambda b, h, i: (b, h, 0, i)),
        ],
        out_specs=heads,
        out_shape=jax.ShapeDtypeStruct((H, D, B * S), BF16),
        scratch_shapes=[pltpu.VMEM((ns, NSA_QB), F32)]
        + [pltpu.VMEM((SEL_SUB, G * NSA_QB), F32)] * 2 + [pltpu.VMEM((SEL_SUB, G * NSA_QB), BF16)] * 2
        + [pltpu.VMEM((2 * D, G * NSA_QB), F32)] * 2,
        compiler_params=_cparams("parallel", "parallel", "arbitrary"),
        name="nsa_attention",
    )(qt, kcb, vct, overlap_t, ks.reshape(B, S, NSA_KAUG), vst, kw.reshape(B, S, NSA_KV), vwt, gates)


GDN_TB = 512
GDN_HALO = 8


def _gdn_prep_kernel(qkv_ref, halo_ref, sm_ref, cw_ref, alog_ref, dtb_ref,
                     u_ref, wq_ref, kd_ref, intra_ref, dec_ref, xs_scr, y_scr, bg_scr):
    C, HD, NH, BD = GDN_CHUNK, GDN_HEAD_DIM, GDN_HEADS, GDN_BD
    tb = qkv_ref.shape[1]
    first = pl.program_id(1) == 0
    xs_scr[GDN_HALO:GDN_HALO + tb, :] = qkv_ref[0]
    xs_scr[0:GDN_HALO, :] = jnp.where(first, 0.0, halo_ref[0])
    for cb in range(3 * GDN_W // LANES):
        cols = slice(cb * LANES, (cb + 1) * LANES)
        y = cw_ref[GDN_CONV - 1:GDN_CONV, cols] * xs_scr[GDN_HALO:GDN_HALO + tb, cols]
        for d in range(1, GDN_CONV):
            y = y + cw_ref[GDN_CONV - 1 - d:GDN_CONV - d, cols] * xs_scr[GDN_HALO - d:GDN_HALO - d + tb, cols]
        y_scr[:, cols] = _silu(y)
    sm = sm_ref[0]
    b_logit = sm[:, 3 * NSA_HEADS:3 * NSA_HEADS + NH]
    a_logit = sm[:, 3 * NSA_HEADS + NH:3 * NSA_HEADS + 2 * NH] + dtb_ref[...]
    softplus = jnp.maximum(a_logit, 0.0) + jnp.log(1.0 + jnp.exp(-jnp.abs(a_logit)))
    bg_scr[:, 0:NH] = _sigmoid(b_logit)
    bg_scr[:, NH:2 * NH] = -jnp.exp(alog_ref[...]) * softplus

    r = lax.broadcasted_iota(jnp.int32, (BD, BD), 0)
    c = lax.broadcasted_iota(jnp.int32, (BD, BD), 1)
    same = (r // C) == (c // C)
    eye = r == c
    tril = same & (r >= c)
    triu = same & (r <= c)
    strict = same & (r > c)
    eye_f = jnp.where(eye, 1.0, 0.0)

    def stack_heads(m, c0):
        return jnp.concatenate([m[:, c0 + h * HD:c0 + (h + 1) * HD] for h in range(NH)], axis=0)

    def chunk(ci, carry):
        r0 = pl.multiple_of(ci * C, C)
        yc = y_scr[pl.ds(r0, C), :]
        bg = bg_scr[pl.ds(r0, C), :]
        q4 = stack_heads(yc, 0)
        k4 = stack_heads(yc, GDN_W)
        v4 = stack_heads(yc, 2 * GDN_W)
        q4 = q4 * lax.rsqrt(jnp.sum(q4 * q4, -1, keepdims=True) + NORM_EPS) * (HD ** -0.5)
        k4 = k4 * lax.rsqrt(jnp.sum(k4 * k4, -1, keepdims=True) + NORM_EPS)
        b4 = jnp.concatenate([bg[:, h:h + 1] for h in range(NH)], axis=0)
        g4 = jnp.concatenate([bg[:, NH + h:NH + h + 1] for h in range(NH)], axis=0)
        g_row = jnp.sum(jnp.where(eye, g4, 0.0), axis=0, keepdims=True)
        gc_col = jnp.sum(jnp.where(tril, g_row, 0.0), axis=1, keepdims=True)
        gc_row = jnp.sum(jnp.where(triu, g4, 0.0), axis=0, keepdims=True)
        gl_col = jnp.sum(jnp.where(same, g_row, 0.0), axis=1, keepdims=True)
        decay = jnp.where(tril, jnp.exp(jnp.minimum(gc_col - gc_row, 0.0)), 0.0)
        kb4 = k4 * b4
        k4b = k4.astype(BF16)
        lmat = jnp.where(strict, _dot_nt(kb4.astype(BF16), k4b) * decay, 0.0)
        intra = jnp.where(tril, _dot_nt(q4.astype(BF16), k4b) * decay, 0.0)
        inv = eye_f - lmat
        pw = lmat
        for _ in range(int(math.log2(C)) - 1):
            pwb = pw.astype(BF16)
            pw = _dot(pwb, pwb)
            inv = _dot2(inv, eye_f + pw)
        egc = jnp.exp(gc_col)
        uw = _dot2(inv, jnp.concatenate([v4 * b4, kb4 * egc], axis=1))
        u_ref[0, ci] = uw[:, :HD]
        w4 = uw[:, HD:].astype(BF16).reshape(NH, C, HD)
        qg4 = (q4 * egc).astype(BF16).reshape(NH, C, HD)
        wq_ref[0, ci] = jnp.concatenate([w4, qg4], axis=1)
        kd_ref[0, ci] = (k4 * jnp.exp(gl_col - gc_col)).astype(BF16)
        intra_ref[0, ci] = intra.astype(BF16)
        dec = jnp.broadcast_to(jnp.exp(gl_col), (BD, LANES)).reshape(NH, C, LANES)
        dec_ref[0, ci] = dec[:, 0:8, :]
        return carry

    for ci in range(tb // C):
        chunk(ci, 0)


def _gdn_scan_kernel(u_ref, wq_ref, kd_ref, intra_ref, dec_ref, z_ref, nw_ref, o_ref, s_scr):
    C, HD, NH = GDN_CHUNK, GDN_HEAD_DIM, GDN_HEADS
    nbatch = u_ref.shape[0]

    @pl.when(pl.program_id(0) == 0)
    def _():
        s_scr[...] = jnp.zeros_like(s_scr)

    nw = nw_ref[...]
    for b in range(nbatch):
        vnew, qs = [], []
        for h in range(NH):
            st = s_scr[b * NH + h]
            res = _dot(wq_ref[b, 0, h], st.astype(BF16))
            vnew.append(u_ref[b, 0, h * C:(h + 1) * C, :] - res[:C])
            qs.append(res[C:])
        vnew4 = jnp.concatenate(vnew, axis=0).astype(BF16)
        o4 = jnp.concatenate(qs, axis=0) + _dot(intra_ref[b, 0], vnew4)
        for h in range(NH):
            rows = slice(h * C, (h + 1) * C)
            st = s_scr[b * NH + h]
            s_scr[b * NH + h] = st * dec_ref[b, 0, h, 0:1, :] + _dot_tn(kd_ref[b, 0, rows, :], vnew4[rows])
            oh = o4[rows]
            oh = oh * lax.rsqrt(jnp.mean(oh * oh, -1, keepdims=True) + NORM_EPS) * nw
            o_ref[b, :, h * HD:(h + 1) * HD] = oh * _silu(z_ref[b, :, h * HD:(h + 1) * HD])


def _gdn_mixer(gqkv, z, small, conv_w, a_log, dt_bias, norm_w, B, S):
    C, HD, NH, BD = GDN_CHUNK, GDN_HEAD_DIM, GDN_HEADS, GDN_BD
    nc = S // C
    tb = min(GDN_TB, S)
    cb = tb // C
    qkv3 = gqkv.reshape(B, S, 3 * GDN_W)
    sm3 = small.reshape(B, S, LANES)
    hb = tb // GDN_HALO
    u, wq, kd, intra, dec = pl.pallas_call(
        _gdn_prep_kernel,
        grid=(B, S // tb),
        in_specs=[
            pl.BlockSpec((1, tb, 3 * GDN_W), lambda b, i: (b, i, 0)),
            pl.BlockSpec((1, GDN_HALO, 3 * GDN_W), lambda b, i: (b, jnp.maximum(i * hb - 1, 0), 0)),
            pl.BlockSpec((1, tb, LANES), lambda b, i: (b, i, 0)),
            pl.BlockSpec((GDN_CONV, 3 * GDN_W), lambda b, i: (0, 0)),
            pl.BlockSpec((1, NH), lambda b, i: (0, 0)),
            pl.BlockSpec((1, NH), lambda b, i: (0, 0)),
        ],
        out_specs=[
            pl.BlockSpec((1, cb, BD, HD), lambda b, i: (b, i, 0, 0)),
            pl.BlockSpec((1, cb, NH, 2 * C, HD), lambda b, i: (b, i, 0, 0, 0)),
            pl.BlockSpec((1, cb, BD, HD), lambda b, i: (b, i, 0, 0)),
            pl.BlockSpec((1, cb, BD, BD), lambda b, i: (b, i, 0, 0)),
            pl.BlockSpec((1, cb, NH, 8, LANES), lambda b, i: (b, i, 0, 0, 0)),
        ],
        out_shape=[
            jax.ShapeDtypeStruct((B, nc, BD, HD), F32),
            jax.ShapeDtypeStruct((B, nc, NH, 2 * C, HD), BF16),
            jax.ShapeDtypeStruct((B, nc, BD, HD), BF16),
            jax.ShapeDtypeStruct((B, nc, BD, BD), BF16),
            jax.ShapeDtypeStruct((B, nc, NH, 8, LANES), F32),
        ],
        scratch_shapes=[
            pltpu.VMEM((GDN_HALO + tb, 3 * GDN_W), F32),
            pltpu.VMEM((tb, 3 * GDN_W), F32),
            pltpu.VMEM((tb, 2 * NH), F32),
        ],
        compiler_params=_cparams("parallel", "parallel"),
        name="gdn_prepare",
    )(qkv3, qkv3, sm3, conv_w, a_log.reshape(1, NH), dt_bias.reshape(1, NH))

    o = pl.pallas_call(
        _gdn_scan_kernel,
        grid=(nc,),
        in_specs=[
            pl.BlockSpec((B, 1, BD, HD), lambda c: (0, c, 0, 0)),
            pl.BlockSpec((B, 1, NH, 2 * C, HD), lambda c: (0, c, 0, 0, 0)),
            pl.BlockSpec((B, 1, BD, HD), lambda c: (0, c, 0, 0)),
            pl.BlockSpec((B, 1, BD, BD), lambda c: (0, c, 0, 0)),
            pl.BlockSpec((B, 1, NH, 8, LANES), lambda c: (0, c, 0, 0, 0)),
            pl.BlockSpec((B, C, GDN_W), lambda c: (0, c, 0)),
            pl.BlockSpec((1, HD), lambda c: (0, 0)),
        ],
        out_specs=pl.BlockSpec((B, C, GDN_W), lambda c: (0, c, 0)),
        out_shape=jax.ShapeDtypeStruct((B, S, GDN_W), F32),
        scratch_shapes=[pltpu.VMEM((B * NH, HD, HD), F32)],
        compiler_params=_cparams("arbitrary"),
        name="gdn_scan",
    )(u, wq, kd, intra, dec, z.reshape(B, S, GDN_W), norm_w.reshape(1, HD))
    return o.reshape(B * S, GDN_W)


MIX_TM = 512


def _mix_ln_kernel(x_ref, at_ref, b_ref, w_ref, g_ref, be_ref, o_ref):
    a_t = at_ref[...].reshape(NSA_Q, x_ref.shape[0])
    mix = (_dot_tn(a_t, w_ref[0:NSA_Q, :])
           + _dot(b_ref[...].astype(BF16), w_ref[NSA_Q:NSA_Q + GDN_W, :]))
    o_ref[...] = _layer_norm(DN_ALPHA * x_ref[...] + mix, g_ref[...], be_ref[...])


def _mix_out_ln(x2d, o_at, o_b, w_out, g, b):
    T = x2d.shape[0]
    row = lambda i: (i, 0)
    const = lambda i: (0, 0)
    return pl.pallas_call(
        _mix_ln_kernel,
        grid=(T // MIX_TM,),
        in_specs=[
            pl.BlockSpec((MIX_TM, D_MODEL), row),
            pl.BlockSpec((NSA_HEADS, NSA_HEAD_DIM, MIX_TM), lambda i: (0, 0, i)),
            pl.BlockSpec((MIX_TM, GDN_W), row), pl.BlockSpec((NSA_Q + GDN_W, D_MODEL), const),
            pl.BlockSpec((1, D_MODEL), const), pl.BlockSpec((1, D_MODEL), const),
        ],
        out_specs=pl.BlockSpec((MIX_TM, D_MODEL), row),
        out_shape=jax.ShapeDtypeStruct((T, D_MODEL), F32),
        compiler_params=_cparams("parallel"),
        name="mix_out_ln",
    )(x2d, o_at, o_b, w_out.astype(BF16), g.reshape(1, D_MODEL), b.reshape(1, D_MODEL))


FFN_TM = 512
FFN_TF = 256


def _ffn_ln_kernel(x_ref, wg_ref, wu_ref, wd_ref, g_ref, be_ref, o_ref, hid_scr):
    x = x_ref[...]
    xb = x.astype(BF16)
    for j in range(hid_scr.shape[1] // FFN_TF):
        cols = slice(j * FFN_TF, (j + 1) * FFN_TF)
        hid_scr[:, cols] = (_silu(_dot(xb, wg_ref[:, cols])) * _dot(xb, wu_ref[:, cols])).astype(BF16)
    o_ref[...] = _layer_norm(DN_ALPHA * x + _dot(hid_scr[...], wd_ref[...]), g_ref[...], be_ref[...])


def _ffn_ln(x2d, wg, wu, wd, g, b):
    T = x2d.shape[0]
    dff = wg.shape[1]
    tm = min(FFN_TM, T)
    const = lambda i: (0, 0)
    resident = lambda shape: pl.BlockSpec(shape, const, pipeline_mode=pl.Buffered(1))
    return pl.pallas_call(
        _ffn_ln_kernel,
        grid=(T // tm,),
        in_specs=[
            pl.BlockSpec((tm, D_MODEL), lambda i: (i, 0)),
            resident((D_MODEL, dff)), resident((D_MODEL, dff)), resident((dff, D_MODEL)),
            pl.BlockSpec((1, D_MODEL), const), pl.BlockSpec((1, D_MODEL), const),
        ],
        out_specs=pl.BlockSpec((tm, D_MODEL), lambda i: (i, 0)),
        out_shape=jax.ShapeDtypeStruct((T, D_MODEL), F32),
        scratch_shapes=[pltpu.VMEM((tm, dff), BF16)],
        compiler_params=_cparams("parallel"),
        name="ffn_ln",
    )(x2d, wg.astype(BF16), wu.astype(BF16), wd.astype(BF16), g.reshape(1, D_MODEL), b.reshape(1, D_MODEL))


POOL_TM = 512
POOL_HALO = 16


def _route_top2(x1, rw):
    lane = lax.broadcasted_iota(jnp.int32, (x1.shape[0], LANES), 1)
    lanef = lane.astype(F32)
    logits = jnp.where(lane < N_EXPERTS, _dot3(x1, rw), NEG)
    ex = jnp.exp(logits - jnp.max(logits, -1, keepdims=True))
    probs = jnp.where(lane < N_EXPERTS, ex / jnp.sum(ex, -1, keepdims=True), -1.0)
    p1 = jnp.max(probs, -1, keepdims=True)
    i1 = jnp.min(jnp.where(probs == p1, lanef, float(LANES)), -1, keepdims=True)
    rest = jnp.where(lanef == i1, -1.0, probs)
    p2 = jnp.max(rest, -1, keepdims=True)
    i2 = jnp.min(jnp.where(rest == p2, lanef, float(LANES)), -1, keepdims=True)
    den = p1 + p2
    return jnp.where(lane == 0, i1, jnp.where(lane == 1, i2, jnp.where(
        lane == 2, p1 / den, jnp.where(lane == 3, p2 / den, 0.0))))


def _pool_ln_kernel(x_ref, halo_ref, pw_ref, ps_ref, g_ref, be_ref, rw_ref, o_ref, route_ref, xs_scr):
    tm = x_ref.shape[1]
    i = pl.program_id(1)
    x = x_ref[0]
    xs_scr[POOL_HALO:POOL_HALO + tm, :] = x
    xs_scr[0:POOL_HALO, :] = jnp.where(i == 0, 0.0, halo_ref[0])
    t1 = (i * tm + 1 + lax.broadcasted_iota(jnp.int32, (tm, 1), 0)).astype(F32)
    for gi, win in enumerate(POOL_SIZES):
        cols = slice(gi * POOL_GROUP, (gi + 1) * POOL_GROUP)
        acc = x[:, cols]
        for d in range(1, win):
            acc = acc + xs_scr[POOL_HALO - d:POOL_HALO - d + tm, cols]
        mean = acc / jnp.minimum(t1, float(win))
        mix = _dot((mean - x[:, cols]).astype(BF16), pw_ref[gi]) * ps_ref[:, cols]
        xs_scr[POOL_HALO:POOL_HALO + tm, cols] = DN_ALPHA * x[:, cols] + mix
    x1 = _layer_norm(xs_scr[POOL_HALO:POOL_HALO + tm, :], g_ref[...], be_ref[...])
    o_ref[0] = x1
    route_ref[0] = _route_top2(x1, rw_ref[...])


def _pool_ln(x3d, pool_w, pool_scale, g, b, router_w):
    B, S, _ = x3d.shape
    tm = min(POOL_TM, S)
    hb = tm // POOL_HALO
    const = lambda bb, i: (0, 0)
    rw = jnp.concatenate([router_w, jnp.zeros((D_MODEL, LANES - N_EXPERTS), F32)], axis=1)
    return pl.pallas_call(
        _pool_ln_kernel,
        grid=(B, S // tm),
        in_specs=[
            pl.BlockSpec((1, tm, D_MODEL), lambda bb, i: (bb, i, 0)),
            pl.BlockSpec((1, POOL_HALO, D_MODEL), lambda bb, i: (bb, jnp.maximum(i * hb - 1, 0), 0)),
            pl.BlockSpec((len(POOL_SIZES), POOL_GROUP, POOL_GROUP), lambda bb, i: (0, 0, 0)),
            pl.BlockSpec((1, D_MODEL), const), pl.BlockSpec((1, D_MODEL), const),
            pl.BlockSpec((1, D_MODEL), const), pl.BlockSpec((D_MODEL, LANES), const),
        ],
        out_specs=[pl.BlockSpec((1, tm, D_MODEL), lambda bb, i: (bb, i, 0)),
                   pl.BlockSpec((1, tm, LANES), lambda bb, i: (bb, i, 0))],
        out_shape=[jax.ShapeDtypeStruct((B, S, D_MODEL), F32), jax.ShapeDtypeStruct((B, S, LANES), F32)],
        scratch_shapes=[pltpu.VMEM((POOL_HALO + tm, D_MODEL), F32)],
        compiler_params=_cparams("parallel", "parallel"),
        name="pool_ln",
    )(x3d, x3d, pool_w.astype(BF16), pool_scale.reshape(1, D_MODEL), g.reshape(1, D_MODEL),
      b.reshape(1, D_MODEL), rw)


MOE_TM = 1024
MOE_TF = 512
MOE_TC = 512


def _moe_plan(route, T, tm):
    ne = N_EXPERTS
    e_flat = jnp.concatenate([route[:, 0], route[:, 1]]).astype(jnp.int32)
    onehot = (e_flat[:, None] == jnp.arange(ne)[None, :]).astype(jnp.int32)
    csum = jnp.cumsum(onehot, axis=0)
    rank = jnp.sum(onehot * csum, axis=1) - 1
    counts = csum[-1]
    padded = (counts + tm - 1) // tm * tm
    gend = jnp.cumsum(padded)
    gstart = gend - padded
    pos = gstart[e_flat] + rank
    n_tiles = (2 * T) // tm + ne
    tile_e = jnp.minimum(jnp.searchsorted(gend, jnp.arange(n_tiles) * tm, side="right"), ne - 1)
    tile_e = tile_e.astype(jnp.int32)
    n_used = (gend[-1] // tm).astype(jnp.int32).reshape(1)
    order = jnp.argsort(e_flat, stable=True)
    cstart = jnp.cumsum(counts) - counts
    e_p = jnp.repeat(tile_e, tm)
    r = jnp.clip(jnp.arange(n_tiles * tm) - gstart[e_p], 0, jnp.maximum(counts[e_p] - 1, 0))
    src = (order[jnp.clip(cstart[e_p] + r, 0, 2 * T - 1)] % T).astype(jnp.int32)
    return pos.astype(jnp.int32), src, tile_e, n_used, n_tiles


SUBLANES = 8


def _gather_rows(src_hbm, idx, dst, sem):
    def issue(blk, c):
        for u in range(SUBLANES):
            row = idx(blk * SUBLANES + u)
            pltpu.make_async_copy(src_hbm.at[row >> 3, pl.ds(row & (SUBLANES - 1), 1)],
                                  dst.at[blk, pl.ds(u, 1)], sem).start()
        return c

    lax.fori_loop(0, dst.shape[0], issue, 0)


def _gather_wait(src_hbm, dst, sem):
    pltpu.make_async_copy(src_hbm.at[pl.ds(0, dst.shape[0])], dst, sem).wait()


def _moe_expert_kernel(te_ref, nu_ref, src_ref, nxt_ref, x_hbm, wg_ref, wu_ref, wd_ref, o_ref,
                       xbuf, xb_scr, acc_scr, sem):
    i = pl.program_id(0)
    j = pl.program_id(1)
    n_used = nu_ref[0]
    used = i < n_used
    slot = i % 2

    @pl.when((i == 0) & (j == 0))
    def _():
        _gather_rows(x_hbm, lambda r: src_ref[0, 0, r], xbuf.at[0], sem.at[0])

    @pl.when((i + 1 < n_used) & (j == 0))
    def _():
        _gather_rows(x_hbm, lambda r: nxt_ref[0, 0, r], xbuf.at[1 - slot], sem.at[1 - slot])

    @pl.when(used & (j == 0))
    def _():
        _gather_wait(x_hbm, xbuf.at[slot], sem.at[slot])
        xb_scr[...] = xbuf[slot].reshape(xb_scr.shape).astype(BF16)
        acc_scr[...] = jnp.zeros_like(acc_scr)

    @pl.when(used)
    def _():
        xb = xb_scr[...]
        hid = _silu(_dot(xb, wg_ref[0])) * _dot(xb, wu_ref[0])
        acc_scr[...] += _dot(hid.astype(BF16), wd_ref[0])

    last = j == pl.num_programs(1) - 1

    @pl.when(used & last)
    def _():
        o_ref[...] = acc_scr[...]

    @pl.when(jnp.logical_not(used) & last)
    def _():
        o_ref[...] = jnp.zeros_like(o_ref)


def _moe_combine_kernel(pos_ref, nxt_ref, x_ref, route_ref, ys_hbm, g_ref, be_ref, o_ref, ybuf, sem):
    i = pl.program_id(0)
    slot = i % 2

    def fetch(p_ref, s):
        for k in range(2):
            _gather_rows(ys_hbm, lambda r: p_ref[0, k, r], ybuf.at[s, k], sem.at[s])

    @pl.when(i == 0)
    def _():
        fetch(pos_ref, 0)

    @pl.when(i + 1 < pl.num_programs(0))
    def _():
        fetch(nxt_ref, 1 - slot)

    for k in range(2):
        _gather_wait(ys_hbm, ybuf.at[slot, k], sem.at[slot])
    route = route_ref[...]
    y0 = ybuf[slot, 0].reshape(x_ref.shape)
    y1 = ybuf[slot, 1].reshape(x_ref.shape)
    y = DN_ALPHA * x_ref[...] + route[:, 2:3] * y0 + route[:, 3:4] * y1
    o_ref[...] = _layer_norm(y, g_ref[...], be_ref[...])


def _moe_ln(x2d, route, wg, wu, wd, g, b):
    T = x2d.shape[0]
    ne, _, dff = wg.shape
    tm = min(MOE_TM, T)
    tc = min(MOE_TC, T)
    pos, src, tile_e, n_used, n_tiles = _moe_plan(route, T, tm)

    nj = dff // MOE_TF

    def w_col(i, j, te, nu):
        return (te[i], 0, jnp.where(i < nu[0], j, 0))

    def w_row(i, j, te, nu):
        return (te[i], jnp.where(i < nu[0], j, 0), 0)

    src3 = src.reshape(n_tiles, 1, tm)
    ys = pl.pallas_call(
        _moe_expert_kernel,
        grid_spec=pltpu.PrefetchScalarGridSpec(
            num_scalar_prefetch=2,
            grid=(n_tiles, nj),
            in_specs=[
                pl.BlockSpec((1, 1, tm), lambda i, j, te, nu: (i, 0, 0), memory_space=pltpu.SMEM),
                pl.BlockSpec((1, 1, tm), lambda i, j, te, nu: (jnp.minimum(i + 1, n_tiles - 1), 0, 0),
                             memory_space=pltpu.SMEM),
                pl.BlockSpec(memory_space=pl.ANY),
                pl.BlockSpec((1, D_MODEL, MOE_TF), w_col),
                pl.BlockSpec((1, D_MODEL, MOE_TF), w_col),
                pl.BlockSpec((1, MOE_TF, D_MODEL), w_row),
            ],
            out_specs=pl.BlockSpec((tm, D_MODEL), lambda i, j, te, nu: (i, 0)),
            scratch_shapes=[pltpu.VMEM((2, tm // SUBLANES, SUBLANES, D_MODEL), F32),
                            pltpu.VMEM((tm, D_MODEL), BF16), pltpu.VMEM((tm, D_MODEL), F32),
                            pltpu.SemaphoreType.DMA((2,))],
        ),
        out_shape=jax.ShapeDtypeStruct((n_tiles * tm, D_MODEL), F32),
        compiler_params=_cparams("arbitrary", "arbitrary"),
        name="moe_experts",
    )(tile_e, n_used, src3, src3, x2d.reshape(T // SUBLANES, SUBLANES, D_MODEL), wg.astype(BF16),
      wu.astype(BF16), wd.astype(BF16))

    n_steps = T // tc
    pos3 = pos.reshape(2, n_steps, tc).transpose(1, 0, 2)
    row = lambda i: (i, 0)
    const = lambda i: (0, 0)
    return pl.pallas_call(
        _moe_combine_kernel,
        grid=(n_steps,),
        in_specs=[
            pl.BlockSpec((1, 2, tc), lambda i: (i, 0, 0), memory_space=pltpu.SMEM),
            pl.BlockSpec((1, 2, tc), lambda i: (jnp.minimum(i + 1, n_steps - 1), 0, 0),
                         memory_space=pltpu.SMEM),
            pl.BlockSpec((tc, D_MODEL), row), pl.BlockSpec((tc, LANES), row),
            pl.BlockSpec(memory_space=pl.ANY),
            pl.BlockSpec((1, D_MODEL), const), pl.BlockSpec((1, D_MODEL), const),
        ],
        out_specs=pl.BlockSpec((tc, D_MODEL), row),
        out_shape=jax.ShapeDtypeStruct((T, D_MODEL), F32),
        scratch_shapes=[pltpu.VMEM((2, 2, tc // SUBLANES, SUBLANES, D_MODEL), F32),
                        pltpu.SemaphoreType.DMA((2,))],
        compiler_params=_cparams("arbitrary"),
        name="moe_combine_ln",
    )(pos3, pos3, x2d, route, ys.reshape(n_tiles * tm // SUBLANES, SUBLANES, D_MODEL),
      g.reshape(1, D_MODEL), b.reshape(1, D_MODEL))


def _even_layer(x2d, B, S, w_in, pe_k, w_k, pe_v, w_v, conv_w, a_log, dt_bias, gdn_norm, w_out,
                ln1_g, ln1_b, wg, wu, wd, ln2_g, ln2_b):
    qt, vst, vwt, ks, kw, cmp2d, gqkv, z, small = _in_projection(x2d, w_in, S)
    kc, vc = _compress(cmp2d, pe_k, w_k, pe_v, w_v, B, S)
    o_a = _nsa_attention(qt, ks, vst, kw, vwt, kc, vc, small, B, S)
    o_b = _gdn_mixer(gqkv, z, small, conv_w, a_log, dt_bias, gdn_norm, B, S)
    x2d = _mix_out_ln(x2d, o_a, o_b, w_out, ln1_g, ln1_b)
    return _ffn_ln(x2d, wg, wu, wd, ln2_g, ln2_b)


def _odd_layer(x2d, B, S, pool_w, pool_scale, ln1_g, ln1_b, router_w, wg, wu, wd, ln2_g, ln2_b):
    x3d, route = _pool_ln(x2d.reshape(B, S, D_MODEL), pool_w, pool_scale, ln1_g, ln1_b, router_w)
    return _moe_ln(x3d.reshape(B * S, D_MODEL), route.reshape(B * S, LANES), wg, wu, wd, ln2_g, ln2_b)


def kernel(x, ev_w_in, ev_cmp_pe_k, ev_cmp_w_k, ev_cmp_pe_v, ev_cmp_w_v, ev_conv_w, ev_a_log, ev_dt_bias,
           ev_gdn_norm, ev_w_out, ev_ln1_g, ev_ln1_b, ev_ffn_wg, ev_ffn_wu, ev_ffn_wd, ev_ln2_g, ev_ln2_b,
           od_pool_w, od_pool_scale, od_ln1_g, od_ln1_b, od_router_w, od_exp_wg, od_exp_wu, od_exp_wd,
           od_ln2_g, od_ln2_b):
    B, S, _ = x.shape
    h = x.reshape(B * S, D_MODEL)
    for layer in range(DEPTH):
        i = layer // 2
        if layer % 2 == 0:
            h = _even_layer(h, B, S, ev_w_in[i], ev_cmp_pe_k[i], ev_cmp_w_k[i], ev_cmp_pe_v[i],
                            ev_cmp_w_v[i], ev_conv_w[i], ev_a_log[i], ev_dt_bias[i], ev_gdn_norm[i],
                            ev_w_out[i], ev_ln1_g[i], ev_ln1_b[i], ev_ffn_wg[i], ev_ffn_wu[i],
                            ev_ffn_wd[i], ev_ln2_g[i], ev_ln2_b[i])
        else:
            h = _odd_layer(h, B, S, od_pool_w[i], od_pool_scale[i], od_ln1_g[i], od_ln1_b[i],
                           od_router_w[i], od_exp_wg[i], od_exp_wu[i], od_exp_wd[i], od_ln2_g[i],
                           od_ln2_b[i])
    return h.reshape(B, S, D_MODEL)
```

```python
import functools
import math

import jax
import jax.numpy as jnp
from jax import lax
from jax.experimental import pallas as pl
from jax.experimental.pallas import tpu as pltpu

F32 = jnp.float32
BF16 = jnp.bfloat16

D_MODEL = 1024
DEPTH = 2
NSA_HEADS = 8
NSA_KV_HEADS = 2
NSA_GROUP = NSA_HEADS // NSA_KV_HEADS
NSA_HEAD_DIM = 64
CMP_BLOCK = 32
CMP_STRIDE = 16
SLC_BLOCK = 64
SLC_TOP_N = 16
SLC_LOCAL = 2
WINDOW = 512
Q_BLOCK = 128
FORCED_SCORE = 1e6
GDN_HEADS = 4
GDN_HEAD_DIM = 128
GDN_CONV = 4
GDN_CHUNK = 64
POOL_SIZES = (2, 4, 8, 16)
POOL_GROUP = D_MODEL // 4
D_FF = 2816
N_EXPERTS = 8
D_FF_EXPERT = 3584
ROPE_THETA = 10000.0
LN_EPS = 1e-5
NORM_EPS = 1e-6
DN_ALPHA = (2 * DEPTH) ** 0.25

NSA_Q = NSA_HEADS * NSA_HEAD_DIM
NSA_KV = NSA_KV_HEADS * NSA_HEAD_DIM
GDN_W = GDN_HEADS * GDN_HEAD_DIM
GDN_BD = GDN_HEADS * GDN_CHUNK

LANES = 128
NEG = -1e30
LOG2E = 1.4426950408889634
VMEM_LIMIT = 56 * 1024 * 1024


def _cparams(*sem):
    return pltpu.CompilerParams(dimension_semantics=sem, vmem_limit_bytes=VMEM_LIMIT)


def _dot(a, b):
    return jnp.dot(a, b, preferred_element_type=F32)


def _dot_nt(a, b):
    return lax.dot_general(a, b, (((1,), (1,)), ((), ())), preferred_element_type=F32)


def _dot_tn(a, b):
    return lax.dot_general(a, b, (((0,), (0,)), ((), ())), preferred_element_type=F32)


def _dot3(a, b):
    ah = a.astype(BF16)
    al = (a - ah.astype(F32)).astype(BF16)
    bh = b.astype(BF16)
    bl = (b - bh.astype(F32)).astype(BF16)
    return _dot(ah, bh) + (_dot(ah, bl) + _dot(al, bh))


def _dot2(a, b):
    ah = a.astype(BF16)
    al = (a - ah.astype(F32)).astype(BF16)
    bh = b.astype(BF16)
    return _dot(ah, bh) + _dot(al, bh)


def _layer_norm(y, g, b):
    mu = jnp.mean(y, -1, keepdims=True)
    d = y - mu
    var = jnp.mean(d * d, -1, keepdims=True)
    return d * lax.rsqrt(var + LN_EPS) * g + b


def _sigmoid(x):
    return 1.0 / (1.0 + jnp.exp(-x))


def _silu(x):
    return x * _sigmoid(x)


IN_TM = 512
_R_Q, _R_QR, _R_VS, _R_VW, _R_END = 0, 512, 1024, 1152, 1280
_C_KS, _C_KSR, _C_KW, _C_KWR, _C_CMP, _C_GQKV, _C_Z, _C_SM, _C_END = 0, 128, 256, 384, 512, 768, 2304, 2816, 2944


def _inproj_kernel(x_ref, wt_ref, wn_ref, cos_ref, sin_ref, cost_ref, sint_ref,
                   qt_ref, vst_ref, vwt_ref, ks_ref, kw_ref, cmp_ref, gqkv_ref, z_ref, sm_ref):
    D, HKV = NSA_HEAD_DIM, NSA_KV_HEADS
    tm = x_ref.shape[0]
    xb = x_ref[...].astype(BF16)

    def mt(r0, n):
        return _dot_nt(wt_ref[r0:r0 + n, :], xb)

    def mm(c0, n):
        return _dot(xb, wn_ref[:, c0:c0 + n])

    scale = D ** -0.5 * LOG2E
    cost = cost_ref[...]
    sint = sint_ref[...]
    for h0 in range(0, NSA_HEADS, NSA_GROUP):
        a = mt(_R_Q + h0 * D, NSA_GROUP * D)
        r = mt(_R_QR + h0 * D, NSA_GROUP * D)
        for g in range(NSA_GROUP):
            rows = slice(g * D, (g + 1) * D)
            qt_ref[h0 + g] = ((a[rows] * cost + r[rows] * sint) * scale).astype(BF16)
    ones_rows = jnp.where(lax.broadcasted_iota(jnp.int32, (D, tm), 0) == 0, 1.0, 0.0).astype(BF16)
    for r0, out in ((_R_VS, vst_ref), (_R_VW, vwt_ref)):
        vt = mt(r0, NSA_KV).astype(BF16)
        for h in range(HKV):
            out[h, 0:D, :] = vt[h * D:(h + 1) * D]
            out[h, D:2 * D, :] = ones_rows
    cos = cos_ref[...]
    sin = sin_ref[...]
    ks_ref[:, 0:LANES] = (mm(_C_KS, LANES) * cos + mm(_C_KSR, LANES) * sin).astype(BF16)
    r_i = lax.broadcasted_iota(jnp.int32, (tm, NSA_KAUG - LANES), 0)
    c_i = lax.broadcasted_iota(jnp.int32, (tm, NSA_KAUG - LANES), 1)
    ks_ref[:, LANES:NSA_KAUG] = jnp.where((r_i // SLC_BLOCK) % SEL_BLOCKS == c_i, 1.0, 0.0).astype(BF16)
    kw_ref[...] = (mm(_C_KW, LANES) * cos + mm(_C_KWR, LANES) * sin).astype(BF16)
    cmp_ref[...] = mm(_C_CMP, 2 * NSA_KV)
    gqkv_ref[...] = mm(_C_GQKV, 3 * GDN_W)
    z_ref[...] = mm(_C_Z, GDN_W)
    sm_ref[...] = mm(_C_SM, LANES)


def _rot_cols(w):
    k, n = w.shape
    half = NSA_HEAD_DIM // 2
    w4 = w.reshape(k, n // NSA_HEAD_DIM, 2, half)
    return jnp.stack([-w4[:, :, 1], w4[:, :, 0]], axis=2).reshape(k, n)


def _rope_tables(pos, reps):
    half = NSA_HEAD_DIM // 2
    inv = jnp.power(ROPE_THETA, -jnp.arange(half, dtype=F32) / half)
    ang = pos.astype(F32)[:, None] * inv[None, :]
    cos = jnp.tile(jnp.cos(ang), (1, 2 * reps))
    sin = jnp.tile(jnp.sin(ang), (1, 2 * reps))
    return cos, sin


def _in_projection(x2d, w_in, S):
    T = x2d.shape[0]
    o = 0
    parts = {}
    for name, n in (("q", NSA_Q), ("kc", NSA_KV), ("vc", NSA_KV), ("ks", NSA_KV), ("vs", NSA_KV),
                    ("kw", NSA_KV), ("vw", NSA_KV), ("gl", 3 * NSA_HEADS), ("gq", GDN_W), ("gk", GDN_W),
                    ("gv", GDN_W), ("gz", GDN_W), ("gb", GDN_HEADS), ("ga", GDN_HEADS)):
        parts[name] = w_in[:, o:o + n]
        o += n
    pad = jnp.zeros((D_MODEL, LANES - 3 * NSA_HEADS - 2 * GDN_HEADS), F32)
    w_t = jnp.concatenate([parts["q"], _rot_cols(parts["q"]), parts["vs"], parts["vw"]], axis=1).T.astype(BF16)
    w_n = jnp.concatenate([
        parts["ks"], _rot_cols(parts["ks"]), parts["kw"], _rot_cols(parts["kw"]), parts["kc"], parts["vc"],
        parts["gq"], parts["gk"], parts["gv"], parts["gz"], parts["gl"], parts["gb"], parts["ga"], pad,
    ], axis=1).astype(BF16)
    assert w_t.shape[0] == _R_END and w_n.shape[1] == _C_END
    cos, sin = _rope_tables(jnp.arange(S), LANES // NSA_HEAD_DIM)
    cos_t, sin_t = cos[:, :NSA_HEAD_DIM].T, sin[:, :NSA_HEAD_DIM].T
    tm = min(IN_TM, S)
    nblk = S // tm
    row = lambda i: (i, 0)
    col3 = lambda i: (0, 0, i)
    const = lambda i: (0, 0)
    return pl.pallas_call(
        _inproj_kernel,
        grid=(T // tm,),
        in_specs=[
            pl.BlockSpec((tm, D_MODEL), row),
            pl.BlockSpec((_R_END, D_MODEL), const), pl.BlockSpec((D_MODEL, _C_END), const),
            pl.BlockSpec((tm, LANES), lambda i: (i % nblk, 0)),
            pl.BlockSpec((tm, LANES), lambda i: (i % nblk, 0)),
            pl.BlockSpec((NSA_HEAD_DIM, tm), lambda i: (0, i % nblk)),
            pl.BlockSpec((NSA_HEAD_DIM, tm), lambda i: (0, i % nblk)),
        ],
        out_specs=[
            pl.BlockSpec((NSA_HEADS, NSA_HEAD_DIM, tm), col3),
            pl.BlockSpec((NSA_KV_HEADS, 2 * NSA_HEAD_DIM, tm), col3),
            pl.BlockSpec((NSA_KV_HEADS, 2 * NSA_HEAD_DIM, tm), col3),
            pl.BlockSpec((tm, NSA_KAUG), row), pl.BlockSpec((tm, NSA_KV), row),
            pl.BlockSpec((tm, 2 * NSA_KV), row), pl.BlockSpec((tm, 3 * GDN_W), row),
            pl.BlockSpec((tm, GDN_W), row), pl.BlockSpec((tm, LANES), row),
        ],
        out_shape=[
            jax.ShapeDtypeStruct((NSA_HEADS, NSA_HEAD_DIM, T), BF16),
            jax.ShapeDtypeStruct((NSA_KV_HEADS, 2 * NSA_HEAD_DIM, T), BF16),
            jax.ShapeDtypeStruct((NSA_KV_HEADS, 2 * NSA_HEAD_DIM, T), BF16),
            jax.ShapeDtypeStruct((T, NSA_KAUG), BF16), jax.ShapeDtypeStruct((T, NSA_KV), BF16),
            jax.ShapeDtypeStruct((T, 2 * NSA_KV), F32), jax.ShapeDtypeStruct((T, 3 * GDN_W), F32),
            jax.ShapeDtypeStruct((T, GDN_W), F32), jax.ShapeDtypeStruct((T, LANES), F32),
        ],
        compiler_params=_cparams("parallel"),
        name="in_projection",
    )(x2d, w_t, w_n, cos, sin, cos_t, sin_t)


def _compress_kernel(k_ref, v_ref, wk_ref, wv_ref, pe_ref, cos_ref, sin_ref, kc_ref, vc_ref):
    nb = k_ref.shape[2]
    k = k_ref[0, 0]
    v = v_ref[0, 0]
    klo = (k + pe_ref[0:1, :]).astype(BF16)
    khi = (k + pe_ref[1:2, :]).astype(BF16)
    vlo = (v + pe_ref[2:3, :]).astype(BF16)
    vhi = (v + pe_ref[3:4, :]).astype(BF16)

    def up(m):
        return pltpu.roll(m, nb - 1, axis=0)

    a = _dot(klo, wk_ref[0]) + up(_dot(khi, wk_ref[1]))
    ar = _dot(klo, wk_ref[2]) + up(_dot(khi, wk_ref[3]))
    kc_ref[0] = a * cos_ref[...] + ar * sin_ref[...]
    vc_ref[0] = _dot(vlo, wv_ref[0]) + up(_dot(vhi, wv_ref[1]))


def _compress(cmp2d, pe_k, w_k, pe_v, w_v, B, S):
    nb = S // CMP_STRIDE
    bh = B * NSA_KV_HEADS
    row_w = CMP_STRIDE * NSA_HEAD_DIM
    c = cmp2d.reshape(B, S, 2, NSA_KV_HEADS, NSA_HEAD_DIM).transpose(2, 0, 3, 1, 4)
    c = c.reshape(2, bh, nb, row_w)
    wk = w_k.reshape(2, row_w, NSA_HEAD_DIM)
    wk_all = jnp.concatenate([wk, jnp.stack([_rot_cols(wk[0]), _rot_cols(wk[1])])], 0).astype(BF16)
    wv_all = w_v.reshape(2, row_w, NSA_HEAD_DIM).astype(BF16)
    pe = jnp.concatenate([pe_k.reshape(2, row_w), pe_v.reshape(2, row_w)], 0)
    cos, sin = _rope_tables(jnp.arange(nb) * CMP_STRIDE + (CMP_BLOCK - 1) / 2, 1)
    full = lambda shape: pl.BlockSpec(shape, lambda i: (0,) * len(shape))
    kc, vc = pl.pallas_call(
        _compress_kernel,
        grid=(bh,),
        in_specs=[
            pl.BlockSpec((1, 1, nb, row_w), lambda i: (0, i, 0, 0)),
            pl.BlockSpec((1, 1, nb, row_w), lambda i: (1, i, 0, 0)),
            full((4, row_w, NSA_HEAD_DIM)), full((2, row_w, NSA_HEAD_DIM)), full((4, row_w)),
            full((nb, NSA_HEAD_DIM)), full((nb, NSA_HEAD_DIM)),
        ],
        out_specs=[pl.BlockSpec((1, nb, NSA_HEAD_DIM), lambda i: (i, 0, 0))] * 2,
        out_shape=[jax.ShapeDtypeStruct((bh, nb, NSA_HEAD_DIM), F32)] * 2,
        compiler_params=_cparams("parallel"),
        name="nsa_compress",
    )(c, c, wk_all, wv_all, pe, cos, sin)
    return kc, vc


NSA_QB = 512
SEL_KT = 1024
SEL_SUB = 512
SEL_BLOCKS = SEL_SUB // SLC_BLOCK
NSA_KAUG = 2 * LANES


def _nsa_kernel(qt_ref, kc_ref, vct_ref, ovt_ref, ksa_ref, vst_ref, kw_ref, vwt_ref, gate_ref, o_ref,
                sel_scr, s_a, s_b, p_a, p_b, acc_a, acc_b, *, n_sel, kt_w, win_w):
    G, QB, D = NSA_GROUP, NSA_QB, NSA_HEAD_DIM
    GQ = G * QB
    nb = kc_ref.shape[2]
    ns = ovt_ref.shape[0]
    h = pl.program_id(1)
    qb = pl.program_id(2)
    s0 = qb * QB
    q4 = jnp.concatenate([qt_ref[g] for g in range(G)], axis=1)
    zeros = jnp.zeros_like(q4)
    qpad = jnp.concatenate([jnp.where(h == hh, q4, zeros) for hh in range(NSA_KV_HEADS)], axis=0)
    tq = s0 + lax.broadcasted_iota(jnp.int32, (1, QB), 1)
    lanes4 = lambda a: jnp.concatenate([a] * G, axis=1)

    s_c = _dot(kc_ref[0, 0], q4)
    nrow = lax.broadcasted_iota(jnp.int32, (nb, QB), 0)
    mask_c = lanes4((nrow * CMP_STRIDE + (CMP_BLOCK - 1)) <= tq)
    s_c = jnp.where(mask_c, s_c, NEG)
    m_c = jnp.max(s_c, 0, keepdims=True)
    e_c = jnp.where(mask_c, jnp.exp2(s_c - m_c), 0.0)
    l_c = jnp.sum(e_c, 0, keepdims=True)
    p_c = e_c / jnp.where(l_c > 0, l_c, 1.0)
    o_c = _dot(vct_ref[0, 0], p_c.astype(BF16))

    psum = p_c[:, 0:QB]
    for g in range(1, G):
        psum = psum + p_c[:, g * QB:(g + 1) * QB]
    p_hi = psum.astype(BF16)
    p_lo = (psum - p_hi.astype(F32)).astype(BF16)
    imp_t = _dot(ovt_ref[...], p_hi) + _dot(ovt_ref[...], p_lo)

    w0 = pl.multiple_of(jnp.maximum(s0 + QB - win_w, 0), QB)
    s_w = _dot(kw_ref[0, pl.ds(w0, win_w), :], qpad)
    rel = tq - (w0 + lax.broadcasted_iota(jnp.int32, (win_w, QB), 0))
    s_w = s_w + lanes4(jnp.where((rel >= 0) & (rel < WINDOW), 0.0, NEG))
    p_w = jnp.exp2((s_w - jnp.max(s_w, 0, keepdims=True)).astype(BF16))
    acc_w = _dot(vwt_ref[0, :, pl.ds(w0, win_w)], p_w)
    o_w = acc_w[:D] / acc_w[D:D + 1]

    jrow = lax.broadcasted_iota(jnp.int32, (ns, QB), 0)
    tcol = s0 + lax.broadcasted_iota(jnp.int32, (ns, QB), 1)
    cur = tcol // SLC_BLOCK
    forced = (jrow == 0) | ((jrow <= cur) & (jrow > cur - SLC_LOCAL))
    score = jnp.where((jrow * SLC_BLOCK) <= tcol, jnp.where(forced, -2.0, imp_t), -1.0)
    jf = jrow.astype(F32)
    for _ in range(n_sel - 1 - SLC_LOCAL):
        mx = jnp.max(score, 0, keepdims=True)
        first = jnp.min(jnp.where(score == mx, jf, float(ns)), 0, keepdims=True)
        score = jnp.where(jf == first, -2.0, score)
    sel_scr[...] = jnp.where(score == -2.0, 1.0, 0.0)

    krow = lax.broadcasted_iota(jnp.int32, (SEL_SUB, QB), 0)
    n_kt = (s0 + QB + kt_w - 1) // kt_w
    pad_rows = ksa_ref.shape[2] - qpad.shape[0] - 2 * SEL_BLOCKS

    def q_with_mask(k0):
        chosen = sel_scr[pl.ds(pl.multiple_of(k0 // SLC_BLOCK, SEL_BLOCKS), SEL_BLOCKS), :]
        bias = jnp.concatenate([jnp.where(chosen > 0.5, 0.0, NEG), jnp.zeros((SEL_BLOCKS, QB), F32)], axis=0)
        return jnp.concatenate([qpad, lanes4(bias).astype(BF16), jnp.zeros((pad_rows, GQ), BF16)], axis=0)

    def score_into(k0, s_ref):
        s_ref[...] = _dot(ksa_ref[0, pl.ds(k0, SEL_SUB), :], q_with_mask(k0))

    def softmax_into(s_ref, p_ref, k0, m, causal):
        if causal:
            bias = lanes4(jnp.where((k0 + krow) <= tq, 0.0, NEG))
            s_ref[...] = s_ref[...] + bias
        m_new = jnp.maximum(m, jnp.max(s_ref[...], 0, keepdims=True))
        p_ref[...] = jnp.exp2((s_ref[...] - m_new).astype(BF16))
        return jnp.exp2(m - m_new), m_new

    def values_into(k0, p_ref, acc_ref, alpha):
        acc_ref[...] = alpha * acc_ref[...] + _dot(vst_ref[0, :, pl.ds(k0, SEL_SUB)], p_ref[...])

    def sel_trip(t, carry, last):
        m_a, m_b, alpha_b = carry
        k_a = pl.multiple_of(t * kt_w, kt_w)
        k_b = pl.multiple_of(k_a + SEL_SUB, SEL_SUB)
        score_into(k_b, s_b)
        alpha_a, m_a = softmax_into(s_a, p_a, k_a, m_a, last)
        values_into(pl.multiple_of(jnp.maximum(k_b - kt_w, SEL_SUB), SEL_SUB), p_b, acc_b, alpha_b)
        if not last:
            score_into(pl.multiple_of(k_a + kt_w, kt_w), s_a)
        alpha_b, m_b = softmax_into(s_b, p_b, k_b, m_b, last)
        values_into(k_a, p_a, acc_a, alpha_a)
        return m_a, m_b, alpha_b

    acc_a[...] = jnp.zeros_like(acc_a)
    acc_b[...] = jnp.zeros_like(acc_b)
    p_b[...] = jnp.zeros_like(p_b)
    score_into(0, s_a)
    neg_row = jnp.full((1, GQ), NEG, F32)
    carry = lax.fori_loop(0, n_kt - 1, functools.partial(sel_trip, last=False),
                          (neg_row, neg_row, jnp.ones((1, GQ), F32)))
    m_a, m_b, alpha_b = sel_trip(n_kt - 1, carry, True)
    values_into(pl.multiple_of((n_kt - 1) * kt_w + SEL_SUB, SEL_SUB), p_b, acc_b, alpha_b)
    m_s = jnp.maximum(m_a, m_b)
    acc_s = jnp.exp2(m_a - m_s) * acc_a[...] + jnp.exp2(m_b - m_s) * acc_b[...]
    o_s = acc_s[:D] / acc_s[D:D + 1]

    gate = _sigmoid(gate_ref[0, 0])
    for g in range(G):
        cols = slice(g * QB, (g + 1) * QB)
        o_ref[g] = (gate[3 * g:3 * g + 1] * o_c[:, cols] + gate[3 * g + 1:3 * g + 2] * o_s[:, cols]
                    + gate[3 * g + 2:3 * g + 3] * o_w[:, cols]).astype(BF16)


def _nsa_attention(qt, ks, vst, kw, vwt, kc, vc, small, B, S):
    H, HKV, G, D = NSA_HEADS, NSA_KV_HEADS, NSA_GROUP, NSA_HEAD_DIM
    nb = S // CMP_STRIDE
    ns = S // SLC_BLOCK
    nq = S // NSA_QB
    n_sel = min(SLC_TOP_N, ns)
    kt_w = min(SEL_KT, S)
    win_w = min(WINDOW + NSA_QB, S)
    kcb = kc.reshape(B, HKV, nb, D).astype(BF16)
    vct = vc.reshape(B, HKV, nb, D).transpose(0, 1, 3, 2).astype(BF16)
    gates = small[:, :3 * H].reshape(B, S, HKV, 3 * G).transpose(0, 2, 3, 1)
    m_i = jnp.arange(ns)[:, None]
    n_i = jnp.arange(nb)[None, :]
    overlap_t = ((n_i * CMP_STRIDE <= m_i * SLC_BLOCK + SLC_BLOCK - 1)
                 & (n_i * CMP_STRIDE + CMP_BLOCK - 1 >= m_i * SLC_BLOCK)).astype(BF16)
    per_bh = lambda shape: pl.BlockSpec((1, 1) + shape, lambda b, h, i: (b, h, 0, 0))
    keys = pl.BlockSpec((1, S, NSA_KV), lambda b, h, i: (b, 0, 0))
    keys_aug = pl.BlockSpec((1, S, NSA_KAUG), lambda b, h, i: (b, 0, 0))
    vals = pl.BlockSpec((1, 2 * D, S), lambda b, h, i: (h, 0, b))
    heads = pl.BlockSpec((G, D, NSA_QB), lambda b, h, i: (h, 0, b * nq + i))
    return pl.pallas_call(
        functools.partial(_nsa_kernel, n_sel=n_sel, kt_w=kt_w, win_w=win_w),
        grid=(B, HKV, nq),
        in_specs=[
            heads, per_bh((nb, D)), per_bh((D, nb)),
            pl.BlockSpec((ns, nb), lambda b, h, i: (0, 0)),
            keys_aug, vals, keys, vals,
            pl.BlockSpec((1, 1, 3 * G, NSA_QB), lambda b, h, i: (b, h, 0, i)),
        ],
        out_specs=heads,
        out_shape=jax.ShapeDtypeStruct((H, D, B * S), BF16),
        scratch_shapes=[pltpu.VMEM((ns, NSA_QB), F32)]
        + [pltpu.VMEM((SEL_SUB, G * NSA_QB), F32)] * 2 + [pltpu.VMEM((SEL_SUB, G * NSA_QB), BF16)] * 2
        + [pltpu.VMEM((2 * D, G * NSA_QB), F32)] * 2,
        compiler_params=_cparams("parallel", "parallel", "arbitrary"),
        name="nsa_attention",
    )(qt, kcb, vct, overlap_t, ks.reshape(B, S, NSA_KAUG), vst, kw.reshape(B, S, NSA_KV), vwt, gates)


GDN_TB = 512
GDN_HALO = 8


def _gdn_prep_kernel(qkv_ref, halo_ref, sm_ref, cw_ref, alog_ref, dtb_ref,
                     u_ref, wq_ref, kd_ref, intra_ref, dec_ref, xs_scr, y_scr, bg_scr):
    C, HD, NH, BD = GDN_CHUNK, GDN_HEAD_DIM, GDN_HEADS, GDN_BD
    tb = qkv_ref.shape[1]
    first = pl.program_id(1) == 0
    xs_scr[GDN_HALO:GDN_HALO + tb, :] = qkv_ref[0]
    xs_scr[0:GDN_HALO, :] = jnp.where(first, 0.0, halo_ref[0])
    for cb in range(3 * GDN_W // LANES):
        cols = slice(cb * LANES, (cb + 1) * LANES)
        y = cw_ref[GDN_CONV - 1:GDN_CONV, cols] * xs_scr[GDN_HALO:GDN_HALO + tb, cols]
        for d in range(1, GDN_CONV):
            y = y + cw_ref[GDN_CONV - 1 - d:GDN_CONV - d, cols] * xs_scr[GDN_HALO - d:GDN_HALO - d + tb, cols]
        y_scr[:, cols] = _silu(y)
    sm = sm_ref[0]
    b_logit = sm[:, 3 * NSA_HEADS:3 * NSA_HEADS + NH]
    a_logit = sm[:, 3 * NSA_HEADS + NH:3 * NSA_HEADS + 2 * NH] + dtb_ref[...]
    softplus = jnp.maximum(a_logit, 0.0) + jnp.log(1.0 + jnp.exp(-jnp.abs(a_logit)))
    bg_scr[:, 0:NH] = _sigmoid(b_logit)
    bg_scr[:, NH:2 * NH] = -jnp.exp(alog_ref[...]) * softplus

    r = lax.broadcasted_iota(jnp.int32, (BD, BD), 0)
    c = lax.broadcasted_iota(jnp.int32, (BD, BD), 1)
    same = (r // C) == (c // C)
    eye = r == c
    tril = same & (r >= c)
    triu = same & (r <= c)
    strict = same & (r > c)
    eye_f = jnp.where(eye, 1.0, 0.0)

    def stack_heads(m, c0):
        return jnp.concatenate([m[:, c0 + h * HD:c0 + (h + 1) * HD] for h in range(NH)], axis=0)

    def chunk(ci, carry):
        r0 = pl.multiple_of(ci * C, C)
        yc = y_scr[pl.ds(r0, C), :]
        bg = bg_scr[pl.ds(r0, C), :]
        q4 = stack_heads(yc, 0)
        k4 = stack_heads(yc, GDN_W)
        v4 = stack_heads(yc, 2 * GDN_W)
        q4 = q4 * lax.rsqrt(jnp.sum(q4 * q4, -1, keepdims=True) + NORM_EPS) * (HD ** -0.5)
        k4 = k4 * lax.rsqrt(jnp.sum(k4 * k4, -1, keepdims=True) + NORM_EPS)
        b4 = jnp.concatenate([bg[:, h:h + 1] for h in range(NH)], axis=0)
        g4 = jnp.concatenate([bg[:, NH + h:NH + h + 1] for h in range(NH)], axis=0)
        g_row = jnp.sum(jnp.where(eye, g4, 0.0), axis=0, keepdims=True)
        gc_col = jnp.sum(jnp.where(tril, g_row, 0.0), axis=1, keepdims=True)
        gc_row = jnp.sum(jnp.where(triu, g4, 0.0), axis=0, keepdims=True)
        gl_col = jnp.sum(jnp.where(same, g_row, 0.0), axis=1, keepdims=True)
        decay = jnp.where(tril, jnp.exp(jnp.minimum(gc_col - gc_row, 0.0)), 0.0)
        kb4 = k4 * b4
        k4b = k4.astype(BF16)
        lmat = jnp.where(strict, _dot_nt(kb4.astype(BF16), k4b) * decay, 0.0)
        intra = jnp.where(tril, _dot_nt(q4.astype(BF16), k4b) * decay, 0.0)
        inv = eye_f - lmat
        pw = lmat
        for _ in range(int(math.log2(C)) - 1):
            pwb = pw.astype(BF16)
            pw = _dot(pwb, pwb)
            inv = _dot2(inv, eye_f + pw)
        egc = jnp.exp(gc_col)
        uw = _dot2(inv, jnp.concatenate([v4 * b4, kb4 * egc], axis=1))
        u_ref[0, ci] = uw[:, :HD]
        w4 = uw[:, HD:].astype(BF16).reshape(NH, C, HD)
        qg4 = (q4 * egc).astype(BF16).reshape(NH, C, HD)
        wq_ref[0, ci] = jnp.concatenate([w4, qg4], axis=1)
        kd_ref[0, ci] = (k4 * jnp.exp(gl_col - gc_col)).astype(BF16)
        intra_ref[0, ci] = intra.astype(BF16)
        dec = jnp.broadcast_to(jnp.exp(gl_col), (BD, LANES)).reshape(NH, C, LANES)
        dec_ref[0, ci] = dec[:, 0:8, :]
        return carry

    for ci in range(tb // C):
        chunk(ci, 0)


def _gdn_scan_kernel(u_ref, wq_ref, kd_ref, intra_ref, dec_ref, z_ref, nw_ref, o_ref, s_scr):
    C, HD, NH = GDN_CHUNK, GDN_HEAD_DIM, GDN_HEADS
    nbatch = u_ref.shape[0]

    @pl.when(pl.program_id(0) == 0)
    def _():
        s_scr[...] = jnp.zeros_like(s_scr)

    nw = nw_ref[...]
    for b in range(nbatch):
        vnew, qs = [], []
        for h in range(NH):
            st = s_scr[b * NH + h]
            res = _dot(wq_ref[b, 0, h], st.astype(BF16))
            vnew.append(u_ref[b, 0, h * C:(h + 1) * C, :] - res[:C])
            qs.append(res[C:])
        vnew4 = jnp.concatenate(vnew, axis=0).astype(BF16)
        o4 = jnp.concatenate(qs, axis=0) + _dot(intra_ref[b, 0], vnew4)
        for h in range(NH):
            rows = slice(h * C, (h + 1) * C)
            st = s_scr[b * NH + h]
            s_scr[b * NH + h] = st * dec_ref[b, 0, h, 0:1, :] + _dot_tn(kd_ref[b, 0, rows, :], vnew4[rows])
            oh = o4[rows]
            oh = oh * lax.rsqrt(jnp.mean(oh * oh, -1, keepdims=True) + NORM_EPS) * nw
            o_ref[b, :, h * HD:(h + 1) * HD] = oh * _silu(z_ref[b, :, h * HD:(h + 1) * HD])


def _gdn_mixer(gqkv, z, small, conv_w, a_log, dt_bias, norm_w, B, S):
    C, HD, NH, BD = GDN_CHUNK, GDN_HEAD_DIM, GDN_HEADS, GDN_BD
    nc = S // C
    tb = min(GDN_TB, S)
    cb = tb // C
    qkv3 = gqkv.reshape(B, S, 3 * GDN_W)
    sm3 = small.reshape(B, S, LANES)
    hb = tb // GDN_HALO
    u, wq, kd, intra, dec = pl.pallas_call(
        _gdn_prep_kernel,
        grid=(B, S // tb),
        in_specs=[
            pl.BlockSpec((1, tb, 3 * GDN_W), lambda b, i: (b, i, 0)),
            pl.BlockSpec((1, GDN_HALO, 3 * GDN_W), lambda b, i: (b, jnp.maximum(i * hb - 1, 0), 0)),
            pl.BlockSpec((1, tb, LANES), lambda b, i: (b, i, 0)),
            pl.BlockSpec((GDN_CONV, 3 * GDN_W), lambda b, i: (0, 0)),
            pl.BlockSpec((1, NH), lambda b, i: (0, 0)),
            pl.BlockSpec((1, NH), lambda b, i: (0, 0)),
        ],
        out_specs=[
            pl.BlockSpec((1, cb, BD, HD), lambda b, i: (b, i, 0, 0)),
            pl.BlockSpec((1, cb, NH, 2 * C, HD), lambda b, i: (b, i, 0, 0, 0)),
            pl.BlockSpec((1, cb, BD, HD), lambda b, i: (b, i, 0, 0)),
            pl.BlockSpec((1, cb, BD, BD), lambda b, i: (b, i, 0, 0)),
            pl.BlockSpec((1, cb, NH, 8, LANES), lambda b, i: (b, i, 0, 0, 0)),
        ],
        out_shape=[
            jax.ShapeDtypeStruct((B, nc, BD, HD), F32),
            jax.ShapeDtypeStruct((B, nc, NH, 2 * C, HD), BF16),
            jax.ShapeDtypeStruct((B, nc, BD, HD), BF16),
            jax.ShapeDtypeStruct((B, nc, BD, BD), BF16),
            jax.ShapeDtypeStruct((B, nc, NH, 8, LANES), F32),
        ],
        scratch_shapes=[
            pltpu.VMEM((GDN_HALO + tb, 3 * GDN_W), F32),
            pltpu.VMEM((tb, 3 * GDN_W), F32),
            pltpu.VMEM((tb, 2 * NH), F32),
        ],
        compiler_params=_cparams("parallel", "parallel"),
        name="gdn_prepare",
    )(qkv3, qkv3, sm3, conv_w, a_log.reshape(1, NH), dt_bias.reshape(1, NH))

    o = pl.pallas_call(
        _gdn_scan_kernel,
        grid=(nc,),
        in_specs=[
            pl.BlockSpec((B, 1, BD, HD), lambda c: (0, c, 0, 0)),
            pl.BlockSpec((B, 1, NH, 2 * C, HD), lambda c: (0, c, 0, 0, 0)),
            pl.BlockSpec((B, 1, BD, HD), lambda c: (0, c, 0, 0)),
            pl.BlockSpec((B, 1, BD, BD), lambda c: (0, c, 0, 0)),
            pl.BlockSpec((B, 1, NH, 8, LANES), lambda c: (0, c, 0, 0, 0)),
            pl.BlockSpec((B, C, GDN_W), lambda c: (0, c, 0)),
            pl.BlockSpec((1, HD), lambda c: (0, 0)),
        ],
        out_specs=pl.BlockSpec((B, C, GDN_W), lambda c: (0, c, 0)),
        out_shape=jax.ShapeDtypeStruct((B, S, GDN_W), F32),
        scratch_shapes=[pltpu.VMEM((B * NH, HD, HD), F32)],
        compiler_params=_cparams("arbitrary"),
        name="gdn_scan",
    )(u, wq, kd, intra, dec, z.reshape(B, S, GDN_W), norm_w.reshape(1, HD))
    return o.reshape(B * S, GDN_W)


MIX_TM = 512


def _mix_ln_kernel(x_ref, at_ref, b_ref, w_ref, g_ref, be_ref, o_ref):
    a_t = at_ref[...].reshape(NSA_Q, x_ref.shape[0])
    mix = (_dot_tn(a_t, w_ref[0:NSA_Q, :])
           + _dot(b_ref[...].astype(BF16), w_ref[NSA_Q:NSA_Q + GDN_W, :]))
    o_ref[...] = _layer_norm(DN_ALPHA * x_ref[...] + mix, g_ref[...], be_ref[...])


def _mix_out_ln(x2d, o_at, o_b, w_out, g, b):
    T = x2d.shape[0]
    row = lambda i: (i, 0)
    const = lambda i: (0, 0)
    return pl.pallas_call(
        _mix_ln_kernel,
        grid=(T // MIX_TM,),
        in_specs=[
            pl.BlockSpec((MIX_TM, D_MODEL), row),
            pl.BlockSpec((NSA_HEADS, NSA_HEAD_DIM, MIX_TM), lambda i: (0, 0, i)),
            pl.BlockSpec((MIX_TM, GDN_W), row), pl.BlockSpec((NSA_Q + GDN_W, D_MODEL), const),
            pl.BlockSpec((1, D_MODEL), const), pl.BlockSpec((1, D_MODEL), const),
        ],
        out_specs=pl.BlockSpec((MIX_TM, D_MODEL), row),
        out_shape=jax.ShapeDtypeStruct((T, D_MODEL), F32),
        compiler_params=_cparams("parallel"),
        name="mix_out_ln",
    )(x2d, o_at, o_b, w_out.astype(BF16), g.reshape(1, D_MODEL), b.reshape(1, D_MODEL))


FFN_TM = 512
FFN_TF = 256


def _ffn_ln_kernel(x_ref, wg_ref, wu_ref, wd_ref, g_ref, be_ref, o_ref, hid_scr):
    x = x_ref[...]
    xb = x.astype(BF16)
    for j in range(hid_scr.shape[1] // FFN_TF):
        cols = slice(j * FFN_TF, (j + 1) * FFN_TF)
        hid_scr[:, cols] = (_silu(_dot(xb, wg_ref[:, cols])) * _dot(xb, wu_ref[:, cols])).astype(BF16)
    o_ref[...] = _layer_norm(DN_ALPHA * x + _dot(hid_scr[...], wd_ref[...]), g_ref[...], be_ref[...])


def _ffn_ln(x2d, wg, wu, wd, g, b):
    T = x2d.shape[0]
    dff = wg.shape[1]
    tm = min(FFN_TM, T)
    const = lambda i: (0, 0)
    resident = lambda shape: pl.BlockSpec(shape, const, pipeline_mode=pl.Buffered(1))
    return pl.pallas_call(
        _ffn_ln_kernel,
        grid=(T // tm,),
        in_specs=[
            pl.BlockSpec((tm, D_MODEL), lambda i: (i, 0)),
            resident((D_MODEL, dff)), resident((D_MODEL, dff)), resident((dff, D_MODEL)),
            pl.BlockSpec((1, D_MODEL), const), pl.BlockSpec((1, D_MODEL), const),
        ],
        out_specs=pl.BlockSpec((tm, D_MODEL), lambda i: (i, 0)),
        out_shape=jax.ShapeDtypeStruct((T, D_MODEL), F32),
        scratch_shapes=[pltpu.VMEM((tm, dff), BF16)],
        compiler_params=_cparams("parallel"),
        name="ffn_ln",
    )(x2d, wg.astype(BF16), wu.astype(BF16), wd.astype(BF16), g.reshape(1, D_MODEL), b.reshape(1, D_MODEL))


POOL_TM = 512
POOL_HALO = 16


def _route_top2(x1, rw):
    lane = lax.broadcasted_iota(jnp.int32, (x1.shape[0], LANES), 1)
    lanef = lane.astype(F32)
    logits = jnp.where(lane < N_EXPERTS, _dot3(x1, rw), NEG)
    ex = jnp.exp(logits - jnp.max(logits, -1, keepdims=True))
    probs = jnp.where(lane < N_EXPERTS, ex / jnp.sum(ex, -1, keepdims=True), -1.0)
    p1 = jnp.max(probs, -1, keepdims=True)
    i1 = jnp.min(jnp.where(probs == p1, lanef, float(LANES)), -1, keepdims=True)
    rest = jnp.where(lanef == i1, -1.0, probs)
    p2 = jnp.max(rest, -1, keepdims=True)
    i2 = jnp.min(jnp.where(rest == p2, lanef, float(LANES)), -1, keepdims=True)
    den = p1 + p2
    return jnp.where(lane == 0, i1, jnp.where(lane == 1, i2, jnp.where(
        lane == 2, p1 / den, jnp.where(lane == 3, p2 / den, 0.0))))


def _pool_ln_kernel(x_ref, halo_ref, pw_ref, ps_ref, g_ref, be_ref, rw_ref, o_ref, route_ref, xs_scr):
    tm = x_ref.shape[1]
    i = pl.program_id(1)
    x = x_ref[0]
    xs_scr[POOL_HALO:POOL_HALO + tm, :] = x
    xs_scr[0:POOL_HALO, :] = jnp.where(i == 0, 0.0, halo_ref[0])
    t1 = (i * tm + 1 + lax.broadcasted_iota(jnp.int32, (tm, 1), 0)).astype(F32)
    for gi, win in enumerate(POOL_SIZES):
        cols = slice(gi * POOL_GROUP, (gi + 1) * POOL_GROUP)
        acc = x[:, cols]
        for d in range(1, win):
            acc = acc + xs_scr[POOL_HALO - d:POOL_HALO - d + tm, cols]
        mean = acc / jnp.minimum(t1, float(win))
        mix = _dot((mean - x[:, cols]).astype(BF16), pw_ref[gi]) * ps_ref[:, cols]
        xs_scr[POOL_HALO:POOL_HALO + tm, cols] = DN_ALPHA * x[:, cols] + mix
    x1 = _layer_norm(xs_scr[POOL_HALO:POOL_HALO + tm, :], g_ref[...], be_ref[...])
    o_ref[0] = x1
    route_ref[0] = _route_top2(x1, rw_ref[...])


def _pool_ln(x3d, pool_w, pool_scale, g, b, router_w):
    B, S, _ = x3d.shape
    tm = min(POOL_TM, S)
    hb = tm // POOL_HALO
    const = lambda bb, i: (0, 0)
    rw = jnp.concatenate([router_w, jnp.zeros((D_MODEL, LANES - N_EXPERTS), F32)], axis=1)
    return pl.pallas_call(
        _pool_ln_kernel,
        grid=(B, S // tm),
        in_specs=[
            pl.BlockSpec((1, tm, D_MODEL), lambda bb, i: (bb, i, 0)),
            pl.BlockSpec((1, POOL_HALO, D_MODEL), lambda bb, i: (bb, jnp.maximum(i * hb - 1, 0), 0)),
            pl.BlockSpec((len(POOL_SIZES), POOL_GROUP, POOL_GROUP), lambda bb, i: (0, 0, 0)),
            pl.BlockSpec((1, D_MODEL), const), pl.BlockSpec((1, D_MODEL), const),
            pl.BlockSpec((1, D_MODEL), const), pl.BlockSpec((D_MODEL, LANES), const),
        ],
        out_specs=[pl.BlockSpec((1, tm, D_MODEL), lambda bb, i: (bb, i, 0)),
                   pl.BlockSpec((1, tm, LANES), lambda bb, i: (bb, i, 0))],
        out_shape=[jax.ShapeDtypeStruct((B, S, D_MODEL), F32), jax.ShapeDtypeStruct((B, S, LANES), F32)],
        scratch_shapes=[pltpu.VMEM((POOL_HALO + tm, D_MODEL), F32)],
        compiler_params=_cparams("parallel", "parallel"),
        name="pool_ln",
    )(x3d, x3d, pool_w.astype(BF16), pool_scale.reshape(1, D_MODEL), g.reshape(1, D_MODEL),
      b.reshape(1, D_MODEL), rw)


MOE_TM = 1024
MOE_TF = 512
MOE_TC = 512


def _moe_plan(route, T, tm):
    ne = N_EXPERTS
    e_flat = jnp.concatenate([route[:, 0], route[:, 1]]).astype(jnp.int32)
    onehot = (e_flat[:, None] == jnp.arange(ne)[None, :]).astype(jnp.int32)
    csum = jnp.cumsum(onehot, axis=0)
    rank = jnp.sum(onehot * csum, axis=1) - 1
    counts = csum[-1]
    padded = (counts + tm - 1) // tm * tm
    gend = jnp.cumsum(padded)
    gstart = gend - padded
    pos = gstart[e_flat] + rank
    n_tiles = (2 * T) // tm + ne
    tile_e = jnp.minimum(jnp.searchsorted(gend, jnp.arange(n_tiles) * tm, side="right"), ne - 1)
    tile_e = tile_e.astype(jnp.int32)
    n_used = (gend[-1] // tm).astype(jnp.int32).reshape(1)
    order = jnp.argsort(e_flat, stable=True)
    cstart = jnp.cumsum(counts) - counts
    e_p = jnp.repeat(tile_e, tm)
    r = jnp.clip(jnp.arange(n_tiles * tm) - gstart[e_p], 0, jnp.maximum(counts[e_p] - 1, 0))
    src = (order[jnp.clip(cstart[e_p] + r, 0, 2 * T - 1)] % T).astype(jnp.int32)
    return pos.astype(jnp.int32), src, tile_e, n_used, n_tiles


SUBLANES = 8


def _gather_rows(src_hbm, idx, dst, sem, alternate=False):
    def issue(blk, c):
        for u in range(SUBLANES):
            row = idx(blk * SUBLANES + u)
            pltpu.make_async_copy(src_hbm.at[row >> 3, pl.ds(row & (SUBLANES - 1), 1)],
                                  dst.at[blk, pl.ds(u, 1)], sem).start(priority=u % 2 if alternate else 0)
        return c

    lax.fori_loop(0, dst.shape[0], issue, 0)


def _gather_wait(src_hbm, dst, sem):
    pltpu.make_async_copy(src_hbm.at[pl.ds(0, dst.shape[0])], dst, sem).wait()


def _moe_expert_kernel(te_ref, nu_ref, src_ref, nxt_ref, x_hbm, wg_ref, wu_ref, wd_ref, o_ref,
                       xbuf, xb_scr, acc_scr, sem):
    i = pl.program_id(0)
    j = pl.program_id(1)
    n_used = nu_ref[0]
    used = i < n_used
    slot = i % 2

    @pl.when((i == 0) & (j == 0))
    def _():
        _gather_rows(x_hbm, lambda r: src_ref[0, 0, r], xbuf.at[0], sem.at[0])

    @pl.when((i + 1 < n_used) & (j == 0))
    def _():
        _gather_rows(x_hbm, lambda r: nxt_ref[0, 0, r], xbuf.at[1 - slot], sem.at[1 - slot])

    @pl.when(used & (j == 0))
    def _():
        _gather_wait(x_hbm, xbuf.at[slot], sem.at[slot])
        xb_scr[...] = xbuf[slot].reshape(xb_scr.shape).astype(BF16)
        acc_scr[...] = jnp.zeros_like(acc_scr)

    @pl.when(used)
    def _():
        xb = xb_scr[...]
        hid = _silu(_dot(xb, wg_ref[0])) * _dot(xb, wu_ref[0])
        acc_scr[...] += _dot(hid.astype(BF16), wd_ref[0])

    last = j == pl.num_programs(1) - 1

    @pl.when(used & last)
    def _():
        o_ref[...] = acc_scr[...]

    @pl.when(jnp.logical_not(used) & last)
    def _():
        o_ref[...] = jnp.zeros_like(o_ref)


def _moe_combine_kernel(pos_ref, nxt_ref, x_ref, route_ref, ys_hbm, g_ref, be_ref, o_ref, ybuf, sem):
    i = pl.program_id(0)
    slot = i % 2

    def fetch(p_ref, s):
        for k in range(2):
            _gather_rows(ys_hbm, lambda r: p_ref[0, k, r], ybuf.at[s, k], sem.at[s], alternate=True)

    @pl.when(i == 0)
    def _():
        fetch(pos_ref, 0)

    @pl.when(i + 1 < pl.num_programs(0))
    def _():
        fetch(nxt_ref, 1 - slot)

    for k in range(2):
        _gather_wait(ys_hbm, ybuf.at[slot, k], sem.at[slot])
    route = route_ref[...]
    y0 = ybuf[slot, 0].reshape(x_ref.shape)
    y1 = ybuf[slot, 1].reshape(x_ref.shape)
    y = DN_ALPHA * x_ref[...] + route[:, 2:3] * y0 + route[:, 3:4] * y1
    o_ref[...] = _layer_norm(y, g_ref[...], be_ref[...])


def _moe_ln(x2d, route, wg, wu, wd, g, b):
    T = x2d.shape[0]
    ne, _, dff = wg.shape
    tm = min(MOE_TM, T)
    tc = min(MOE_TC, T)
    pos, src, tile_e, n_used, n_tiles = _moe_plan(route, T, tm)

    nj = dff // MOE_TF

    def w_col(i, j, te, nu):
        return (te[i], 0, jnp.where(i < nu[0], j, 0))

    def w_row(i, j, te, nu):
        return (te[i], jnp.where(i < nu[0], j, 0), 0)

    src3 = src.reshape(n_tiles, 1, tm)
    ys = pl.pallas_call(
        _moe_expert_kernel,
        grid_spec=pltpu.PrefetchScalarGridSpec(
            num_scalar_prefetch=2,
            grid=(n_tiles, nj),
            in_specs=[
                pl.BlockSpec((1, 1, tm), lambda i, j, te, nu: (i, 0, 0), memory_space=pltpu.SMEM),
                pl.BlockSpec((1, 1, tm), lambda i, j, te, nu: (jnp.minimum(i + 1, n_tiles - 1), 0, 0),
                             memory_space=pltpu.SMEM),
                pl.BlockSpec(memory_space=pl.ANY),
                pl.BlockSpec((1, D_MODEL, MOE_TF), w_col),
                pl.BlockSpec((1, D_MODEL, MOE_TF), w_col),
                pl.BlockSpec((1, MOE_TF, D_MODEL), w_row),
            ],
            out_specs=pl.BlockSpec((tm, D_MODEL), lambda i, j, te, nu: (i, 0)),
            scratch_shapes=[pltpu.VMEM((2, tm // SUBLANES, SUBLANES, D_MODEL), F32),
                            pltpu.VMEM((tm, D_MODEL), BF16), pltpu.VMEM((tm, D_MODEL), F32),
                            pltpu.SemaphoreType.DMA((2,))],
        ),
        out_shape=jax.ShapeDtypeStruct((n_tiles * tm, D_MODEL), F32),
        compiler_params=_cparams("arbitrary", "arbitrary"),
        name="moe_experts",
    )(tile_e, n_used, src3, src3, x2d.reshape(T // SUBLANES, SUBLANES, D_MODEL), wg.astype(BF16),
      wu.astype(BF16), wd.astype(BF16))

    n_steps = T // tc
    pos3 = pos.reshape(2, n_steps, tc).transpose(1, 0, 2)
    row = lambda i: (i, 0)
    const = lambda i: (0, 0)
    return pl.pallas_call(
        _moe_combine_kernel,
        grid=(n_steps,),
        in_specs=[
            pl.BlockSpec((1, 2, tc), lambda i: (i, 0, 0), memory_space=pltpu.SMEM),
            pl.BlockSpec((1, 2, tc), lambda i: (jnp.minimum(i + 1, n_steps - 1), 0, 0),
                         memory_space=pltpu.SMEM),
            pl.BlockSpec((tc, D_MODEL), row), pl.BlockSpec((tc, LANES), row),
            pl.BlockSpec(memory_space=pl.ANY),
            pl.BlockSpec((1, D_MODEL), const), pl.BlockSpec((1, D_MODEL), const),
        ],
        out_specs=pl.BlockSpec((tc, D_MODEL), row),
        out_shape=jax.ShapeDtypeStruct((T, D_MODEL), F32),
        scratch_shapes=[pltpu.VMEM((2, 2, tc // SUBLANES, SUBLANES, D_MODEL), F32),
                        pltpu.SemaphoreType.DMA((2,))],
        compiler_params=_cparams("arbitrary"),
        name="moe_combine_ln",
    )(pos3, pos3, x2d, route, ys.reshape(n_tiles * tm // SUBLANES, SUBLANES, D_MODEL),
      g.reshape(1, D_MODEL), b.reshape(1, D_MODEL))


def _even_layer(x2d, B, S, w_in, pe_k, w_k, pe_v, w_v, conv_w, a_log, dt_bias, gdn_norm, w_out,
                ln1_g, ln1_b, wg, wu, wd, ln2_g, ln2_b):
    qt, vst, vwt, ks, kw, cmp2d, gqkv, z, small = _in_projection(x2d, w_in, S)
    kc, vc = _compress(cmp2d, pe_k, w_k, pe_v, w_v, B, S)
    o_a = _nsa_attention(qt, ks, vst, kw, vwt, kc, vc, small, B, S)
    o_b = _gdn_mixer(gqkv, z, small, conv_w, a_log, dt_bias, gdn_norm, B, S)
    x2d = _mix_out_ln(x2d, o_a, o_b, w_out, ln1_g, ln1_b)
    return _ffn_ln(x2d, wg, wu, wd, ln2_g, ln2_b)


def _odd_layer(x2d, B, S, pool_w, pool_scale, ln1_g, ln1_b, router_w, wg, wu, wd, ln2_g, ln2_b):
    x3d, route = _pool_ln(x2d.reshape(B, S, D_MODEL), pool_w, pool_scale, ln1_g, ln1_b, router_w)
    return _moe_ln(x3d.reshape(B * S, D_MODEL), route.reshape(B * S, LANES), wg, wu, wd, ln2_g, ln2_b)


def kernel(x, ev_w_in, ev_cmp_pe_k, ev_cmp_w_k, ev_cmp_pe_v, ev_cmp_w_v, ev_conv_w, ev_a_log, ev_dt_bias,
           ev_gdn_norm, ev_w_out, ev_ln1_g, ev_ln1_b, ev_ffn_wg, ev_ffn_wu, ev_ffn_wd, ev_ln2_g, ev_ln2_b,
           od_pool_w, od_pool_scale, od_ln1_g, od_ln1_b, od_router_w, od_exp_wg, od_exp_wu, od_exp_wd,
           od_ln2_g, od_ln2_b):
    B, S, _ = x.shape
    h = x.reshape(B * S, D_MODEL)
    for layer in range(DEPTH):
        i = layer // 2
        if layer % 2 == 0:
            h = _even_layer(h, B, S, ev_w_in[i], ev_cmp_pe_k[i], ev_cmp_w_k[i], ev_cmp_pe_v[i],
                            ev_cmp_w_v[i], ev_conv_w[i], ev_a_log[i], ev_dt_bias[i], ev_gdn_norm[i],
                            ev_w_out[i], ev_ln1_g[i], ev_ln1_b[i], ev_ffn_wg[i], ev_ffn_wu[i],
                            ev_ffn_wd[i], ev_ln2_g[i], ev_ln2_b[i])
        else:
            h = _odd_layer(h, B, S, od_pool_w[i], od_pool_scale[i], od_ln1_g[i], od_ln1_b[i],
                           od_router_w[i], od_exp_wg[i], od_exp_wu[i], od_exp_wd[i], od_ln2_g[i],
                           od_ln2_b[i])
    return h.reshape(B, S, D_MODEL)
```
